```python
import math
import jax
import jax.numpy as jnp
from jax import lax
import numpy as np

D_MODEL = 1024
BATCH = 8
SEQ = 4096
DEPTH = 1

D_MIX = D_MODEL
D_ATTN = D_MIX // 2
D_SSM = D_MIX - D_ATTN
ATTN_HEADS = 4
ATTN_QK_DIM = D_ATTN // (2 * ATTN_HEADS)
ATTN_V_DIM = D_ATTN // ATTN_HEADS
SSM_GROUP_WIDTH = 16
SSM_GROUPS = D_SSM // SSM_GROUP_WIDTH
SSM_STATE = 64
D_IN_PROJ = 3 * D_ATTN + D_SSM
N_EXPERTS = 32
TOP_K = 4
D_FF = D_MODEL
SWIGLU_LIMIT = 7.0
SWIGLU_ALPHA = 1.702
Q_BLOCK = 128
MOE_BLOCK = 128
RMS_EPS = 1e-6
DT_MIN = 1e-3
DT_MAX = 1e-1
MASK_VALUE = -1e30

kernel_name = 'hybrid_diffattn_s5_moe_block'


def _rmsnorm(x, g):
    xf = x.astype(jnp.float32)
    y = xf * lax.rsqrt(jnp.mean(xf * xf, axis=-1, keepdims=True) + RMS_EPS)
    return (y * g.astype(jnp.float32)).astype(x.dtype)


def _modulate(h, shift, scale):
    return h * (1.0 + scale[:, None, :]) + shift[:, None, :]


def _diff_attention(q, k, v, lam, lambda_init, subln_g):
    b, s, h, _, dk = q.shape
    dv = v.shape[-1]
    n_blk = s // Q_BLOCK
    q_blocks = q.reshape(b, n_blk, Q_BLOCK, h, 2, dk).transpose(1, 0, 2, 3, 4, 5)
    key_pos = jnp.arange(s)
    scale = dk ** -0.5

    def one_block(args):
        q_blk, i = args
        sc = jnp.einsum('bqhmd,bkhmd->bhmqk', q_blk, k).astype(jnp.float32) * scale
        q_pos = i * Q_BLOCK + jnp.arange(Q_BLOCK)
        causal = key_pos[None, :] <= q_pos[:, None]
        p = jax.nn.softmax(jnp.where(causal, sc, MASK_VALUE), axis=-1)
        att = p[:, :, 0] - lam * p[:, :, 1]
        return jnp.einsum('bhqk,bkhe->bqhe', att.astype(v.dtype), v)

    out = lax.map(one_block, (q_blocks, jnp.arange(n_blk)))
    out = out.transpose(1, 0, 2, 3, 4).reshape(b, s, h, dv)
    out = _rmsnorm(out, subln_g) * (1.0 - lambda_init)
    return out.reshape(b, s, h * dv)


def _s5(u, a_re, a_im, log_dt, b_re, b_im, c_re, c_im, d_skip, w_glu, b_glu):
    bsz, s, _ = u.shape
    f32 = jnp.float32
    ug = u.reshape(bsz, s, SSM_GROUPS, SSM_GROUP_WIDTH).astype(f32)
    dt = jnp.exp(log_dt.astype(f32))[:, None]
    lam = lax.complex(jnp.minimum(a_re.astype(f32), -1e-4), a_im.astype(f32))
    lam_bar = jnp.exp(lam * dt)
    b_cplx = lax.complex(b_re.astype(f32), b_im.astype(f32))
    b_bar = ((lam_bar - 1.0) / lam)[..., None] * b_cplx
    bu = jnp.einsum('bsgc,gpc->bsgp', ug.astype(jnp.complex64), b_bar)
    a = jnp.broadcast_to(lam_bar[None, None], (1, s, SSM_GROUPS, SSM_STATE))

    def combine(e_i, e_j):
        a_i, x_i = e_i
        a_j, x_j = e_j
        return a_j * a_i, a_j * x_i + x_j

    _, states = lax.associative_scan(combine, (a, bu), axis=1)
    c_cplx = lax.complex(c_re.astype(f32), c_im.astype(f32))
    y = jnp.einsum('bsgp,gcp->bsgc', states, c_cplx).real
    y = y + d_skip.astype(f32).reshape(SSM_GROUPS, SSM_GROUP_WIDTH) * ug
    y = y.reshape(bsz, s, D_SSM).astype(u.dtype)
    z = jax.nn.gelu(y)
    return z * jax.nn.sigmoid(z @ w_glu + b_glu)


def _moe(h, w_router, b_router, w1, b1, w2, b2):
    bsz, s, d = h.shape
    n_tok = bsz * s
    hf = h.reshape(n_tok, d)
    logits = (hf @ w_router + b_router).astype(jnp.float32)
    top_val, top_idx = lax.top_k(logits, TOP_K)
    gates = jax.nn.softmax(top_val, axis=-1)
    m = n_tok * TOP_K
    flat_e = top_idx.reshape(m)
    flat_tok = jnp.arange(m, dtype=jnp.int32) // TOP_K
    flat_gate = gates.reshape(m)
    order = jnp.argsort(flat_e)
    sorted_e = flat_e[order]
    counts = jnp.bincount(flat_e, length=N_EXPERTS)
    padded = ((counts + MOE_BLOCK - 1) // MOE_BLOCK) * MOE_BLOCK
    pend = jnp.cumsum(padded)
    pstart = pend - padded
    ustart = jnp.cumsum(counts) - counts
    dest = pstart[sorted_e] + (jnp.arange(m) - ustart[sorted_e])
    n_blocks = -(-m // MOE_BLOCK) + N_EXPERTS
    n_rows = n_blocks * MOE_BLOCK
    row_tok = jnp.zeros((n_rows,), jnp.int32).at[dest].set(flat_tok[order])
    row_gate = jnp.zeros((n_rows,), jnp.float32).at[dest].set(flat_gate[order])
    block_e = jnp.minimum(
        jnp.searchsorted(pend, jnp.arange(n_blocks) * MOE_BLOCK, side='right'),
        N_EXPERTS - 1)
    x_blocks = hf[row_tok].reshape(n_blocks, MOE_BLOCK, d)

    def expert_block(args):
        xb, e = args
        gu = xb @ w1[e] + b1[e]
        gate, up = jnp.split(gu, 2, axis=-1)
        gate = jnp.minimum(gate, SWIGLU_LIMIT)
        up = jnp.clip(up, -SWIGLU_LIMIT, SWIGLU_LIMIT)
        glu = gate * jax.nn.sigmoid(SWIGLU_ALPHA * gate)
        return ((up + 1.0) * glu) @ w2[e] + b2[e]

    y_rows = lax.map(expert_block, (x_blocks, block_e)).reshape(n_rows, d)
    y_rows = y_rows * row_gate[:, None].astype(y_rows.dtype)
    y = jnp.zeros((n_tok, d), h.dtype).at[row_tok].add(y_rows.astype(h.dtype))
    return y.reshape(bsz, s, d)


def setup_inputs(seed: int = 0) -> dict:
    key = jax.random.key(seed)
    ks = jax.random.split(key, 32)
    f32 = jnp.float32
    L, D, G, P, W = DEPTH, D_MODEL, SSM_GROUPS, SSM_STATE, SSM_GROUP_WIDTH

    def nrm(k, shape, std):
        return jax.random.normal(k, shape, f32) * std

    a_im_base = jnp.pi * jnp.arange(P, dtype=f32)[None, None, :]
    return {
        'x': nrm(ks[0], (BATCH, SEQ, D), 1.0),
        'c': nrm(ks[1], (BATCH, D), 1.0),
        'w_ada': nrm(ks[2], (L, D, 6 * D), 0.5 * D ** -0.5),
        'b_ada': nrm(ks[3], (L, 6 * D), 0.01),
        'norm1_g': 1.0 + nrm(ks[4], (L, D), 0.01),
        'w_in': nrm(ks[5], (L, D, D_IN_PROJ), D ** -0.5),
        'lq1': nrm(ks[6], (L, ATTN_QK_DIM), 0.1),
        'lk1': nrm(ks[7], (L, ATTN_QK_DIM), 0.1),
        'lq2': nrm(ks[8], (L, ATTN_QK_DIM), 0.1),
        'lk2': nrm(ks[9], (L, ATTN_QK_DIM), 0.1),
        'subln_g': 1.0 + nrm(ks[10], (L, ATTN_V_DIM), 0.01),
        'ssm_a_re': -0.5 + nrm(ks[11], (L, G, P), 0.01),
        'ssm_a_im': a_im_base + nrm(ks[12], (L, G, P), 0.01),
        'ssm_log_dt': jax.random.uniform(ks[13], (L, G), f32,
                                         minval=math.log(DT_MIN), maxval=math.log(DT_MAX)),
        'ssm_b_re': nrm(ks[14], (L, G, P, W), (2 * W) ** -0.5),
        'ssm_b_im': nrm(ks[15], (L, G, P, W), (2 * W) ** -0.5),
        'ssm_c_re': nrm(ks[16], (L, G, W, P), (2 * P) ** -0.5),
        'ssm_c_im': nrm(ks[17], (L, G, W, P), (2 * P) ** -0.5),
        'ssm_d': nrm(ks[18], (L, D_SSM), 1.0),
        'w_glu': nrm(ks[19], (L, D_SSM, D_SSM), D_SSM ** -0.5),
        'b_glu': nrm(ks[20], (L, D_SSM), 0.01),
        'w_out': nrm(ks[21], (L, D_MIX, D), D_MIX ** -0.5),
        'norm2_g': 1.0 + nrm(ks[22], (L, D), 0.01),
        'w_router': nrm(ks[23], (L, D, N_EXPERTS), D ** -0.5),
        'b_router': nrm(ks[24], (L, N_EXPERTS), 0.01),
        'w1': nrm(ks[25], (L, N_EXPERTS, D, 2 * D_FF), D ** -0.5),
        'b1': nrm(ks[26], (L, N_EXPERTS, 2 * D_FF), 0.01),
        'w2': nrm(ks[27], (L, N_EXPERTS, D_FF, D), D_FF ** -0.5),
        'b2': nrm(ks[28], (L, N_EXPERTS, D), 0.01),
        'final_g': 1.0 + nrm(ks[29], (D,), 0.01),
    }


def reference(x, c, w_ada, b_ada, norm1_g, w_in, lq1, lk1, lq2, lk2, subln_g,
              ssm_a_re, ssm_a_im, ssm_log_dt, ssm_b_re, ssm_b_im, ssm_c_re, ssm_c_im,
              ssm_d, w_glu, b_glu, w_out, norm2_g, w_router, b_router, w1, b1, w2, b2,
              final_g):
    bsz, s, _ = x.shape
    c_act = jax.nn.silu(c)
    for l in range(DEPTH):
        lambda_init = 0.8 - 0.6 * math.exp(-0.3 * l)
        mod = c_act @ w_ada[l] + b_ada[l]
        sh1, sc1, g1, sh2, sc2, g2 = jnp.split(mod, 6, axis=-1)

        h = _modulate(_rmsnorm(x, norm1_g[l]), sh1, sc1)
        proj = h @ w_in[l]
        q, k, v, u = jnp.split(proj, [D_ATTN, 2 * D_ATTN, 3 * D_ATTN], axis=-1)
        q = q.reshape(bsz, s, ATTN_HEADS, 2, ATTN_QK_DIM)
        k = k.reshape(bsz, s, ATTN_HEADS, 2, ATTN_QK_DIM)
        v = v.reshape(bsz, s, ATTN_HEADS, ATTN_V_DIM)
        lam = (jnp.exp(jnp.sum(lq1[l].astype(jnp.float32) * lk1[l].astype(jnp.float32)))
               - jnp.exp(jnp.sum(lq2[l].astype(jnp.float32) * lk2[l].astype(jnp.float32)))
               + lambda_init)
        attn_out = _diff_attention(q, k, v, lam, lambda_init, subln_g[l])
        ssm_out = _s5(u, ssm_a_re[l], ssm_a_im[l], ssm_log_dt[l], ssm_b_re[l], ssm_b_im[l],
                      ssm_c_re[l], ssm_c_im[l], ssm_d[l], w_glu[l], b_glu[l])
        mix = jnp.concatenate([attn_out, ssm_out.astype(attn_out.dtype)], axis=-1) @ w_out[l]
        x = x + g1[:, None, :] * mix

        h = _modulate(_rmsnorm(x, norm2_g[l]), sh2, sc2)
        x = x + g2[:, None, :] * _moe(h, w_router[l], b_router[l], w1[l], b1[l], w2[l], b2[l])
    return _rmsnorm(x, final_g)
```

```python
import functools
import math

import jax
import jax.numpy as jnp
from jax import lax
from jax.experimental import pallas as pl
from jax.experimental.pallas import tpu as pltpu

F32 = jnp.float32
BF16 = jnp.bfloat16
HIGHEST = lax.Precision.HIGHEST

RMS_EPS = 1e-6
MASK_VALUE = -1e30
ATTN_HEADS = 4
QK_DIM = 64
V_DIM = 128
SSM_GROUP_WIDTH = 16
SSM_STATE = 64
SSM_CHUNK = 32
N_EXPERTS = 32
TOP_K = 4
SWIGLU_LIMIT = 7.0
SWIGLU_ALPHA = 1.702
LAMBDA_INIT = 0.8 - 0.6 * math.exp(-0.3 * 0)

VMEM_LIMIT = 56 * 1024 * 1024


def _cparams(sem, vmem=None):
    return pltpu.CompilerParams(dimension_semantics=sem, vmem_limit_bytes=vmem)


def _sigmoid(x):
    return 1.0 / (1.0 + jnp.exp(-x))


def _rms(x, g):
    ms = jnp.mean(x * x, axis=-1, keepdims=True)
    return x * lax.rsqrt(ms + RMS_EPS) * g


def _adaln_kernel(c_ref, w_ref, b_ref, o_ref):
    c = c_ref[...]
    ca = c * _sigmoid(c)
    o_ref[...] = jnp.dot(ca, w_ref[...], preferred_element_type=F32, precision=HIGHEST) + b_ref[...]


def _adaln(c, w, b):
    bsz, d = c.shape
    n = w.shape[1]
    tn = 1536
    return pl.pallas_call(
        _adaln_kernel,
        grid=(n // tn,),
        in_specs=[pl.BlockSpec((bsz, d), lambda j: (0, 0)),
                  pl.BlockSpec((d, tn), lambda j: (0, j)),
                  pl.BlockSpec((1, tn), lambda j: (0, j))],
        out_specs=pl.BlockSpec((bsz, tn), lambda j: (0, j)),
        out_shape=jax.ShapeDtypeStruct((bsz, n), F32),
        compiler_params=_cparams(("arbitrary",)),
        name="adaln",
    )(c, w, b.reshape(1, n))


def _inproj_kernel(x_ref, mod_ref, g_ref, w_ref, qkv_ref, u_ref, *, n_qkv):
    x = x_ref[0]
    y = _rms(x, g_ref[...])
    h = y * (1.0 + mod_ref[0, 1:2, :]) + mod_ref[0, 0:1, :]
    p = jnp.dot(h.astype(BF16), w_ref[...], preferred_element_type=F32)
    qkv_ref[0] = p[:, :n_qkv].astype(BF16)
    u_ref[0] = p[:, n_qkv:]


def _inproj(x, mod3, g, w_bf16, n_qkv, tm=512):
    bsz, s, d = x.shape
    n = w_bf16.shape[1]
    n_u = n - n_qkv
    return pl.pallas_call(
        functools.partial(_inproj_kernel, n_qkv=n_qkv),
        grid=(bsz, s // tm),
        in_specs=[pl.BlockSpec((1, tm, d), lambda b, i: (b, i, 0)),
                  pl.BlockSpec((1, 6, d), lambda b, i: (b, 0, 0)),
                  pl.BlockSpec((1, d), lambda b, i: (0, 0)),
                  pl.BlockSpec((d, n), lambda b, i: (0, 0))],
        out_specs=[pl.BlockSpec((1, tm, n_qkv), lambda b, i: (b, i, 0)),
                   pl.BlockSpec((1, tm, n_u), lambda b, i: (b, i, 0))],
        out_shape=[jax.ShapeDtypeStruct((bsz, s, n_qkv), BF16),
                   jax.ShapeDtypeStruct((bsz, s, n_u), F32)],
        compiler_params=_cparams(("arbitrary", "arbitrary"), VMEM_LIMIT),
        name="inproj",
    )(x, mod3, g.reshape(1, d), w_bf16)


def _attn_kernel(lq1_ref, lk1_ref, lq2_ref, lk2_ref, sg_ref, q_ref, k_ref, v_ref, o_ref, *, tq):
    qi = pl.program_id(2)
    lam = (jnp.exp(jnp.sum(lq1_ref[...] * lk1_ref[...], axis=-1, keepdims=True))
           - jnp.exp(jnp.sum(lq2_ref[...] * lk2_ref[...], axis=-1, keepdims=True))
           + LAMBDA_INIT)
    q = q_ref[0] * (QK_DIM ** -0.5)
    lane = lax.broadcasted_iota(jnp.int32, q.shape, 1)
    zero = jnp.zeros_like(q)
    q1 = jnp.where(lane < QK_DIM, q, zero)
    q2 = jnp.where(lane >= QK_DIM, q, zero)
    nt = (((1,), (1,)), ((), ()))

    def one_map(qm, kb, vb, m, l, acc, mask):
        s = lax.dot_general(qm, kb, nt, preferred_element_type=F32)
        if mask is not None:
            s = jnp.where(mask, s, MASK_VALUE)
        m_new = jnp.maximum(m, jnp.max(s, axis=-1, keepdims=True))
        alpha = jnp.exp(m - m_new)
        p = jnp.exp(s - m_new)
        l = alpha * l + jnp.sum(p, axis=-1, keepdims=True)
        acc = alpha * acc + jnp.dot(p.astype(BF16), vb, preferred_element_type=F32)
        return m_new, l, acc

    def block(ki, carry, mask):
        m1, l1, a1, m2, l2, a2 = carry
        start = pl.multiple_of(ki * tq, tq)
        kb = k_ref[0, pl.ds(start, tq), :]
        vb = v_ref[0, pl.ds(start, tq), :]
        m1, l1, a1 = one_map(q1, kb, vb, m1, l1, a1, mask)
        m2, l2, a2 = one_map(q2, kb, vb, m2, l2, a2, mask)
        return m1, l1, a1, m2, l2, a2

    col0 = jnp.full((tq, 1), MASK_VALUE, F32)
    zc = jnp.zeros((tq, 1), F32)
    za = jnp.zeros((tq, V_DIM), F32)
    carry = (col0, zc, za, col0, zc, za)
    carry = lax.fori_loop(0, qi, lambda ki, c: block(ki, c, None), carry)
    row = lax.broadcasted_iota(jnp.int32, (tq, tq), 0)
    col = lax.broadcasted_iota(jnp.int32, (tq, tq), 1)
    m1, l1, a1, m2, l2, a2 = block(qi, carry, col <= row)
    o = a1 / l1 - lam * (a2 / l2)
    o = _rms(o, sg_ref[...]) * (1.0 - LAMBDA_INIT)
    o_ref[0] = o.astype(BF16)


def _attention(qkv, lq1, lk1, lq2, lk2, subln_g, tq=512):
    bsz, s, _ = qkv.shape
    h = ATTN_HEADS
    vec = lambda n: pl.BlockSpec((1, n), lambda b, hh, i: (0, 0))
    return pl.pallas_call(
        functools.partial(_attn_kernel, tq=tq),
        grid=(bsz, h, s // tq),
        in_specs=[vec(QK_DIM), vec(QK_DIM), vec(QK_DIM), vec(QK_DIM), vec(V_DIM),
                  pl.BlockSpec((1, tq, V_DIM), lambda b, hh, i: (b, i, hh)),
                  pl.BlockSpec((1, s, V_DIM), lambda b, hh, i: (b, 0, h + hh)),
                  pl.BlockSpec((1, s, V_DIM), lambda b, hh, i: (b, 0, 2 * h + hh))],
        out_specs=pl.BlockSpec((1, tq, V_DIM), lambda b, hh, i: (b, i, hh)),
        out_shape=jax.ShapeDtypeStruct((bsz, s, h * V_DIM), BF16),
        compiler_params=_cparams(("arbitrary", "arbitrary", "arbitrary"), VMEM_LIMIT),
        name="attention",
    )(lq1.reshape(1, -1), lk1.reshape(1, -1), lq2.reshape(1, -1), lk2.reshape(1, -1),
      subln_g.reshape(1, -1), qkv, qkv, qkv)


def _ssm_tables(a_re, a_im, log_dt, b_re, b_im, c_re, c_im, t):
    g, p = a_re.shape
    w = b_re.shape[-1]
    dt = jnp.exp(log_dt.astype(F32))[:, None]
    lam = lax.complex(jnp.minimum(a_re.astype(F32), -1e-4), a_im.astype(F32))
    lam_dt = lam * dt
    lam_bar = jnp.exp(lam_dt)
    b_bar = ((lam_bar - 1.0) / lam)[..., None] * lax.complex(b_re.astype(F32), b_im.astype(F32))
    c_cplx = lax.complex(c_re.astype(F32), c_im.astype(F32))
    tau = jnp.arange(t + 1, dtype=F32)
    pw = jnp.exp(lam_dt[:, None, :] * tau[None, :, None])
    kern = jnp.einsum('gcp,gtp,gpd->gtcd', c_cplx, pw[:, :t], b_bar).real
    kern = jnp.concatenate([kern, jnp.zeros((g, 1, w, w), F32)], axis=1)
    jj = jnp.arange(t)[:, None]
    tt = jnp.arange(t)[None, :]
    lag = jnp.where(tt >= jj, tt - jj, t)
    toep = kern[:, lag]
    toep = toep.transpose(0, 1, 4, 2, 3).reshape(g, t * w, t * w)
    wmat = pw[:, t - 1 - jnp.arange(t), :, None] * b_bar[:, None]
    wmat = wmat.transpose(0, 1, 3, 2).reshape(g, t * w, p)
    vmat = c_cplx[:, None] * pw[:, 1:t + 1, None, :]
    vmat = vmat.transpose(0, 3, 1, 2).reshape(g, p, t * w)
    lt = pw[:, t][:, None, :]
    return (toep.astype(BF16), wmat.real.astype(BF16), wmat.imag.astype(BF16),
            vmat.real.astype(BF16), (-vmat.imag).astype(BF16), lt.real, lt.imag)


def _ssm_kernel(u_ref, toep_ref, wre_ref, wim_ref, vre_ref, vim_ref, ltre_ref, ltim_ref, y_ref,
                slre, slim, spre, spim, *, bsz, nc):
    u = u_ref[0]
    slre[...] = jnp.dot(u, wre_ref[0], preferred_element_type=F32)
    slim[...] = jnp.dot(u, wim_ref[0], preferred_element_type=F32)
    a = ltre_ref[0]
    b = ltim_ref[0]

    def step(c, carry):
        sre, sim = carry
        rows = pl.ds(pl.multiple_of(c * bsz, bsz), bsz)
        spre[rows, :] = sre
        spim[rows, :] = sim
        nre = a * sre - b * sim + slre[rows, :]
        nim = a * sim + b * sre + slim[rows, :]
        return nre, nim

    z = jnp.zeros((bsz, a.shape[-1]), F32)
    lax.fori_loop(0, nc, step, (z, z))
    y = jnp.dot(u, toep_ref[0], preferred_element_type=F32)
    y = y + jnp.dot(spre[...].astype(BF16), vre_ref[0], preferred_element_type=F32)
    y = y + jnp.dot(spim[...].astype(BF16), vim_ref[0], preferred_element_type=F32)
    y_ref[0] = y


def _ssm(u_g, tables, bsz, nc):
    toep, wre, wim, vre, vim, ltre, ltim = tables
    g, r, tw = u_g.shape
    p = wre.shape[-1]
    blk = lambda a, b: pl.BlockSpec((1, a, b), lambda i: (i, 0, 0))
    return pl.pallas_call(
        functools.partial(_ssm_kernel, bsz=bsz, nc=nc),
        grid=(g,),
        in_specs=[blk(r, tw), blk(tw, tw), blk(tw, p), blk(tw, p), blk(p, tw), blk(p, tw),
                  blk(1, p), blk(1, p)],
        out_specs=blk(r, tw),
        out_shape=jax.ShapeDtypeStruct((g, r, tw), F32),
        scratch_shapes=[pltpu.VMEM((r, p), F32)] * 4,
        compiler_params=_cparams(("arbitrary",), VMEM_LIMIT),
        name="ssm",
    )(u_g, toep, wre, wim, vre, vim, ltre, ltim)


def _mix_kernel(x_ref, attn_ref, y_ref, u_ref, mod_ref, d_ref, wglu_ref, bglu_ref, wout_ref,
                g2_ref, wr_ref, br_ref, x1_ref, h2_ref, lg_ref, *, d_attn):
    y = y_ref[0] + d_ref[...] * u_ref[0]
    z = 0.5 * y * (1.0 + jnp.tanh(math.sqrt(2.0 / math.pi) * (y + 0.044715 * (y * y * y))))
    gl = jnp.dot(z.astype(BF16), wglu_ref[...], preferred_element_type=F32) + bglu_ref[...]
    so = z * _sigmoid(gl)
    mix = (jnp.dot(attn_ref[0], wout_ref[:d_attn, :], preferred_element_type=F32)
           + jnp.dot(so.astype(BF16), wout_ref[d_attn:, :], preferred_element_type=F32))
    x1 = x_ref[0] + mod_ref[0, 2:3, :] * mix
    x1_ref[0] = x1
    h2 = _rms(x1, g2_ref[...]) * (1.0 + mod_ref[0, 4:5, :]) + mod_ref[0, 3:4, :]
    h2_ref[0] = h2
    lg_ref[0] = jnp.dot(h2, wr_ref[...], preferred_element_type=F32, precision=HIGHEST) + br_ref[...]


def _mix(x, attn, y, u, mod3, d_skip, wglu, bglu, wout, g2, wr, br, tm=512):
    bsz, s, d = x.shape
    da = attn.shape[-1]
    ds_ = y.shape[-1]
    e = wr.shape[-1]
    tok = lambda n: pl.BlockSpec((1, tm, n), lambda b, i: (b, i, 0))
    full = lambda a, b_: pl.BlockSpec((a, b_), lambda b, i: (0, 0))
    return pl.pallas_call(
        functools.partial(_mix_kernel, d_attn=da),
        grid=(bsz, s // tm),
        in_specs=[tok(d), tok(da), tok(ds_), tok(ds_),
                  pl.BlockSpec((1, 6, d), lambda b, i: (b, 0, 0)),
                  full(1, ds_), full(ds_, ds_), full(1, ds_), full(da + ds_, d),
                  full(1, d), full(d, e), full(1, e)],
        out_specs=[tok(d), tok(d), tok(e)],
        out_shape=[jax.ShapeDtypeStruct((bsz, s, d), F32),
                   jax.ShapeDtypeStruct((bsz, s, d), F32),
                   jax.ShapeDtypeStruct((bsz, s, e), F32)],
        compiler_params=_cparams(("arbitrary", "arbitrary"), VMEM_LIMIT),
        name="mix",
    )(x, attn, y, u, mod3, d_skip.reshape(1, -1), wglu, bglu.reshape(1, -1), wout,
      g2.reshape(1, -1), wr, br.reshape(1, -1))


def _route_kernel(lg_ref, idx_ref, gate_ref, rank_ref, cnt_ref, run_ref, *, tm):
    i = pl.program_id(0)

    @pl.when(i == 0)
    def _():
        run_ref[...] = jnp.zeros_like(run_ref)

    l = lg_ref[...]
    e = l.shape[-1]
    col = lax.broadcasted_iota(jnp.int32, l.shape, 1)
    vals, sels, idxs = [], [], []
    for _k in range(TOP_K):
        mx = jnp.max(l, axis=-1, keepdims=True)
        ix = jnp.min(jnp.where(l == mx, col, e), axis=-1, keepdims=True)
        sel = col == ix
        vals.append(mx)
        idxs.append(ix)
        sels.append(sel)
        l = jnp.where(sel, -jnp.inf, l)
    ex = [jnp.exp(v - vals[0]) for v in vals]
    den = ex[0] + ex[1] + ex[2] + ex[3]
    chosen = jnp.zeros(l.shape, F32)
    for sel in sels:
        chosen = chosen + jnp.where(sel, 1.0, 0.0)
    r_i = lax.broadcasted_iota(jnp.int32, (tm, tm), 0)
    c_i = lax.broadcasted_iota(jnp.int32, (tm, tm), 1)
    tri = jnp.where(c_i < r_i, 1.0, 0.0).astype(BF16)
    before = jnp.dot(tri, chosen.astype(BF16), preferred_element_type=F32) + run_ref[...]
    lane = lax.broadcasted_iota(jnp.int32, (tm, TOP_K), 1)
    idx_o = jnp.zeros((tm, TOP_K), jnp.int32)
    gate_o = jnp.zeros((tm, TOP_K), F32)
    rank_o = jnp.zeros((tm, TOP_K), F32)
    for k in range(TOP_K):
        rk = jnp.sum(jnp.where(sels[k], before, 0.0), axis=-1, keepdims=True)
        idx_o = jnp.where(lane == k, idxs[k], idx_o)
        gate_o = jnp.where(lane == k, ex[k] / den, gate_o)
        rank_o = jnp.where(lane == k, rk, rank_o)
    idx_ref[...] = idx_o
    gate_ref[...] = gate_o
    rank_ref[...] = rank_o.astype(jnp.int32)
    run_ref[...] = run_ref[...] + jnp.sum(chosen, axis=0, keepdims=True)
    cnt_ref[...] = run_ref[...].astype(jnp.int32)


def _route(logits, tm=512):
    n, e = logits.shape
    tok = lambda w: pl.BlockSpec((tm, w), lambda i: (i, 0))
    return pl.pallas_call(
        functools.partial(_route_kernel, tm=tm),
        grid=(n // tm,),
        in_specs=[tok(e)],
        out_specs=[tok(TOP_K), tok(TOP_K), tok(TOP_K), pl.BlockSpec((1, e), lambda i: (0, 0))],
        out_shape=[jax.ShapeDtypeStruct((n, TOP_K), jnp.int32),
                   jax.ShapeDtypeStruct((n, TOP_K), F32),
                   jax.ShapeDtypeStruct((n, TOP_K), jnp.int32),
                   jax.ShapeDtypeStruct((1, e), jnp.int32)],
        scratch_shapes=[pltpu.VMEM((1, e), F32)],
        compiler_params=_cparams(("arbitrary",)),
        name="route",
    )(logits)


def _dispatch_kernel(pend_ref, padded_ref, nb_ref, pos_ref, h_ref, xs_ref, zero_ref, sem, zsem,
                     *, tm, rb, n_exp, nb_max):
    i = pl.program_id(0)

    def zero_block(blk_start):
        cp = pltpu.make_async_copy(zero_ref, xs_ref.at[pl.ds(pl.multiple_of(blk_start, rb), rb), :],
                                   zsem)
        cp.start()
        cp.wait()

    @pl.when(i == 0)
    def _():
        zero_ref[...] = jnp.zeros_like(zero_ref)

    @pl.when(i < n_exp)
    def _():
        e = jnp.minimum(i, n_exp - 1)

        @pl.when(padded_ref[e] > 0)
        def _():
            zero_block(pend_ref[e] - rb)

        @pl.when(nb_ref[0] + e < nb_max)
        def _():
            zero_block((nb_ref[0] + e) * rb)

    @pl.when(i >= n_exp)
    def _():
        def row_copy(t, p):
            return pltpu.make_async_copy(h_ref.at[pl.ds(t, 1), :], xs_ref.at[pl.ds(p, 1), :], sem)

        def issue(t, _):
            for k in range(TOP_K):
                row_copy(t, pos_ref[0, 0, t * TOP_K + k]).start()
            return 0

        lax.fori_loop(0, tm, issue, 0)

        def drain(t, _):
            for k in range(TOP_K):
                row_copy(0, 0).wait()
            return 0

        lax.fori_loop(0, tm, drain, 0)


def _dispatch(pend, padded, nblk, pos3, h2, n_rows, tm, rb):
    n, d = h2.shape
    n_exp = pend.shape[0]
    tile = lambda i, *_: (jnp.maximum(i - n_exp, 0), 0)
    return pl.pallas_call(
        functools.partial(_dispatch_kernel, tm=tm, rb=rb, n_exp=n_exp, nb_max=n_rows // rb),
        grid_spec=pltpu.PrefetchScalarGridSpec(
            num_scalar_prefetch=3,
            grid=(n_exp + n // tm,),
            in_specs=[pl.BlockSpec((1, 1, tm * TOP_K), lambda i, *_: tile(i) + (0,),
                                   memory_space=pltpu.SMEM),
                      pl.BlockSpec((tm, d), tile)],
            out_specs=pl.BlockSpec(memory_space=pl.ANY),
            scratch_shapes=[pltpu.VMEM((rb, d), F32), pltpu.SemaphoreType.DMA,
                            pltpu.SemaphoreType.DMA]),
        out_shape=jax.ShapeDtypeStruct((n_rows, d), F32),
        compiler_params=_cparams(("arbitrary",)),
        name="dispatch",
    )(pend, padded, nblk, pos3, h2)


def _experts_kernel(be_ref, xb_ref, nb_ref, x_ref, w1_ref, b1_ref, w2_ref, b2_ref, y_ref,
                    w1b, w2b, *, f):
    i = pl.program_id(0)
    prev = be_ref[jnp.maximum(i - 1, 0)]
    live = i < nb_ref[0]

    @pl.when(live & ((i == 0) | (be_ref[i] != prev)))
    def _():
        w1b[...] = w1_ref[0].astype(BF16)
        w2b[...] = w2_ref[0].astype(BF16)

    @pl.when(live)
    def _():
        x = x_ref[...].astype(BF16)
        gu = jnp.dot(x, w1b[...], preferred_element_type=F32) + b1_ref[0]
        gate = jnp.minimum(gu[:, :f], SWIGLU_LIMIT)
        up = jnp.clip(gu[:, f:], -SWIGLU_LIMIT, SWIGLU_LIMIT)
        glu = gate * _sigmoid(SWIGLU_ALPHA * gate)
        hmid = ((up + 1.0) * glu).astype(BF16)
        y_ref[...] = jnp.dot(hmid, w2b[...], preferred_element_type=F32) + b2_ref[0]

    @pl.when(jnp.logical_not(live))
    def _():
        y_ref[...] = jnp.zeros_like(y_ref)


def _experts(block_e, xblk, nblk, xs, w1, b1, w2, b2, rb):
    n_rows, d = xs.shape
    e, _, f2 = w1.shape
    f = f2 // 2
    nb_max = n_rows // rb
    return pl.pallas_call(
        functools.partial(_experts_kernel, f=f),
        grid_spec=pltpu.PrefetchScalarGridSpec(
            num_scalar_prefetch=3,
            grid=(nb_max,),
            in_specs=[pl.BlockSpec((rb, d), lambda i, be, xb, nb: (xb[i], 0)),
                      pl.BlockSpec((1, d, f2), lambda i, be, xb, nb: (be[i], 0, 0)),
                      pl.BlockSpec((1, 1, f2), lambda i, be, xb, nb: (be[i], 0, 0)),
                      pl.BlockSpec((1, f, d), lambda i, be, xb, nb: (be[i], 0, 0)),
                      pl.BlockSpec((1, 1, d), lambda i, be, xb, nb: (be[i], 0, 0))],
            out_specs=pl.BlockSpec((rb, d), lambda i, be, xb, nb: (i, 0)),
            scratch_shapes=[pltpu.VMEM((d, f2), BF16), pltpu.VMEM((f, d), BF16)]),
        out_shape=jax.ShapeDtypeStruct((n_rows, d), F32),
        compiler_params=_cparams(("arbitrary",), VMEM_LIMIT),
        name="experts",
    )(block_e, xblk, nblk, xs, w1, b1.reshape(e, 1, f2), w2, b2.reshape(e, 1, d))


def _combine_kernel(pos_ref, gate_ref, x1_ref, mod_ref, fg_ref, ys_ref, o_ref, buf, sem, *, tm):
    def row_copy(t, k, p):
        return pltpu.make_async_copy(ys_ref.at[pl.ds(p, 1), :], buf.at[k, pl.ds(t, 1), :], sem)

    def issue(t, _):
        for k in range(TOP_K):
            row_copy(t, k, pos_ref[0, 0, t * TOP_K + k]).start()
        return 0

    lax.fori_loop(0, tm, issue, 0)

    def drain(t, _):
        for k in range(TOP_K):
            row_copy(0, k, 0).wait()
        return 0

    lax.fori_loop(0, tm, drain, 0)
    gate = gate_ref[0]
    moe = gate[:, 0:1] * buf[0]
    for k in range(1, TOP_K):
        moe = moe + gate[:, k:k + 1] * buf[k]
    x2 = x1_ref[0] + mod_ref[0, 5:6, :] * moe
    o_ref[0] = _rms(x2, fg_ref[...])


def _combine(pos3, gates, x1, mod3, final_g, ys, tm):
    bsz, s, d = x1.shape
    nt = s // tm
    return pl.pallas_call(
        functools.partial(_combine_kernel, tm=tm),
        grid=(bsz, nt),
        in_specs=[pl.BlockSpec((1, 1, tm * TOP_K), lambda b, i: (b * nt + i, 0, 0),
                               memory_space=pltpu.SMEM),
                  pl.BlockSpec((1, tm, TOP_K), lambda b, i: (b, i, 0)),
                  pl.BlockSpec((1, tm, d), lambda b, i: (b, i, 0)),
                  pl.BlockSpec((1, 6, d), lambda b, i: (b, 0, 0)),
                  pl.BlockSpec((1, d), lambda b, i: (0, 0)),
                  pl.BlockSpec(memory_space=pl.ANY)],
        out_specs=pl.BlockSpec((1, tm, d), lambda b, i: (b, i, 0)),
        out_shape=jax.ShapeDtypeStruct((bsz, s, d), F32),
        scratch_shapes=[pltpu.VMEM((TOP_K, tm, d), F32), pltpu.SemaphoreType.DMA],
        compiler_params=_cparams(("arbitrary", "arbitrary"), VMEM_LIMIT),
        name="combine",
    )(pos3, gates.reshape(bsz, s, TOP_K), x1, mod3, final_g.reshape(1, d), ys)


def _layer(x, mod3, norm1_g, w_in, lq1, lk1, lq2, lk2, subln_g, ssm_a_re, ssm_a_im, ssm_log_dt,
           ssm_b_re, ssm_b_im, ssm_c_re, ssm_c_im, ssm_d, w_glu, b_glu, w_out, norm2_g,
           w_router, b_router, w1, b1, w2, b2, final_g):
    bsz, s, d = x.shape
    n = bsz * s
    d_attn = ATTN_HEADS * V_DIM
    n_qkv = 3 * d_attn
    d_ssm = w_in.shape[1] - n_qkv
    g = d_ssm // SSM_GROUP_WIDTH
    t = SSM_CHUNK
    nc = s // t

    qkv, u = _inproj(x, mod3, norm1_g, w_in.astype(BF16), n_qkv)
    attn = _attention(qkv, lq1, lk1, lq2, lk2, subln_g)

    u_g = (u.reshape(bsz, nc, t, g, SSM_GROUP_WIDTH).transpose(3, 1, 0, 2, 4)
           .reshape(g, nc * bsz, t * SSM_GROUP_WIDTH).astype(BF16))
    tables = _ssm_tables(ssm_a_re, ssm_a_im, ssm_log_dt, ssm_b_re, ssm_b_im, ssm_c_re, ssm_c_im, t)
    y_g = _ssm(u_g, tables, bsz, nc)
    y = (y_g.reshape(g, nc, bsz, t, SSM_GROUP_WIDTH).transpose(2, 1, 3, 0, 4)
         .reshape(bsz, s, d_ssm))

    x1, h2, logits = _mix(x, attn, y, u, mod3, ssm_d, w_glu.astype(BF16), b_glu,
                          w_out.astype(BF16), norm2_g, w_router, b_router)

    idx, gates, rank, counts = _route(logits.reshape(n, N_EXPERTS))
    rb = 512 if n * TOP_K >= 512 * N_EXPERTS else 128
    counts = counts.reshape(N_EXPERTS)
    padded = ((counts + rb - 1) // rb) * rb
    pend = jnp.cumsum(padded).astype(jnp.int32)
    pstart = pend - padded
    pos = (pstart[idx] + rank).astype(jnp.int32)
    nb_max = (n * TOP_K) // rb + N_EXPERTS
    n_rows = nb_max * rb
    nblk = pend[-1] // rb
    blk_ids = jnp.minimum(jnp.arange(nb_max, dtype=jnp.int32), nblk - 1)
    block_e = jnp.minimum(jnp.searchsorted(pend, blk_ids * rb, side='right'),
                          N_EXPERTS - 1).astype(jnp.int32)
    tmd = 256
    pos3 = pos.reshape(n // tmd, 1, tmd * TOP_K)

    nblk = nblk.reshape(1).astype(jnp.int32)
    xs = _dispatch(pend, padded.astype(jnp.int32), nblk, pos3, h2.reshape(n, d), n_rows, tmd, rb)
    ys = _experts(block_e, blk_ids, nblk, xs, w1, b1, w2, b2, rb)
    return _combine(pos3, gates, x1, mod3, final_g, ys, tmd)


def kernel(x, c, w_ada, b_ada, norm1_g, w_in, lq1, lk1, lq2, lk2, subln_g, ssm_a_re, ssm_a_im,
           ssm_log_dt, ssm_b_re, ssm_b_im, ssm_c_re, ssm_c_im, ssm_d, w_glu, b_glu, w_out, norm2_g,
           w_router, b_router, w1, b1, w2, b2, final_g):
    assert w_ada.shape[0] == 1, "single-layer block"
    bsz, s, d = x.shape
    mod3 = _adaln(c, w_ada[0], b_ada[0]).reshape(bsz, 6, d)
    return _layer(x, mod3, norm1_g[0], w_in[0], lq1[0], lk1[0], lq2[0], lk2[0], subln_g[0],
                  ssm_a_re[0], ssm_a_im[0], ssm_log_dt[0], ssm_b_re[0], ssm_b_im[0], ssm_c_re[0],
                  ssm_c_im[0], ssm_d[0], w_glu[0], b_glu[0], w_out[0], norm2_g[0], w_router[0],
                  b_router[0], w1[0], b1[0], w2[0], b2[0], final_g)
```

```python
import functools
import math

import jax
import jax.numpy as jnp
from jax import lax
from jax.experimental import pallas as pl
from jax.experimental.pallas import tpu as pltpu

F32 = jnp.float32
BF16 = jnp.bfloat16
HIGHEST = lax.Precision.HIGHEST

RMS_EPS = 1e-6
MASK_VALUE = -1e30
ATTN_HEADS = 4
QK_DIM = 64
V_DIM = 128
SSM_GROUP_WIDTH = 16
SSM_STATE = 64
SSM_CHUNK = 32
N_EXPERTS = 32
TOP_K = 4
SWIGLU_LIMIT = 7.0
SWIGLU_ALPHA = 1.702
LAMBDA_INIT = 0.8 - 0.6 * math.exp(-0.3 * 0)

VMEM_LIMIT = 56 * 1024 * 1024
NT_DIMS = (((1,), (1,)), ((), ()))
N_QUERY_GROUPS = 1


def _cparams(sem, vmem=None):
    return pltpu.CompilerParams(dimension_semantics=sem, vmem_limit_bytes=vmem)


def _sigmoid(x):
    return 1.0 / (1.0 + jnp.exp(-x))


def _rms(x, g):
    ms = jnp.mean(x * x, axis=-1, keepdims=True)
    return x * lax.rsqrt(ms + RMS_EPS) * g


def _adaln_kernel(c_ref, w_ref, b_ref, o_ref):
    c = c_ref[...]
    ca = c * _sigmoid(c)
    o_ref[...] = jnp.dot(ca, w_ref[...], preferred_element_type=F32, precision=HIGHEST) + b_ref[...]


def _adaln(c, w, b):
    bsz, d = c.shape
    n = w.shape[1]
    tn = 1536
    return pl.pallas_call(
        _adaln_kernel,
        grid=(n // tn,),
        in_specs=[pl.BlockSpec((bsz, d), lambda j: (0, 0)),
                  pl.BlockSpec((d, tn), lambda j: (0, j)),
                  pl.BlockSpec((1, tn), lambda j: (0, j))],
        out_specs=pl.BlockSpec((bsz, tn), lambda j: (0, j)),
        out_shape=jax.ShapeDtypeStruct((bsz, n), F32),
        compiler_params=_cparams(("arbitrary",)),
        name="adaln",
    )(c, w, b.reshape(1, n))


def _inproj_kernel(x_ref, mod_ref, g_ref, w_ref, qkv_ref, u_ref, *, n_qkv):
    x = x_ref[0]
    y = _rms(x, g_ref[...])
    h = y * (1.0 + mod_ref[0, 1:2, :]) + mod_ref[0, 0:1, :]
    p = jnp.dot(h.astype(BF16), w_ref[...], preferred_element_type=F32)
    qkv_ref[0] = p[:, :n_qkv].astype(BF16)
    u_ref[0] = p[:, n_qkv:]


def _inproj(x, mod3, g, w_bf16, n_qkv, tm=512):
    bsz, s, d = x.shape
    n = w_bf16.shape[1]
    n_u = n - n_qkv
    return pl.pallas_call(
        functools.partial(_inproj_kernel, n_qkv=n_qkv),
        grid=(bsz, s // tm),
        in_specs=[pl.BlockSpec((1, tm, d), lambda b, i: (b, i, 0)),
                  pl.BlockSpec((1, 6, d), lambda b, i: (b, 0, 0)),
                  pl.BlockSpec((1, d), lambda b, i: (0, 0)),
                  pl.BlockSpec((d, n), lambda b, i: (0, 0))],
        out_specs=[pl.BlockSpec((1, tm, n_qkv), lambda b, i: (b, i, 0)),
                   pl.BlockSpec((1, tm, n_u), lambda b, i: (b, i, 0))],
        out_shape=[jax.ShapeDtypeStruct((bsz, s, n_qkv), BF16),
                   jax.ShapeDtypeStruct((bsz, s, n_u), F32)],
        compiler_params=_cparams(("arbitrary", "arbitrary"), VMEM_LIMIT),
        name="inproj",
    )(x, mod3, g.reshape(1, d), w_bf16)


def _attn_kernel(lq1_ref, lk1_ref, lq2_ref, lk2_ref, sg_ref, q_ref, k_ref, v_ref, o_ref,
                 vt_ref, m_ref, acc_ref, *, tq):
    qi = pl.program_id(2)
    n_acc = acc_ref.shape[0]

    @pl.when(qi == 0)
    def _():
        vt_ref[:V_DIM, :] = v_ref[0].astype(F32).T.astype(BF16)
        vt_ref[V_DIM:, :] = jnp.ones((n_acc - V_DIM, vt_ref.shape[1]), BF16)

    lam = (jnp.exp(jnp.sum(lq1_ref[...] * lk1_ref[...], axis=-1, keepdims=True))
           - jnp.exp(jnp.sum(lq2_ref[...] * lk2_ref[...], axis=-1, keepdims=True))
           + LAMBDA_INIT)
    q = q_ref[0] * (QK_DIM ** -0.5)
    lane = lax.broadcasted_iota(jnp.int32, q.shape, 1)
    zero = jnp.zeros_like(q)
    qs = jnp.concatenate([jnp.where(lane < QK_DIM, q, zero), jnp.where(lane >= QK_DIM, q, zero)],
                         axis=0)
    m_ref[...] = jnp.full(m_ref.shape, MASK_VALUE, F32)
    acc_ref[...] = jnp.zeros(acc_ref.shape, F32)

    def block(ki, mask):
        start = pl.multiple_of(ki * tq, tq)
        kb = k_ref[0, pl.ds(start, tq), :]
        vt = vt_ref[:, pl.ds(start, tq)]
        w = 2 * tq // N_QUERY_GROUPS
        for c in range(N_QUERY_GROUPS):
            cols = slice(c * w, (c + 1) * w)
            st = lax.dot_general(kb, qs[cols], NT_DIMS, preferred_element_type=F32)
            if mask is not None:
                st = jnp.where(mask[:, cols], st, MASK_VALUE)
            m_prev = m_ref[:, cols]
            m_new = jnp.maximum(m_prev, jnp.max(st, axis=0, keepdims=True))
            pt = jnp.exp(st - m_new).astype(BF16)
            acc_ref[:, cols] = (jnp.exp(m_prev - m_new) * acc_ref[:, cols]
                                + jnp.dot(vt, pt, preferred_element_type=F32))
            m_ref[:, cols] = m_new

    def body(ki, carry):
        block(ki, None)
        return carry

    lax.fori_loop(0, qi, body, 0)
    key = lax.broadcasted_iota(jnp.int32, (tq, 2 * tq), 0)
    qry = lax.broadcasted_iota(jnp.int32, (tq, 2 * tq), 1)
    block(qi, key <= jnp.where(qry >= tq, qry - tq, qry))
    a = acc_ref[...]
    ot = (a[:V_DIM, :tq] / a[V_DIM:V_DIM + 1, :tq]
          - lam * (a[:V_DIM, tq:] / a[V_DIM:V_DIM + 1, tq:]))
    o = _rms(ot.T, sg_ref[...]) * (1.0 - LAMBDA_INIT)
    o_ref[0] = o.astype(BF16)


def _attention(qkv, lq1, lk1, lq2, lk2, subln_g, tq=1024):
    bsz, s, _ = qkv.shape
    tq = min(tq, s)
    h = ATTN_HEADS
    n_acc = V_DIM + 8
    vec = lambda n: pl.BlockSpec((1, n), lambda b, hh, i: (0, 0))
    return pl.pallas_call(
        functools.partial(_attn_kernel, tq=tq),
        grid=(bsz, h, s // tq),
        in_specs=[vec(QK_DIM), vec(QK_DIM), vec(QK_DIM), vec(QK_DIM), vec(V_DIM),
                  pl.BlockSpec((1, tq, V_DIM), lambda b, hh, i: (b, i, hh)),
                  pl.BlockSpec((1, s, V_DIM), lambda b, hh, i: (b, 0, h + hh)),
                  pl.BlockSpec((1, s, V_DIM), lambda b, hh, i: (b, 0, 2 * h + hh))],
        out_specs=pl.BlockSpec((1, tq, V_DIM), lambda b, hh, i: (b, i, hh)),
        out_shape=jax.ShapeDtypeStruct((bsz, s, h * V_DIM), BF16),
        scratch_shapes=[pltpu.VMEM((n_acc, s), BF16),
                        pltpu.VMEM((1, 2 * tq), F32), pltpu.VMEM((n_acc, 2 * tq), F32)],
        compiler_params=_cparams(("arbitrary", "arbitrary", "arbitrary"), VMEM_LIMIT),
        name="attention",
    )(lq1.reshape(1, -1), lk1.reshape(1, -1), lq2.reshape(1, -1), lk2.reshape(1, -1),
      subln_g.reshape(1, -1), qkv, qkv, qkv)


def _ssm_tables(a_re, a_im, log_dt, b_re, b_im, c_re, c_im, t):
    g, p = a_re.shape
    w = b_re.shape[-1]
    dt = jnp.exp(log_dt.astype(F32))[:, None]
    lam = lax.complex(jnp.minimum(a_re.astype(F32), -1e-4), a_im.astype(F32))
    lam_dt = lam * dt
    lam_bar = jnp.exp(lam_dt)
    b_bar = ((lam_bar - 1.0) / lam)[..., None] * lax.complex(b_re.astype(F32), b_im.astype(F32))
    c_cplx = lax.complex(c_re.astype(F32), c_im.astype(F32))
    tau = jnp.arange(t + 1, dtype=F32)
    pw = jnp.exp(lam_dt[:, None, :] * tau[None, :, None])
    kern = jnp.einsum('gcp,gtp,gpd->gtcd', c_cplx, pw[:, :t], b_bar).real
    kern = jnp.concatenate([kern, jnp.zeros((g, 1, w, w), F32)], axis=1)
    jj = jnp.arange(t)[:, None]
    tt = jnp.arange(t)[None, :]
    lag = jnp.where(tt >= jj, tt - jj, t)
    toep = kern[:, lag]
    toep = toep.transpose(0, 1, 4, 2, 3).reshape(g, t * w, t * w)
    wmat = pw[:, t - 1 - jnp.arange(t), :, None] * b_bar[:, None]
    wmat = wmat.transpose(0, 1, 3, 2).reshape(g, t * w, p)
    vmat = c_cplx[:, None] * pw[:, 1:t + 1, None, :]
    vmat = vmat.transpose(0, 3, 1, 2).reshape(g, p, t * w)
    lt = pw[:, t][:, None, :]
    return (toep.astype(BF16), wmat.real.astype(BF16), wmat.imag.astype(BF16),
            vmat.real.astype(BF16), (-vmat.imag).astype(BF16), lt.real, lt.imag)


def _ssm_kernel(u_ref, toep_ref, wre_ref, wim_ref, vre_ref, vim_ref, ltre_ref, ltim_ref, y_ref,
                slre, slim, spre, spim, *, bsz, nc):
    u = u_ref[0]
    slre[...] = jnp.dot(u, wre_ref[0], preferred_element_type=F32)
    slim[...] = jnp.dot(u, wim_ref[0], preferred_element_type=F32)
    a = ltre_ref[0]
    b = ltim_ref[0]

    def step(c, carry):
        sre, sim = carry
        rows = pl.ds(pl.multiple_of(c * bsz, bsz), bsz)
        spre[rows, :] = sre
        spim[rows, :] = sim
        nre = a * sre - b * sim + slre[rows, :]
        nim = a * sim + b * sre + slim[rows, :]
        return nre, nim

    z = jnp.zeros((bsz, a.shape[-1]), F32)
    lax.fori_loop(0, nc, step, (z, z))
    y = jnp.dot(u, toep_ref[0], preferred_element_type=F32)
    y = y + jnp.dot(spre[...].astype(BF16), vre_ref[0], preferred_element_type=F32)
    y = y + jnp.dot(spim[...].astype(BF16), vim_ref[0], preferred_element_type=F32)
    y_ref[0] = y


def _ssm(u_g, tables, bsz, nc):
    toep, wre, wim, vre, vim, ltre, ltim = tables
    g, r, tw = u_g.shape
    p = wre.shape[-1]
    blk = lambda a, b: pl.BlockSpec((1, a, b), lambda i: (i, 0, 0))
    return pl.pallas_call(
        functools.partial(_ssm_kernel, bsz=bsz, nc=nc),
        grid=(g,),
        in_specs=[blk(r, tw), blk(tw, tw), blk(tw, p), blk(tw, p), blk(p, tw), blk(p, tw),
                  blk(1, p), blk(1, p)],
        out_specs=blk(r, tw),
        out_shape=jax.ShapeDtypeStruct((g, r, tw), F32),
        scratch_shapes=[pltpu.VMEM((r, p), F32)] * 4,
        compiler_params=_cparams(("arbitrary",), VMEM_LIMIT),
        name="ssm",
    )(u_g, toep, wre, wim, vre, vim, ltre, ltim)


def _mix_kernel(x_ref, attn_ref, y_ref, u_ref, mod_ref, d_ref, wglu_ref, bglu_ref, wout_ref,
                g2_ref, wrt_ref, br_ref, x1_ref, h2_ref, lgt_ref, *, d_attn):
    y = y_ref[0] + d_ref[...] * u_ref[0]
    z = 0.5 * y * (1.0 + jnp.tanh(math.sqrt(2.0 / math.pi) * (y + 0.044715 * (y * y * y))))
    gl = jnp.dot(z.astype(BF16), wglu_ref[...], preferred_element_type=F32) + bglu_ref[...]
    so = z * _sigmoid(gl)
    mix = (jnp.dot(attn_ref[0], wout_ref[:d_attn, :], preferred_element_type=F32)
           + jnp.dot(so.astype(BF16), wout_ref[d_attn:, :], preferred_element_type=F32))
    x1 = x_ref[0] + mod_ref[0, 2:3, :] * mix
    x1_ref[0] = x1
    h2 = _rms(x1, g2_ref[...]) * (1.0 + mod_ref[0, 4:5, :]) + mod_ref[0, 3:4, :]
    h2_ref[0] = h2
    lgt_ref[...] = lax.dot_general(wrt_ref[...], h2, NT_DIMS, preferred_element_type=F32,
                                   precision=HIGHEST) + br_ref[...]


def _mix(x, attn, y, u, mod3, d_skip, wglu, bglu, wout, g2, wr, br, tm=512):
    bsz, s, d = x.shape
    da = attn.shape[-1]
    ds_ = y.shape[-1]
    e = wr.shape[-1]
    nt = s // tm
    tok = lambda n: pl.BlockSpec((1, tm, n), lambda b, i: (b, i, 0))
    full = lambda a, b_: pl.BlockSpec((a, b_), lambda b, i: (0, 0))
    return pl.pallas_call(
        functools.partial(_mix_kernel, d_attn=da),
        grid=(bsz, nt),
        in_specs=[tok(d), tok(da), tok(ds_), tok(ds_),
                  pl.BlockSpec((1, 6, d), lambda b, i: (b, 0, 0)),
                  full(1, ds_), full(ds_, ds_), full(1, ds_), full(da + ds_, d),
                  full(1, d), full(e, d), full(e, 1)],
        out_specs=[tok(d), tok(d), pl.BlockSpec((e, tm), lambda b, i: (0, b * nt + i))],
        out_shape=[jax.ShapeDtypeStruct((bsz, s, d), F32),
                   jax.ShapeDtypeStruct((bsz, s, d), F32),
                   jax.ShapeDtypeStruct((e, bsz * s), F32)],
        compiler_params=_cparams(("arbitrary", "arbitrary"), VMEM_LIMIT),
        name="mix",
    )(x, attn, y, u, mod3, d_skip.reshape(1, -1), wglu, bglu.reshape(1, -1), wout,
      g2.reshape(1, -1), wr.T, br.reshape(-1, 1))


def _route_kernel(lg_ref, idx_ref, gate_ref, rank_ref, cnt_ref, run_ref, *, tm):
    i = pl.program_id(0)

    @pl.when(i == 0)
    def _():
        run_ref[...] = jnp.zeros_like(run_ref)

    l = lg_ref[...]
    e = l.shape[0]
    sub = lax.broadcasted_iota(jnp.int32, l.shape, 0)
    vals, sels, idxs = [], [], []
    for _k in range(TOP_K):
        mx = jnp.max(l, axis=0, keepdims=True)
        ix = jnp.min(jnp.where(l == mx, sub, e), axis=0, keepdims=True)
        sel = sub == ix
        vals.append(mx)
        idxs.append(ix)
        sels.append(sel)
        l = jnp.where(sel, -jnp.inf, l)
    ex = [jnp.exp(v - vals[0]) for v in vals]
    den = ex[0] + ex[1] + ex[2] + ex[3]
    chosen = jnp.zeros(l.shape, F32)
    for sel in sels:
        chosen = chosen + jnp.where(sel, 1.0, 0.0)
    r_i = lax.broadcasted_iota(jnp.int32, (tm, tm), 0)
    c_i = lax.broadcasted_iota(jnp.int32, (tm, tm), 1)
    tri = jnp.where(r_i < c_i, 1.0, 0.0).astype(BF16)
    before = jnp.dot(chosen.astype(BF16), tri, preferred_element_type=F32) + run_ref[...]
    ksub = lax.broadcasted_iota(jnp.int32, (TOP_K, tm), 0)
    idx_o = jnp.zeros((TOP_K, tm), jnp.int32)
    gate_o = jnp.zeros((TOP_K, tm), F32)
    rank_o = jnp.zeros((TOP_K, tm), F32)
    for k in range(TOP_K):
        rk = jnp.sum(jnp.where(sels[k], before, 0.0), axis=0, keepdims=True)
        idx_o = jnp.where(ksub == k, idxs[k], idx_o)
        gate_o = jnp.where(ksub == k, ex[k] / den, gate_o)
        rank_o = jnp.where(ksub == k, rk, rank_o)
    idx_ref[...] = idx_o
    gate_ref[...] = gate_o
    rank_ref[...] = rank_o.astype(jnp.int32)
    run_ref[...] = run_ref[...] + jnp.sum(chosen, axis=1, keepdims=True)
    cnt_ref[...] = run_ref[...].astype(jnp.int32)


def _route(logits_t, tm=512):
    e, n = logits_t.shape
    tok = lambda w: pl.BlockSpec((w, tm), lambda i: (0, i))
    return pl.pallas_call(
        functools.partial(_route_kernel, tm=tm),
        grid=(n // tm,),
        in_specs=[tok(e)],
        out_specs=[tok(TOP_K), tok(TOP_K), tok(TOP_K), pl.BlockSpec((e, 1), lambda i: (0, 0))],
        out_shape=[jax.ShapeDtypeStruct((TOP_K, n), jnp.int32),
                   jax.ShapeDtypeStruct((TOP_K, n), F32),
                   jax.ShapeDtypeStruct((TOP_K, n), jnp.int32),
                   jax.ShapeDtypeStruct((e, 1), jnp.int32)],
        scratch_shapes=[pltpu.VMEM((e, 1), F32)],
        compiler_params=_cparams(("arbitrary",)),
        name="route",
    )(logits_t)


def _dispatch_kernel(pend_ref, padded_ref, nb_ref, pos_ref, h_ref, xs_ref, zero_ref, sem, zsem,
                     *, tm, rb, n_exp, nb_max):
    i = pl.program_id(0)

    def zero_block(blk_start):
        cp = pltpu.make_async_copy(zero_ref, xs_ref.at[pl.ds(pl.multiple_of(blk_start, rb), rb), :],
                                   zsem)
        cp.start()
        cp.wait()

    @pl.when(i == 0)
    def _():
        zero_ref[...] = jnp.zeros_like(zero_ref)

    @pl.when(i < n_exp)
    def _():
        e = jnp.minimum(i, n_exp - 1)

        @pl.when(padded_ref[e] > 0)
        def _():
            zero_block(pend_ref[e] - rb)

        @pl.when(nb_ref[0] + e < nb_max)
        def _():
            zero_block((nb_ref[0] + e) * rb)

    @pl.when(i >= n_exp)
    def _():
        def issue(t, _):
            for k in range(TOP_K):
                p = pos_ref[0, 0, k * tm + t]
                pltpu.make_async_copy(h_ref.at[pl.ds(t, 1), :], xs_ref.at[pl.ds(p, 1), :],
                                      sem).start()
            return 0

        lax.fori_loop(0, tm, issue, 0, unroll=4)
        for k in range(TOP_K):
            pltpu.make_async_copy(h_ref, xs_ref.at[pl.ds(0, tm), :], sem).wait()


def _dispatch(pend, padded, nblk, pos3, h2, n_rows, tm, rb):
    n, d = h2.shape
    n_exp = pend.shape[0]
    tile = lambda i, *_: (jnp.maximum(i - n_exp, 0), 0)
    return pl.pallas_call(
        functools.partial(_dispatch_kernel, tm=tm, rb=rb, n_exp=n_exp, nb_max=n_rows // rb),
        grid_spec=pltpu.PrefetchScalarGridSpec(
            num_scalar_prefetch=3,
            grid=(n_exp + n // tm,),
            in_specs=[pl.BlockSpec((1, 1, tm * TOP_K), lambda i, *_: tile(i) + (0,),
                                   memory_space=pltpu.SMEM),
                      pl.BlockSpec((tm, d), tile)],
            out_specs=pl.BlockSpec(memory_space=pl.ANY),
            scratch_shapes=[pltpu.VMEM((rb, d), F32), pltpu.SemaphoreType.DMA,
                            pltpu.SemaphoreType.DMA]),
        out_shape=jax.ShapeDtypeStruct((n_rows, d), F32),
        compiler_params=_cparams(("arbitrary",)),
        name="dispatch",
    )(pend, padded, nblk, pos3, h2)


def _experts_kernel(be_ref, xb_ref, nb_ref, x_ref, w1_ref, b1_ref, w2_ref, b2_ref, y_ref,
                    w1b, w2b, *, f):
    i = pl.program_id(0)
    prev = be_ref[jnp.maximum(i - 1, 0)]
    live = i < nb_ref[0]

    @pl.when(live & ((i == 0) | (be_ref[i] != prev)))
    def _():
        w1b[...] = w1_ref[0].astype(BF16)
        w2b[...] = w2_ref[0].astype(BF16)

    @pl.when(live)
    def _():
        x = x_ref[...].astype(BF16)
        gu = jnp.dot(x, w1b[...], preferred_element_type=F32) + b1_ref[0]
        gate = jnp.minimum(gu[:, :f], SWIGLU_LIMIT)
        up = jnp.clip(gu[:, f:], -SWIGLU_LIMIT, SWIGLU_LIMIT)
        glu = gate * _sigmoid(SWIGLU_ALPHA * gate)
        hmid = ((up + 1.0) * glu).astype(BF16)
        y_ref[...] = jnp.dot(hmid, w2b[...], preferred_element_type=F32) + b2_ref[0]

    @pl.when(jnp.logical_not(live))
    def _():
        y_ref[...] = jnp.zeros_like(y_ref)


def _experts(block_e, xblk, nblk, xs, w1, b1, w2, b2, rb):
    n_rows, d = xs.shape
    e, _, f2 = w1.shape
    f = f2 // 2
    nb_max = n_rows // rb
    return pl.pallas_call(
        functools.partial(_experts_kernel, f=f),
        grid_spec=pltpu.PrefetchScalarGridSpec(
            num_scalar_prefetch=3,
            grid=(nb_max,),
            in_specs=[pl.BlockSpec((rb, d), lambda i, be, xb, nb: (xb[i], 0)),
                      pl.BlockSpec((1, d, f2), lambda i, be, xb, nb: (be[i], 0, 0)),
                      pl.BlockSpec((1, 1, f2), lambda i, be, xb, nb: (be[i], 0, 0)),
                      pl.BlockSpec((1, f, d), lambda i, be, xb, nb: (be[i], 0, 0)),
                      pl.BlockSpec((1, 1, d), lambda i, be, xb, nb: (be[i], 0, 0))],
            out_specs=pl.BlockSpec((rb, d), lambda i, be, xb, nb: (i, 0)),
            scratch_shapes=[pltpu.VMEM((d, f2), BF16), pltpu.VMEM((f, d), BF16)]),
        out_shape=jax.ShapeDtypeStruct((n_rows, d), F32),
        compiler_params=_cparams(("arbitrary",), VMEM_LIMIT),
        name="experts",
    )(block_e, xblk, nblk, xs, w1, b1.reshape(e, 1, f2), w2, b2.reshape(e, 1, d))


def _combine_kernel(pos_ref, gate_ref, x1_ref, mod_ref, fg_ref, ys_ref, o_ref, buf, sem, *, tm):
    def issue(t, _):
        for k in range(TOP_K):
            p = pos_ref[0, 0, k * tm + t]
            pltpu.make_async_copy(ys_ref.at[pl.ds(p, 1), :], buf.at[k, pl.ds(t, 1), :], sem).start()
        return 0

    lax.fori_loop(0, tm, issue, 0, unroll=4)
    for k in range(TOP_K):
        pltpu.make_async_copy(ys_ref.at[pl.ds(0, tm), :], buf.at[k], sem).wait()
    gate = gate_ref[0]
    moe = gate[:, 0:1] * buf[0]
    for k in range(1, TOP_K):
        moe = moe + gate[:, k:k + 1] * buf[k]
    x2 = x1_ref[0] + mod_ref[0, 5:6, :] * moe
    o_ref[0] = _rms(x2, fg_ref[...])


def _combine(pos3, gates, x1, mod3, final_g, ys, tm):
    bsz, s, d = x1.shape
    nt = s // tm
    return pl.pallas_call(
        functools.partial(_combine_kernel, tm=tm),
        grid=(bsz, nt),
        in_specs=[pl.BlockSpec((1, 1, tm * TOP_K), lambda b, i: (b * nt + i, 0, 0),
                               memory_space=pltpu.SMEM),
                  pl.BlockSpec((1, tm, TOP_K), lambda b, i: (b, i, 0)),
                  pl.BlockSpec((1, tm, d), lambda b, i: (b, i, 0)),
                  pl.BlockSpec((1, 6, d), lambda b, i: (b, 0, 0)),
                  pl.BlockSpec((1, d), lambda b, i: (0, 0)),
                  pl.BlockSpec(memory_space=pl.ANY)],
        out_specs=pl.BlockSpec((1, tm, d), lambda b, i: (b, i, 0)),
        out_shape=jax.ShapeDtypeStruct((bsz, s, d), F32),
        scratch_shapes=[pltpu.VMEM((TOP_K, tm, d), F32), pltpu.SemaphoreType.DMA],
        compiler_params=_cparams(("arbitrary", "arbitrary"), VMEM_LIMIT),
        name="combine",
    )(pos3, gates, x1, mod3, final_g.reshape(1, d), ys)


def _layer(x, mod3, norm1_g, w_in, lq1, lk1, lq2, lk2, subln_g, ssm_a_re, ssm_a_im, ssm_log_dt,
           ssm_b_re, ssm_b_im, ssm_c_re, ssm_c_im, ssm_d, w_glu, b_glu, w_out, norm2_g,
           w_router, b_router, w1, b1, w2, b2, final_g):
    bsz, s, d = x.shape
    n = bsz * s
    d_attn = ATTN_HEADS * V_DIM
    n_qkv = 3 * d_attn
    d_ssm = w_in.shape[1] - n_qkv
    g = d_ssm // SSM_GROUP_WIDTH
    t = SSM_CHUNK
    nc = s // t

    qkv, u = _inproj(x, mod3, norm1_g, w_in.astype(BF16), n_qkv)
    attn = _attention(qkv, lq1, lk1, lq2, lk2, subln_g)

    u_g = (u.reshape(bsz, nc, t, g, SSM_GROUP_WIDTH).transpose(3, 1, 0, 2, 4)
           .reshape(g, nc * bsz, t * SSM_GROUP_WIDTH).astype(BF16))
    tables = _ssm_tables(ssm_a_re, ssm_a_im, ssm_log_dt, ssm_b_re, ssm_b_im, ssm_c_re, ssm_c_im, t)
    y_g = _ssm(u_g, tables, bsz, nc)
    y = (y_g.reshape(g, nc, bsz, t, SSM_GROUP_WIDTH).transpose(2, 1, 3, 0, 4)
         .reshape(bsz, s, d_ssm))

    x1, h2, logits_t = _mix(x, attn, y, u, mod3, ssm_d, w_glu.astype(BF16), b_glu,
                            w_out.astype(BF16), norm2_g, w_router, b_router)

    idx, gates, rank, counts = _route(logits_t)
    rb = 512 if n * TOP_K >= 512 * N_EXPERTS else 128
    counts = counts.reshape(N_EXPERTS)
    padded = ((counts + rb - 1) // rb) * rb
    pend = jnp.cumsum(padded).astype(jnp.int32)
    pstart = pend - padded
    pos = (pstart[idx] + rank).astype(jnp.int32)
    nb_max = (n * TOP_K) // rb + N_EXPERTS
    n_rows = nb_max * rb
    nblk = pend[-1] // rb
    blk_ids = jnp.minimum(jnp.arange(nb_max, dtype=jnp.int32), nblk - 1)
    block_e = jnp.minimum(jnp.sum((pend[None, :] <= (blk_ids * rb)[:, None]).astype(jnp.int32), axis=1),
                          N_EXPERTS - 1).astype(jnp.int32)
    tmd = 256
    pos3 = (pos.reshape(TOP_K, n // tmd, tmd).transpose(1, 0, 2)
            .reshape(n // tmd, 1, TOP_K * tmd))
    gates_tok = gates.T.reshape(bsz, s, TOP_K)

    nblk = nblk.reshape(1).astype(jnp.int32)
    xs = _dispatch(pend, padded.astype(jnp.int32), nblk, pos3, h2.reshape(n, d), n_rows, tmd, rb)
    ys = _experts(block_e, blk_ids, nblk, xs, w1, b1, w2, b2, rb)
    return _combine(pos3, gates_tok, x1, mod3, final_g, ys, tmd)


def kernel(x, c, w_ada, b_ada, norm1_g, w_in, lq1, lk1, lq2, lk2, subln_g, ssm_a_re, ssm_a_im,
           ssm_log_dt, ssm_b_re, ssm_b_im, ssm_c_re, ssm_c_im, ssm_d, w_glu, b_glu, w_out, norm2_g,
           w_router, b_router, w1, b1, w2, b2, final_g):
    assert w_ada.shape[0] == 1, "single-layer block"
    bsz, s, d = x.shape
    mod3 = _adaln(c, w_ada[0], b_ada[0]).reshape(bsz, 6, d)
    return _layer(x, mod3, norm1_g[0], w_in[0], lq1[0], lk1[0], lq2[0], lk2[0], subln_g[0],
                  ssm_a_re[0], ssm_a_im[0], ssm_log_dt[0], ssm_b_re[0], ssm_b_im[0], ssm_c_re[0],
                  ssm_c_im[0], ssm_d[0], w_glu[0], b_glu[0], w_out[0], norm2_g[0], w_router[0],
                  b_router[0], w1[0], b1[0], w2[0], b2[0], final_g)
```

```python
import functools
import math

import jax
import jax.numpy as jnp
from jax import lax
from jax.experimental import pallas as pl
from jax.experimental.pallas import tpu as pltpu

F32 = jnp.float32
BF16 = jnp.bfloat16
HIGHEST = lax.Precision.HIGHEST

RMS_EPS = 1e-6
MASK_VALUE = -1e30
ATTN_HEADS = 4
QK_DIM = 64
V_DIM = 128
SSM_GROUP_WIDTH = 16
SSM_STATE = 64
SSM_CHUNK = 32
N_EXPERTS = 32
TOP_K = 4
SWIGLU_LIMIT = 7.0
SWIGLU_ALPHA = 1.702
LAMBDA_INIT = 0.8 - 0.6 * math.exp(-0.3 * 0)

VMEM_LIMIT = 56 * 1024 * 1024
NT_DIMS = (((1,), (1,)), ((), ()))
N_QUERY_GROUPS = 1


def _cparams(sem, vmem=None):
    return pltpu.CompilerParams(dimension_semantics=sem, vmem_limit_bytes=vmem)


def _sigmoid(x):
    return 1.0 / (1.0 + jnp.exp(-x))


def _rms(x, g):
    ms = jnp.mean(x * x, axis=-1, keepdims=True)
    return x * lax.rsqrt(ms + RMS_EPS) * g


def _adaln_kernel(c_ref, w_ref, b_ref, o_ref):
    c = c_ref[...]
    ca = c * _sigmoid(c)
    o_ref[...] = jnp.dot(ca, w_ref[...], preferred_element_type=F32, precision=HIGHEST) + b_ref[...]


def _adaln(c, w, b):
    bsz, d = c.shape
    n = w.shape[1]
    tn = 1536
    return pl.pallas_call(
        _adaln_kernel,
        grid=(n // tn,),
        in_specs=[pl.BlockSpec((bsz, d), lambda j: (0, 0)),
                  pl.BlockSpec((d, tn), lambda j: (0, j)),
                  pl.BlockSpec((1, tn), lambda j: (0, j))],
        out_specs=pl.BlockSpec((bsz, tn), lambda j: (0, j)),
        out_shape=jax.ShapeDtypeStruct((bsz, n), F32),
        compiler_params=_cparams(("arbitrary",)),
        name="adaln",
    )(c, w, b.reshape(1, n))


def _inproj_kernel(x_ref, mod_ref, g_ref, w_ref, qkv_ref, u_ref, *, n_qkv):
    x = x_ref[0]
    y = _rms(x, g_ref[...])
    h = y * (1.0 + mod_ref[0, 1:2, :]) + mod_ref[0, 0:1, :]
    p = jnp.dot(h.astype(BF16), w_ref[...], preferred_element_type=F32)
    qkv_ref[0] = p[:, :n_qkv].astype(BF16)
    u_ref[0] = p[:, n_qkv:]


def _inproj(x, mod3, g, w_bf16, n_qkv, tm=512):
    bsz, s, d = x.shape
    n = w_bf16.shape[1]
    n_u = n - n_qkv
    return pl.pallas_call(
        functools.partial(_inproj_kernel, n_qkv=n_qkv),
        grid=(bsz, s // tm),
        in_specs=[pl.BlockSpec((1, tm, d), lambda b, i: (b, i, 0)),
                  pl.BlockSpec((1, 6, d), lambda b, i: (b, 0, 0)),
                  pl.BlockSpec((1, d), lambda b, i: (0, 0)),
                  pl.BlockSpec((d, n), lambda b, i: (0, 0))],
        out_specs=[pl.BlockSpec((1, tm, n_qkv), lambda b, i: (b, i, 0)),
                   pl.BlockSpec((1, tm, n_u), lambda b, i: (b, i, 0))],
        out_shape=[jax.ShapeDtypeStruct((bsz, s, n_qkv), BF16),
                   jax.ShapeDtypeStruct((bsz, s, n_u), F32)],
        compiler_params=_cparams(("arbitrary", "arbitrary"), VMEM_LIMIT),
        name="inproj",
    )(x, mod3, g.reshape(1, d), w_bf16)


def _attn_kernel(lq1_ref, lk1_ref, lq2_ref, lk2_ref, sg_ref, q_ref, k_ref, v_ref, o_ref,
                 vt_ref, m_ref, acc_ref, *, tq):
    qi = pl.program_id(2)
    n_acc = acc_ref.shape[0]

    @pl.when(qi == 0)
    def _():
        vt_ref[:V_DIM, :] = v_ref[0].astype(F32).T.astype(BF16)
        vt_ref[V_DIM:, :] = jnp.ones((n_acc - V_DIM, vt_ref.shape[1]), BF16)

    lam = (jnp.exp(jnp.sum(lq1_ref[...] * lk1_ref[...], axis=-1, keepdims=True))
           - jnp.exp(jnp.sum(lq2_ref[...] * lk2_ref[...], axis=-1, keepdims=True))
           + LAMBDA_INIT)
    q = q_ref[0] * (QK_DIM ** -0.5)
    lane = lax.broadcasted_iota(jnp.int32, q.shape, 1)
    zero = jnp.zeros_like(q)
    qs = jnp.concatenate([jnp.where(lane < QK_DIM, q, zero), jnp.where(lane >= QK_DIM, q, zero)],
                         axis=0)
    m_ref[...] = jnp.full(m_ref.shape, MASK_VALUE, F32)
    acc_ref[...] = jnp.zeros(acc_ref.shape, F32)

    def block(ki, mask):
        start = pl.multiple_of(ki * tq, tq)
        kb = k_ref[0, pl.ds(start, tq), :]
        vt = vt_ref[:, pl.ds(start, tq)]
        w = 2 * tq // N_QUERY_GROUPS
        for c in range(N_QUERY_GROUPS):
            cols = slice(c * w, (c + 1) * w)
            st = lax.dot_general(kb, qs[cols], NT_DIMS, preferred_element_type=F32)
            if mask is not None:
                st = jnp.where(mask[:, cols], st, MASK_VALUE)
            m_prev = m_ref[:, cols]
            m_new = jnp.maximum(m_prev, jnp.max(st, axis=0, keepdims=True))
            pt = jnp.exp(st - m_new).astype(BF16)
            acc_ref[:, cols] = (jnp.exp(m_prev - m_new) * acc_ref[:, cols]
                                + jnp.dot(vt, pt, preferred_element_type=F32))
            m_ref[:, cols] = m_new

    def body(ki, carry):
        block(ki, None)
        return carry

    lax.fori_loop(0, qi, body, 0)
    key = lax.broadcasted_iota(jnp.int32, (tq, 2 * tq), 0)
    qry = lax.broadcasted_iota(jnp.int32, (tq, 2 * tq), 1)
    block(qi, key <= jnp.where(qry >= tq, qry - tq, qry))
    a = acc_ref[...]
    ot = (a[:V_DIM, :tq] / a[V_DIM:V_DIM + 1, :tq]
          - lam * (a[:V_DIM, tq:] / a[V_DIM:V_DIM + 1, tq:]))
    o = _rms(ot.T, sg_ref[...]) * (1.0 - LAMBDA_INIT)
    o_ref[0] = o.astype(BF16)


def _attention(qkv, lq1, lk1, lq2, lk2, subln_g, tq=1024):
    bsz, s, _ = qkv.shape
    tq = min(tq, s)
    h = ATTN_HEADS
    n_acc = V_DIM + 8
    vec = lambda n: pl.BlockSpec((1, n), lambda b, hh, i: (0, 0))
    return pl.pallas_call(
        functools.partial(_attn_kernel, tq=tq),
        grid=(bsz, h, s // tq),
        in_specs=[vec(QK_DIM), vec(QK_DIM), vec(QK_DIM), vec(QK_DIM), vec(V_DIM),
                  pl.BlockSpec((1, tq, V_DIM), lambda b, hh, i: (b, i, hh)),
                  pl.BlockSpec((1, s, V_DIM), lambda b, hh, i: (b, 0, h + hh)),
                  pl.BlockSpec((1, s, V_DIM), lambda b, hh, i: (b, 0, 2 * h + hh))],
        out_specs=pl.BlockSpec((1, tq, V_DIM), lambda b, hh, i: (b, i, hh)),
        out_shape=jax.ShapeDtypeStruct((bsz, s, h * V_DIM), BF16),
        scratch_shapes=[pltpu.VMEM((n_acc, s), BF16),
                        pltpu.VMEM((1, 2 * tq), F32), pltpu.VMEM((n_acc, 2 * tq), F32)],
        compiler_params=_cparams(("arbitrary", "arbitrary", "arbitrary"), VMEM_LIMIT),
        name="attention",
    )(lq1.reshape(1, -1), lk1.reshape(1, -1), lq2.reshape(1, -1), lk2.reshape(1, -1),
      subln_g.reshape(1, -1), qkv, qkv, qkv)


def _ssm_tables(a_re, a_im, log_dt, b_re, b_im, c_re, c_im, t):
    g, p = a_re.shape
    w = b_re.shape[-1]
    dt = jnp.exp(log_dt.astype(F32))[:, None]
    lam = lax.complex(jnp.minimum(a_re.astype(F32), -1e-4), a_im.astype(F32))
    lam_dt = lam * dt
    lam_bar = jnp.exp(lam_dt)
    b_bar = ((lam_bar - 1.0) / lam)[..., None] * lax.complex(b_re.astype(F32), b_im.astype(F32))
    c_cplx = lax.complex(c_re.astype(F32), c_im.astype(F32))
    tau = jnp.arange(t + 1, dtype=F32)
    pw = jnp.exp(lam_dt[:, None, :] * tau[None, :, None])
    kern = jnp.einsum('gcp,gtp,gpd->gtcd', c_cplx, pw[:, :t], b_bar).real
    kern = jnp.concatenate([kern, jnp.zeros((g, 1, w, w), F32)], axis=1)
    jj = jnp.arange(t)[:, None]
    tt = jnp.arange(t)[None, :]
    lag = jnp.where(tt >= jj, tt - jj, t)
    toep = kern[:, lag]
    toep = toep.transpose(0, 1, 4, 2, 3).reshape(g, t * w, t * w)
    wmat = pw[:, t - 1 - jnp.arange(t), :, None] * b_bar[:, None]
    wmat = wmat.transpose(0, 1, 3, 2).reshape(g, t * w, p)
    vmat = c_cplx[:, None] * pw[:, 1:t + 1, None, :]
    vmat = vmat.transpose(0, 3, 1, 2).reshape(g, p, t * w)
    lt = pw[:, t][:, None, :]
    return (toep.astype(BF16), wmat.real.astype(BF16), wmat.imag.astype(BF16),
            vmat.real.astype(BF16), (-vmat.imag).astype(BF16), lt.real, lt.imag)


def _ssm_kernel(u_ref, toep_ref, wre_ref, wim_ref, vre_ref, vim_ref, ltre_ref, ltim_ref, y_ref,
                slre, slim, spre, spim, *, bsz, nc):
    u = u_ref[0]
    slre[...] = jnp.dot(u, wre_ref[0], preferred_element_type=F32)
    slim[...] = jnp.dot(u, wim_ref[0], preferred_element_type=F32)
    a = ltre_ref[0]
    b = ltim_ref[0]

    def step(c, carry):
        sre, sim = carry
        rows = pl.ds(pl.multiple_of(c * bsz, bsz), bsz)
        spre[rows, :] = sre
        spim[rows, :] = sim
        nre = a * sre - b * sim + slre[rows, :]
        nim = a * sim + b * sre + slim[rows, :]
        return nre, nim

    z = jnp.zeros((bsz, a.shape[-1]), F32)
    lax.fori_loop(0, nc, step, (z, z))
    y = jnp.dot(u, toep_ref[0], preferred_element_type=F32)
    y = y + jnp.dot(spre[...].astype(BF16), vre_ref[0], preferred_element_type=F32)
    y = y + jnp.dot(spim[...].astype(BF16), vim_ref[0], preferred_element_type=F32)
    y_ref[0] = y


def _ssm(u_g, tables, bsz, nc):
    toep, wre, wim, vre, vim, ltre, ltim = tables
    g, r, tw = u_g.shape
    p = wre.shape[-1]
    blk = lambda a, b: pl.BlockSpec((1, a, b), lambda i: (i, 0, 0))
    return pl.pallas_call(
        functools.partial(_ssm_kernel, bsz=bsz, nc=nc),
        grid=(g,),
        in_specs=[blk(r, tw), blk(tw, tw), blk(tw, p), blk(tw, p), blk(p, tw), blk(p, tw),
                  blk(1, p), blk(1, p)],
        out_specs=blk(r, tw),
        out_shape=jax.ShapeDtypeStruct((g, r, tw), F32),
        scratch_shapes=[pltpu.VMEM((r, p), F32)] * 4,
        compiler_params=_cparams(("arbitrary",), VMEM_LIMIT),
        name="ssm",
    )(u_g, toep, wre, wim, vre, vim, ltre, ltim)


def _mix_kernel(x_ref, attn_ref, y_ref, u_ref, mod_ref, d_ref, wglu_ref, bglu_ref, wout_ref,
                g2_ref, wrt_ref, br_ref, x1_ref, h2_ref, lgt_ref, *, d_attn):
    y = y_ref[0] + d_ref[...] * u_ref[0]
    z = 0.5 * y * (1.0 + jnp.tanh(math.sqrt(2.0 / math.pi) * (y + 0.044715 * (y * y * y))))
    gl = jnp.dot(z.astype(BF16), wglu_ref[...], preferred_element_type=F32) + bglu_ref[...]
    so = z * _sigmoid(gl)
    mix = (jnp.dot(attn_ref[0], wout_ref[:d_attn, :], preferred_element_type=F32)
           + jnp.dot(so.astype(BF16), wout_ref[d_attn:, :], preferred_element_type=F32))
    x1 = x_ref[0] + mod_ref[0, 2:3, :] * mix
    x1_ref[0] = x1
    h2 = _rms(x1, g2_ref[...]) * (1.0 + mod_ref[0, 4:5, :]) + mod_ref[0, 3:4, :]
    h2_ref[0] = h2
    lgt_ref[...] = lax.dot_general(wrt_ref[...], h2, NT_DIMS, preferred_element_type=F32,
                                   precision=HIGHEST) + br_ref[...]


def _mix(x, attn, y, u, mod3, d_skip, wglu, bglu, wout, g2, wr, br, tm=512):
    bsz, s, d = x.shape
    da = attn.shape[-1]
    ds_ = y.shape[-1]
    e = wr.shape[-1]
    nt = s // tm
    tok = lambda n: pl.BlockSpec((1, tm, n), lambda b, i: (b, i, 0))
    full = lambda a, b_: pl.BlockSpec((a, b_), lambda b, i: (0, 0))
    return pl.pallas_call(
        functools.partial(_mix_kernel, d_attn=da),
        grid=(bsz, nt),
        in_specs=[tok(d), tok(da), tok(ds_), tok(ds_),
                  pl.BlockSpec((1, 6, d), lambda b, i: (b, 0, 0)),
                  full(1, ds_), full(ds_, ds_), full(1, ds_), full(da + ds_, d),
                  full(1, d), full(e, d), full(e, 1)],
        out_specs=[tok(d), tok(d), pl.BlockSpec((e, tm), lambda b, i: (0, b * nt + i))],
        out_shape=[jax.ShapeDtypeStruct((bsz, s, d), F32),
                   jax.ShapeDtypeStruct((bsz, s, d), F32),
                   jax.ShapeDtypeStruct((e, bsz * s), F32)],
        compiler_params=_cparams(("arbitrary", "arbitrary"), VMEM_LIMIT),
        name="mix",
    )(x, attn, y, u, mod3, d_skip.reshape(1, -1), wglu, bglu.reshape(1, -1), wout,
      g2.reshape(1, -1), wr.T, br.reshape(-1, 1))


def _route_kernel(lg_ref, idx_ref, gate_ref, rank_ref, cnt_ref, run_ref, *, tm):
    i = pl.program_id(0)

    @pl.when(i == 0)
    def _():
        run_ref[...] = jnp.zeros_like(run_ref)

    l = lg_ref[...]
    e = l.shape[0]
    sub = lax.broadcasted_iota(jnp.int32, l.shape, 0)
    vals, sels, idxs = [], [], []
    for _k in range(TOP_K):
        mx = jnp.max(l, axis=0, keepdims=True)
        ix = jnp.min(jnp.where(l == mx, sub, e), axis=0, keepdims=True)
        sel = sub == ix
        vals.append(mx)
        idxs.append(ix)
        sels.append(sel)
        l = jnp.where(sel, -jnp.inf, l)
    ex = [jnp.exp(v - vals[0]) for v in vals]
    den = ex[0] + ex[1] + ex[2] + ex[3]
    chosen = jnp.zeros(l.shape, F32)
    for sel in sels:
        chosen = chosen + jnp.where(sel, 1.0, 0.0)
    r_i = lax.broadcasted_iota(jnp.int32, (tm, tm), 0)
    c_i = lax.broadcasted_iota(jnp.int32, (tm, tm), 1)
    tri = jnp.where(r_i < c_i, 1.0, 0.0).astype(BF16)
    before = jnp.dot(chosen.astype(BF16), tri, preferred_element_type=F32) + run_ref[...]
    ksub = lax.broadcasted_iota(jnp.int32, (TOP_K, tm), 0)
    idx_o = jnp.zeros((TOP_K, tm), jnp.int32)
    gate_o = jnp.zeros((TOP_K, tm), F32)
    rank_o = jnp.zeros((TOP_K, tm), F32)
    for k in range(TOP_K):
        rk = jnp.sum(jnp.where(sels[k], before, 0.0), axis=0, keepdims=True)
        idx_o = jnp.where(ksub == k, idxs[k], idx_o)
        gate_o = jnp.where(ksub == k, ex[k] / den, gate_o)
        rank_o = jnp.where(ksub == k, rk, rank_o)
    idx_ref[...] = idx_o
    gate_ref[...] = gate_o
    rank_ref[...] = rank_o.astype(jnp.int32)
    run_ref[...] = run_ref[...] + jnp.sum(chosen, axis=1, keepdims=True)
    cnt_ref[...] = run_ref[...].astype(jnp.int32)


def _route(logits_t, tm=512):
    e, n = logits_t.shape
    tok = lambda w: pl.BlockSpec((w, tm), lambda i: (0, i))
    return pl.pallas_call(
        functools.partial(_route_kernel, tm=tm),
        grid=(n // tm,),
        in_specs=[tok(e)],
        out_specs=[tok(TOP_K), tok(TOP_K), tok(TOP_K), pl.BlockSpec((e, 1), lambda i: (0, 0))],
        out_shape=[jax.ShapeDtypeStruct((TOP_K, n), jnp.int32),
                   jax.ShapeDtypeStruct((TOP_K, n), F32),
                   jax.ShapeDtypeStruct((TOP_K, n), jnp.int32),
                   jax.ShapeDtypeStruct((e, 1), jnp.int32)],
        scratch_shapes=[pltpu.VMEM((e, 1), F32)],
        compiler_params=_cparams(("arbitrary",)),
        name="route",
    )(logits_t)


def _dispatch_kernel(pend_ref, padded_ref, nb_ref, pos_ref, h_ref, xs_ref, zero_ref, sem, zsem,
                     *, tm, rb, n_exp, nb_max):
    i = pl.program_id(0)

    def zero_block(blk_start):
        cp = pltpu.make_async_copy(zero_ref, xs_ref.at[pl.ds(pl.multiple_of(blk_start, rb), rb), :],
                                   zsem)
        cp.start()
        cp.wait()

    @pl.when(i == 0)
    def _():
        zero_ref[...] = jnp.zeros_like(zero_ref)

    @pl.when(i < n_exp)
    def _():
        e = jnp.minimum(i, n_exp - 1)

        @pl.when(padded_ref[e] > 0)
        def _():
            zero_block(pend_ref[e] - rb)

        @pl.when(nb_ref[0] + e < nb_max)
        def _():
            zero_block((nb_ref[0] + e) * rb)

    @pl.when(i >= n_exp)
    def _():
        def issue(t, _):
            for k in range(TOP_K):
                p = pos_ref[0, 0, k * tm + t]
                pltpu.make_async_copy(h_ref.at[pl.ds(t, 1), :], xs_ref.at[pl.ds(p, 1), :],
                                      sem).start(priority=k % 2)
            return 0

        lax.fori_loop(0, tm, issue, 0, unroll=4)
        for k in range(TOP_K):
            pltpu.make_async_copy(h_ref, xs_ref.at[pl.ds(0, tm), :], sem).wait()


def _dispatch(pend, padded, nblk, pos3, h2, n_rows, tm, rb):
    n, d = h2.shape
    n_exp = pend.shape[0]
    tile = lambda i, *_: (jnp.maximum(i - n_exp, 0), 0)
    return pl.pallas_call(
        functools.partial(_dispatch_kernel, tm=tm, rb=rb, n_exp=n_exp, nb_max=n_rows // rb),
        grid_spec=pltpu.PrefetchScalarGridSpec(
            num_scalar_prefetch=3,
            grid=(n_exp + n // tm,),
            in_specs=[pl.BlockSpec((1, 1, tm * TOP_K), lambda i, *_: tile(i) + (0,),
                                   memory_space=pltpu.SMEM),
                      pl.BlockSpec((tm, d), tile)],
            out_specs=pl.BlockSpec(memory_space=pl.ANY),
            scratch_shapes=[pltpu.VMEM((rb, d), F32), pltpu.SemaphoreType.DMA,
                            pltpu.SemaphoreType.DMA]),
        out_shape=jax.ShapeDtypeStruct((n_rows, d), F32),
        compiler_params=_cparams(("arbitrary",)),
        name="dispatch",
    )(pend, padded, nblk, pos3, h2)


def _experts_kernel(be_ref, xb_ref, nb_ref, x_ref, w1_ref, b1_ref, w2_ref, b2_ref, y_ref,
                    w1b, w2b, *, f):
    i = pl.program_id(0)
    prev = be_ref[jnp.maximum(i - 1, 0)]
    live = i < nb_ref[0]

    @pl.when(live & ((i == 0) | (be_ref[i] != prev)))
    def _():
        w1b[...] = w1_ref[0].astype(BF16)
        w2b[...] = w2_ref[0].astype(BF16)

    @pl.when(live)
    def _():
        x = x_ref[...].astype(BF16)
        gu = jnp.dot(x, w1b[...], preferred_element_type=F32) + b1_ref[0]
        gate = jnp.minimum(gu[:, :f], SWIGLU_LIMIT)
        up = jnp.clip(gu[:, f:], -SWIGLU_LIMIT, SWIGLU_LIMIT)
        glu = gate * _sigmoid(SWIGLU_ALPHA * gate)
        hmid = ((up + 1.0) * glu).astype(BF16)
        y_ref[...] = jnp.dot(hmid, w2b[...], preferred_element_type=F32) + b2_ref[0]

    @pl.when(jnp.logical_not(live))
    def _():
        y_ref[...] = jnp.zeros_like(y_ref)


def _experts(block_e, xblk, nblk, xs, w1, b1, w2, b2, rb):
    n_rows, d = xs.shape
    e, _, f2 = w1.shape
    f = f2 // 2
    nb_max = n_rows // rb
    return pl.pallas_call(
        functools.partial(_experts_kernel, f=f),
        grid_spec=pltpu.PrefetchScalarGridSpec(
            num_scalar_prefetch=3,
            grid=(nb_max,),
            in_specs=[pl.BlockSpec((rb, d), lambda i, be, xb, nb: (xb[i], 0)),
                      pl.BlockSpec((1, d, f2), lambda i, be, xb, nb: (be[i], 0, 0)),
                      pl.BlockSpec((1, 1, f2), lambda i, be, xb, nb: (be[i], 0, 0)),
                      pl.BlockSpec((1, f, d), lambda i, be, xb, nb: (be[i], 0, 0)),
                      pl.BlockSpec((1, 1, d), lambda i, be, xb, nb: (be[i], 0, 0))],
            out_specs=pl.BlockSpec((rb, d), lambda i, be, xb, nb: (i, 0)),
            scratch_shapes=[pltpu.VMEM((d, f2), BF16), pltpu.VMEM((f, d), BF16)]),
        out_shape=jax.ShapeDtypeStruct((n_rows, d), F32),
        compiler_params=_cparams(("arbitrary",), VMEM_LIMIT),
        name="experts",
    )(block_e, xblk, nblk, xs, w1, b1.reshape(e, 1, f2), w2, b2.reshape(e, 1, d))


def _combine_kernel(pos_ref, nxt_ref, gate_ref, x1_ref, mod_ref, fg_ref, ys_ref, o_ref, buf, sems,
                    *, tm):
    i = pl.program_id(0)
    n_tiles = pl.num_programs(0)
    slot = lax.rem(i, 2)

    def gather(p_ref, dst_slot):
        def issue(t, _):
            for k in range(TOP_K):
                p = p_ref[0, 0, k * tm + t]
                pltpu.make_async_copy(ys_ref.at[pl.ds(p, 1), :],
                                      buf.at[dst_slot, k, pl.ds(t, 1), :],
                                      sems.at[dst_slot]).start(priority=k % 2)
            return 0

        lax.fori_loop(0, tm, issue, 0, unroll=4)

    @pl.when(i == 0)
    def _():
        gather(pos_ref, 0)

    @pl.when(i + 1 < n_tiles)
    def _():
        gather(nxt_ref, 1 - slot)

    for k in range(TOP_K):
        pltpu.make_async_copy(ys_ref.at[pl.ds(0, tm), :], buf.at[slot, k], sems.at[slot]).wait()
    gate = gate_ref[...]
    moe = gate[:, 0:1] * buf[slot, 0]
    for k in range(1, TOP_K):
        moe = moe + gate[:, k:k + 1] * buf[slot, k]
    x2 = x1_ref[...] + mod_ref[0, 5:6, :] * moe
    o_ref[...] = _rms(x2, fg_ref[...])


def _combine(pos3, gates, x1, mod3, final_g, ys, tm):
    bsz, s, d = x1.shape
    n = bsz * s
    nt = s // tm
    n_tiles = n // tm
    pos_spec = lambda f: pl.BlockSpec((1, 1, tm * TOP_K), lambda i: (f(i), 0, 0),
                                      memory_space=pltpu.SMEM)
    out = pl.pallas_call(
        functools.partial(_combine_kernel, tm=tm),
        grid=(n_tiles,),
        in_specs=[pos_spec(lambda i: i),
                  pos_spec(lambda i: jnp.minimum(i + 1, n_tiles - 1)),
                  pl.BlockSpec((tm, TOP_K), lambda i: (i, 0)),
                  pl.BlockSpec((tm, d), lambda i: (i, 0)),
                  pl.BlockSpec((1, 6, d), lambda i: (i // nt, 0, 0)),
                  pl.BlockSpec((1, d), lambda i: (0, 0)),
                  pl.BlockSpec(memory_space=pl.ANY)],
        out_specs=pl.BlockSpec((tm, d), lambda i: (i, 0)),
        out_shape=jax.ShapeDtypeStruct((n, d), F32),
        scratch_shapes=[pltpu.VMEM((2, TOP_K, tm, d), F32), pltpu.SemaphoreType.DMA((2,))],
        compiler_params=_cparams(("arbitrary",), VMEM_LIMIT),
        name="combine",
    )(pos3, pos3, gates, x1.reshape(n, d), mod3, final_g.reshape(1, d), ys)
    return out.reshape(bsz, s, d)


def _layer(x, mod3, norm1_g, w_in, lq1, lk1, lq2, lk2, subln_g, ssm_a_re, ssm_a_im, ssm_log_dt,
           ssm_b_re, ssm_b_im, ssm_c_re, ssm_c_im, ssm_d, w_glu, b_glu, w_out, norm2_g,
           w_router, b_router, w1, b1, w2, b2, final_g):
    bsz, s, d = x.shape
    n = bsz * s
    d_attn = ATTN_HEADS * V_DIM
    n_qkv = 3 * d_attn
    d_ssm = w_in.shape[1] - n_qkv
    g = d_ssm // SSM_GROUP_WIDTH
    t = SSM_CHUNK
    nc = s // t

    qkv, u = _inproj(x, mod3, norm1_g, w_in.astype(BF16), n_qkv)
    attn = _attention(qkv, lq1, lk1, lq2, lk2, subln_g)

    u_g = (u.reshape(bsz, nc, t, g, SSM_GROUP_WIDTH).transpose(3, 1, 0, 2, 4)
           .reshape(g, nc * bsz, t * SSM_GROUP_WIDTH).astype(BF16))
    tables = _ssm_tables(ssm_a_re, ssm_a_im, ssm_log_dt, ssm_b_re, ssm_b_im, ssm_c_re, ssm_c_im, t)
    y_g = _ssm(u_g, tables, bsz, nc)
    y = (y_g.reshape(g, nc, bsz, t, SSM_GROUP_WIDTH).transpose(2, 1, 3, 0, 4)
         .reshape(bsz, s, d_ssm))

    x1, h2, logits_t = _mix(x, attn, y, u, mod3, ssm_d, w_glu.astype(BF16), b_glu,
                            w_out.astype(BF16), norm2_g, w_router, b_router)

    idx, gates, rank, counts = _route(logits_t)
    rb = 512 if n * TOP_K >= 512 * N_EXPERTS else 128
    counts = counts.reshape(N_EXPERTS)
    padded = ((counts + rb - 1) // rb) * rb
    pend = jnp.cumsum(padded).astype(jnp.int32)
    pstart = pend - padded
    eids = jnp.arange(N_EXPERTS, dtype=jnp.int32)[:, None, None]
    pos = rank + jnp.sum(jnp.where(idx[None] == eids, pstart[:, None, None], 0), axis=0)
    pos = pos.astype(jnp.int32)
    nb_max = (n * TOP_K) // rb + N_EXPERTS
    n_rows = nb_max * rb
    nblk = pend[-1] // rb
    blk_ids = jnp.minimum(jnp.arange(nb_max, dtype=jnp.int32), nblk - 1)
    block_e = jnp.minimum(jnp.sum((pend[None, :] <= (blk_ids * rb)[:, None]).astype(jnp.int32), axis=1),
                          N_EXPERTS - 1).astype(jnp.int32)
    tmd = 256
    pos3 = (pos.reshape(TOP_K, n // tmd, tmd).transpose(1, 0, 2)
            .reshape(n // tmd, 1, TOP_K * tmd))
    gates_tok = gates.T

    nblk = nblk.reshape(1).astype(jnp.int32)
    xs = _dispatch(pend, padded.astype(jnp.int32), nblk, pos3, h2.reshape(n, d), n_rows, tmd, rb)
    ys = _experts(block_e, blk_ids, nblk, xs, w1, b1, w2, b2, rb)
    return _combine(pos3, gates_tok, x1, mod3, final_g, ys, tmd)


def kernel(x, c, w_ada, b_ada, norm1_g, w_in, lq1, lk1, lq2, lk2, subln_g, ssm_a_re, ssm_a_im,
           ssm_log_dt, ssm_b_re, ssm_b_im, ssm_c_re, ssm_c_im, ssm_d, w_glu, b_glu, w_out, norm2_g,
           w_router, b_router, w1, b1, w2, b2, final_g):
    assert w_ada.shape[0] == 1, "single-layer block"
    bsz, s, d = x.shape
    mod3 = _adaln(c, w_ada[0], b_ada[0]).reshape(bsz, 6, d)
    return _layer(x, mod3, norm1_g[0], w_in[0], lq1[0], lk1[0], lq2[0], lk2[0], subln_g[0],
                  ssm_a_re[0], ssm_a_im[0], ssm_log_dt[0], ssm_b_re[0], ssm_b_im[0], ssm_c_re[0],
                  ssm_c_im[0], ssm_d[0], w_glu[0], b_glu[0], w_out[0], norm2_g[0], w_router[0],
                  b_router[0], w1[0], b1[0], w2[0], b2[0], final_g)
```

```python
import functools
import math

import jax
import jax.numpy as jnp
from jax import lax
from jax.experimental import pallas as pl
from jax.experimental.pallas import tpu as pltpu

F32 = jnp.float32
BF16 = jnp.bfloat16
HIGHEST = lax.Precision.HIGHEST

RMS_EPS = 1e-6
MASK_VALUE = -1e30
ATTN_HEADS = 4
QK_DIM = 64
V_DIM = 128
SSM_GROUP_WIDTH = 16
SSM_STATE = 64
SSM_CHUNK = 128
N_EXPERTS = 32
TOP_K = 4
SWIGLU_LIMIT = 7.0
SWIGLU_ALPHA = 1.702
LAMBDA_INIT = 0.8 - 0.6 * math.exp(-0.3 * 0)

VMEM_LIMIT = 56 * 1024 * 1024
NT_DIMS = (((1,), (1,)), ((), ()))
N_QUERY_GROUPS = 1


def _cparams(sem, vmem=None):
    return pltpu.CompilerParams(dimension_semantics=sem, vmem_limit_bytes=vmem)


def _sigmoid(x):
    return 1.0 / (1.0 + jnp.exp(-x))


def _rms(x, g):
    ms = jnp.mean(x * x, axis=-1, keepdims=True)
    return x * lax.rsqrt(ms + RMS_EPS) * g


def _adaln_kernel(c_ref, w_ref, b_ref, o_ref):
    c = c_ref[...]
    ca = c * _sigmoid(c)
    o_ref[...] = jnp.dot(ca, w_ref[...], preferred_element_type=F32, precision=HIGHEST) + b_ref[...]


def _adaln(c, w, b):
    bsz, d = c.shape
    n = w.shape[1]
    tn = 1536
    return pl.pallas_call(
        _adaln_kernel,
        grid=(n // tn,),
        in_specs=[pl.BlockSpec((bsz, d), lambda j: (0, 0)),
                  pl.BlockSpec((d, tn), lambda j: (0, j)),
                  pl.BlockSpec((1, tn), lambda j: (0, j))],
        out_specs=pl.BlockSpec((bsz, tn), lambda j: (0, j)),
        out_shape=jax.ShapeDtypeStruct((bsz, n), F32),
        compiler_params=_cparams(("arbitrary",)),
        name="adaln",
    )(c, w, b.reshape(1, n))


def _inproj_kernel(x_ref, mod_ref, g_ref, wq_ref, wut_ref, qkv_ref, ut_ref):
    x = x_ref[0]
    y = _rms(x, g_ref[...])
    h = (y * (1.0 + mod_ref[0, 1:2, :]) + mod_ref[0, 0:1, :]).astype(BF16)
    qkv_ref[0] = jnp.dot(h, wq_ref[...], preferred_element_type=F32).astype(BF16)
    ut = lax.dot_general(wut_ref[...], h, NT_DIMS, preferred_element_type=F32)
    for c in range(ut_ref.shape[2]):
        ut_ref[0, :, c, :] = ut[:, c * SSM_CHUNK:(c + 1) * SSM_CHUNK]


def _inproj(x, mod3, g, wq_bf16, wut_bf16, tm=1024):
    bsz, s, d = x.shape
    tm = min(tm, s)
    n_qkv = wq_bf16.shape[1]
    n_u = wut_bf16.shape[0]
    nck = tm // SSM_CHUNK
    return pl.pallas_call(
        _inproj_kernel,
        grid=(bsz, s // tm),
        in_specs=[pl.BlockSpec((1, tm, d), lambda b, i: (b, i, 0)),
                  pl.BlockSpec((1, 6, d), lambda b, i: (b, 0, 0)),
                  pl.BlockSpec((1, d), lambda b, i: (0, 0)),
                  pl.BlockSpec((d, n_qkv), lambda b, i: (0, 0)),
                  pl.BlockSpec((n_u, d), lambda b, i: (0, 0))],
        out_specs=[pl.BlockSpec((1, tm, n_qkv), lambda b, i: (b, i, 0)),
                   pl.BlockSpec((1, n_u, nck, SSM_CHUNK), lambda b, i: (b, 0, i, 0))],
        out_shape=[jax.ShapeDtypeStruct((bsz, s, n_qkv), BF16),
                   jax.ShapeDtypeStruct((bsz, n_u, s // SSM_CHUNK, SSM_CHUNK), F32)],
        compiler_params=_cparams(("arbitrary", "arbitrary"), VMEM_LIMIT),
        name="inproj",
    )(x, mod3, g.reshape(1, d), wq_bf16, wut_bf16)


def _attn_kernel(lq1_ref, lk1_ref, lq2_ref, lk2_ref, sg_ref, q_ref, k_ref, v_ref, o_ref,
                 vt_ref, m_ref, acc_ref, *, tq):
    qi = pl.program_id(2)
    n_acc = acc_ref.shape[0]

    @pl.when(qi == 0)
    def _():
        vt_ref[:V_DIM, :] = v_ref[0].astype(F32).T.astype(BF16)
        vt_ref[V_DIM:, :] = jnp.ones((n_acc - V_DIM, vt_ref.shape[1]), BF16)

    lam = (jnp.exp(jnp.sum(lq1_ref[...] * lk1_ref[...], axis=-1, keepdims=True))
           - jnp.exp(jnp.sum(lq2_ref[...] * lk2_ref[...], axis=-1, keepdims=True))
           + LAMBDA_INIT)
    q = q_ref[0] * (QK_DIM ** -0.5)
    lane = lax.broadcasted_iota(jnp.int32, q.shape, 1)
    zero = jnp.zeros_like(q)
    qs = jnp.concatenate([jnp.where(lane < QK_DIM, q, zero), jnp.where(lane >= QK_DIM, q, zero)],
                         axis=0)
    m_ref[...] = jnp.full(m_ref.shape, MASK_VALUE, F32)
    acc_ref[...] = jnp.zeros(acc_ref.shape, F32)

    def block(ki, mask):
        start = pl.multiple_of(ki * tq, tq)
        kb = k_ref[0, pl.ds(start, tq), :]
        vt = vt_ref[:, pl.ds(start, tq)]
        w = 2 * tq // N_QUERY_GROUPS
        for c in range(N_QUERY_GROUPS):
            cols = slice(c * w, (c + 1) * w)
            st = lax.dot_general(kb, qs[cols], NT_DIMS, preferred_element_type=F32)
            if mask is not None:
                st = jnp.where(mask[:, cols], st, MASK_VALUE)
            m_prev = m_ref[:, cols]
            m_new = jnp.maximum(m_prev, jnp.max(st, axis=0, keepdims=True))
            pt = jnp.exp(st - m_new).astype(BF16)
            acc_ref[:, cols] = (jnp.exp(m_prev - m_new) * acc_ref[:, cols]
                                + jnp.dot(vt, pt, preferred_element_type=F32))
            m_ref[:, cols] = m_new

    def body(ki, carry):
        block(ki, None)
        return carry

    lax.fori_loop(0, qi, body, 0)
    key = lax.broadcasted_iota(jnp.int32, (tq, 2 * tq), 0)
    qry = lax.broadcasted_iota(jnp.int32, (tq, 2 * tq), 1)
    block(qi, key <= jnp.where(qry >= tq, qry - tq, qry))
    a = acc_ref[...]
    ot = (a[:V_DIM, :tq] / a[V_DIM:V_DIM + 1, :tq]
          - lam * (a[:V_DIM, tq:] / a[V_DIM:V_DIM + 1, tq:]))
    o = _rms(ot.T, sg_ref[...]) * (1.0 - LAMBDA_INIT)
    o_ref[0] = o.astype(BF16)


def _attention(qkv, lq1, lk1, lq2, lk2, subln_g, tq=1024):
    bsz, s, _ = qkv.shape
    tq = min(tq, s)
    h = ATTN_HEADS
    n_acc = V_DIM + 8
    vec = lambda n: pl.BlockSpec((1, n), lambda b, hh, i: (0, 0))
    return pl.pallas_call(
        functools.partial(_attn_kernel, tq=tq),
        grid=(bsz, h, s // tq),
        in_specs=[vec(QK_DIM), vec(QK_DIM), vec(QK_DIM), vec(QK_DIM), vec(V_DIM),
                  pl.BlockSpec((1, tq, V_DIM), lambda b, hh, i: (b, i, hh)),
                  pl.BlockSpec((1, s, V_DIM), lambda b, hh, i: (b, 0, h + hh)),
                  pl.BlockSpec((1, s, V_DIM), lambda b, hh, i: (b, 0, 2 * h + hh))],
        out_specs=pl.BlockSpec((1, tq, V_DIM), lambda b, hh, i: (b, i, hh)),
        out_shape=jax.ShapeDtypeStruct((bsz, s, h * V_DIM), BF16),
        scratch_shapes=[pltpu.VMEM((n_acc, s), BF16),
                        pltpu.VMEM((1, 2 * tq), F32), pltpu.VMEM((n_acc, 2 * tq), F32)],
        compiler_params=_cparams(("arbitrary", "arbitrary", "arbitrary"), VMEM_LIMIT),
        name="attention",
    )(lq1.reshape(1, -1), lk1.reshape(1, -1), lq2.reshape(1, -1), lk2.reshape(1, -1),
      subln_g.reshape(1, -1), qkv, qkv, qkv)


def _ssm_tables(a_re, a_im, log_dt, b_re, b_im, c_re, c_im, t):
    g, p = a_re.shape
    w = b_re.shape[-1]
    dt = jnp.exp(log_dt.astype(F32))[:, None]
    lam = lax.complex(jnp.minimum(a_re.astype(F32), -1e-4), a_im.astype(F32))
    lam_dt = lam * dt
    lam_bar = jnp.exp(lam_dt)
    b_bar = ((lam_bar - 1.0) / lam)[..., None] * lax.complex(b_re.astype(F32), b_im.astype(F32))
    c_cplx = lax.complex(c_re.astype(F32), c_im.astype(F32))
    tau = jnp.arange(t + 1, dtype=F32)
    pw = jnp.exp(lam_dt[:, None, :] * tau[None, :, None])
    ktab = jnp.einsum('gcp,gtp,gpd->gdct', c_cplx, pw[:, :t], b_bar).real.reshape(g, w * w, t)
    wmat = b_bar.transpose(0, 2, 1)[:, :, None, :] * pw[:, t - 1 - jnp.arange(t)][:, None]
    wmat = wmat.reshape(g, w * t, p)
    vmat = c_cplx.transpose(0, 2, 1)[:, :, :, None] * pw[:, 1:t + 1].transpose(0, 2, 1)[:, :, None, :]
    vmat = vmat.reshape(g, p, w * t)
    lt = pw[:, t][:, None, :]
    return (ktab, wmat.real.astype(BF16), wmat.imag.astype(BF16),
            vmat.real.astype(BF16), (-vmat.imag).astype(BF16), lt.real, lt.imag)


def _ssm_kernel(u_ref, k_ref, wre_ref, wim_ref, vre_ref, vim_ref, ltre_ref, ltim_ref, y_ref,
                toep, ubuf, slre, slim, spre, spim, *, bsz, nc):
    gw = SSM_GROUP_WIDTH
    t = SSM_CHUNK
    r = bsz * nc

    jrow = lax.broadcasted_iota(jnp.int32, (t, t), 0)
    tcol = lax.broadcasted_iota(jnp.int32, (t, t), 1)
    causal = tcol >= jrow

    def build(d, carry):
        rows = pl.ds(pl.multiple_of(d * t, t), t)
        for c in range(gw):
            krow = k_ref[0, pl.ds(d * gw + c, 1), :]
            blk = pltpu.roll(jnp.broadcast_to(krow, (t, t)), 0, 1, stride=1, stride_axis=0)
            toep[rows, c * t:(c + 1) * t] = jnp.where(causal, blk, 0.0).astype(BF16)
        return carry

    lax.fori_loop(0, gw, build, 0)

    for d in range(gw):
        ubuf[:, d * t:(d + 1) * t] = u_ref[:, d].reshape(r, t).astype(BF16)
    u = ubuf[...]
    slre[...] = jnp.dot(u, wre_ref[0], preferred_element_type=F32)
    slim[...] = jnp.dot(u, wim_ref[0], preferred_element_type=F32)
    a = ltre_ref[0]
    b = ltim_ref[0]

    def step(c, carry):
        sre, sim = carry
        rows = pl.ds(c, bsz, stride=nc)
        spre[rows, :] = sre
        spim[rows, :] = sim
        nre = a * sre - b * sim + slre[rows, :]
        nim = a * sim + b * sre + slim[rows, :]
        return nre, nim

    z = jnp.zeros((bsz, a.shape[-1]), F32)
    lax.fori_loop(0, nc, step, (z, z))
    y = jnp.dot(u, toep[...], preferred_element_type=F32)
    y = y + jnp.dot(spre[...].astype(BF16), vre_ref[0], preferred_element_type=F32)
    y = y + jnp.dot(spim[...].astype(BF16), vim_ref[0], preferred_element_type=F32)
    for c in range(gw):
        y_ref[:, c] = y[:, c * t:(c + 1) * t].reshape(bsz, nc, t)


def _ssm(ut4, tables):
    ktab, wre, wim, vre, vim, ltre, ltim = tables
    bsz, d_ssm, nc, t = ut4.shape
    gw = SSM_GROUP_WIDTH
    g = d_ssm // gw
    p = wre.shape[-1]
    r = bsz * nc
    blk = lambda a, b: pl.BlockSpec((1, a, b), lambda i: (i, 0, 0))
    grp = pl.BlockSpec((bsz, gw, nc, t), lambda i: (0, i, 0, 0))
    return pl.pallas_call(
        functools.partial(_ssm_kernel, bsz=bsz, nc=nc),
        grid=(g,),
        in_specs=[grp, blk(gw * gw, t), blk(gw * t, p), blk(gw * t, p), blk(p, gw * t),
                  blk(p, gw * t), blk(1, p), blk(1, p)],
        out_specs=grp,
        out_shape=jax.ShapeDtypeStruct(ut4.shape, F32),
        scratch_shapes=[pltpu.VMEM((gw * t, gw * t), BF16), pltpu.VMEM((r, gw * t), BF16)]
        + [pltpu.VMEM((r, p), F32)] * 4,
        compiler_params=_cparams(("arbitrary",), VMEM_LIMIT),
        name="ssm",
    )(ut4, ktab, wre, wim, vre, vim, ltre, ltim)


def _mix_kernel(x_ref, attn_ref, yt_ref, ut_ref, mod_ref, d_ref, wglut_ref, bglu_ref, wout_ref,
                g2_ref, wrt_ref, br_ref, x1_ref, h2_ref, lgt_ref, *, d_attn):
    nck = yt_ref.shape[2]
    yt = jnp.concatenate([yt_ref[0, :, c, :] for c in range(nck)], axis=1)
    ut = jnp.concatenate([ut_ref[0, :, c, :] for c in range(nck)], axis=1)
    y = yt + d_ref[...] * ut
    z = 0.5 * y * (1.0 + jnp.tanh(math.sqrt(2.0 / math.pi) * (y + 0.044715 * (y * y * y))))
    gl = jnp.dot(wglut_ref[...], z.astype(BF16), preferred_element_type=F32) + bglu_ref[...]
    so = (z * _sigmoid(gl)).T
    mix = (jnp.dot(attn_ref[0], wout_ref[:d_attn, :], preferred_element_type=F32)
           + jnp.dot(so.astype(BF16), wout_ref[d_attn:, :], preferred_element_type=F32))
    x1 = x_ref[0] + mod_ref[0, 2:3, :] * mix
    x1_ref[0] = x1
    h2 = _rms(x1, g2_ref[...]) * (1.0 + mod_ref[0, 4:5, :]) + mod_ref[0, 3:4, :]
    h2_ref[0] = h2
    lgt_ref[...] = lax.dot_general(wrt_ref[...], h2, NT_DIMS, preferred_element_type=F32,
                                   precision=HIGHEST) + br_ref[...]


def _mix(x, attn, yt4, ut4, mod3, d_skip, wglu_t, bglu, wout, g2, wr, br, tm=1024):
    bsz, s, d = x.shape
    tm = min(tm, s)
    da = attn.shape[-1]
    ds_ = yt4.shape[1]
    e = wr.shape[-1]
    nt = s // tm
    tok = lambda n: pl.BlockSpec((1, tm, n), lambda b, i: (b, i, 0))
    chan = pl.BlockSpec((1, ds_, tm // SSM_CHUNK, SSM_CHUNK), lambda b, i: (b, 0, i, 0))
    full = lambda a, b_: pl.BlockSpec((a, b_), lambda b, i: (0, 0))
    return pl.pallas_call(
        functools.partial(_mix_kernel, d_attn=da),
        grid=(bsz, nt),
        in_specs=[tok(d), tok(da), chan, chan,
                  pl.BlockSpec((1, 6, d), lambda b, i: (b, 0, 0)),
                  full(ds_, 1), full(ds_, ds_), full(ds_, 1), full(da + ds_, d),
                  full(1, d), full(e, d), full(e, 1)],
        out_specs=[tok(d), tok(d), pl.BlockSpec((e, tm), lambda b, i: (0, b * nt + i))],
        out_shape=[jax.ShapeDtypeStruct((bsz, s, d), F32),
                   jax.ShapeDtypeStruct((bsz, s, d), F32),
                   jax.ShapeDtypeStruct((e, bsz * s), F32)],
        compiler_params=_cparams(("arbitrary", "arbitrary"), VMEM_LIMIT),
        name="mix",
    )(x, attn, yt4, ut4, mod3, d_skip.reshape(-1, 1), wglu_t, bglu.reshape(-1, 1), wout,
      g2.reshape(1, -1), wr.T, br.reshape(-1, 1))


def _route_kernel(lg_ref, idx_ref, gate_ref, rank_ref, cnt_ref, run_ref, *, tm):
    i = pl.program_id(0)

    @pl.when(i == 0)
    def _():
        run_ref[...] = jnp.zeros_like(run_ref)

    l = lg_ref[...]
    e = l.shape[0]
    sub = lax.broadcasted_iota(jnp.int32, l.shape, 0)
    vals, sels, idxs = [], [], []
    for _k in range(TOP_K):
        mx = jnp.max(l, axis=0, keepdims=True)
        ix = jnp.min(jnp.where(l == mx, sub, e), axis=0, keepdims=True)
        sel = sub == ix
        vals.append(mx)
        idxs.append(ix)
        sels.append(sel)
        l = jnp.where(sel, -jnp.inf, l)
    ex = [jnp.exp(v - vals[0]) for v in vals]
    den = ex[0] + ex[1] + ex[2] + ex[3]
    chosen = jnp.zeros(l.shape, F32)
    for sel in sels:
        chosen = chosen + jnp.where(sel, 1.0, 0.0)
    r_i = lax.broadcasted_iota(jnp.int32, (tm, tm), 0)
    c_i = lax.broadcasted_iota(jnp.int32, (tm, tm), 1)
    tri = jnp.where(r_i < c_i, 1.0, 0.0).astype(BF16)
    before = jnp.dot(chosen.astype(BF16), tri, preferred_element_type=F32) + run_ref[...]
    ksub = lax.broadcasted_iota(jnp.int32, (TOP_K, tm), 0)
    idx_o = jnp.zeros((TOP_K, tm), jnp.int32)
    gate_o = jnp.zeros((TOP_K, tm), F32)
    rank_o = jnp.zeros((TOP_K, tm), F32)
    for k in range(TOP_K):
        rk = jnp.sum(jnp.where(sels[k], before, 0.0), axis=0, keepdims=True)
        idx_o = jnp.where(ksub == k, idxs[k], idx_o)
        gate_o = jnp.where(ksub == k, ex[k] / den, gate_o)
        rank_o = jnp.where(ksub == k, rk, rank_o)
    idx_ref[...] = idx_o
    gate_ref[...] = gate_o
    rank_ref[...] = rank_o.astype(jnp.int32)
    run_ref[...] = run_ref[...] + jnp.sum(chosen, axis=1, keepdims=True)
    cnt_ref[...] = run_ref[...].astype(jnp.int32)


def _route(logits_t, tm=512):
    e, n = logits_t.shape
    tok = lambda w: pl.BlockSpec((w, tm), lambda i: (0, i))
    return pl.pallas_call(
        functools.partial(_route_kernel, tm=tm),
        grid=(n // tm,),
        in_specs=[tok(e)],
        out_specs=[tok(TOP_K), tok(TOP_K), tok(TOP_K), pl.BlockSpec((e, 1), lambda i: (0, 0))],
        out_shape=[jax.ShapeDtypeStruct((TOP_K, n), jnp.int32),
                   jax.ShapeDtypeStruct((TOP_K, n), F32),
                   jax.ShapeDtypeStruct((TOP_K, n), jnp.int32),
                   jax.ShapeDtypeStruct((e, 1), jnp.int32)],
        scratch_shapes=[pltpu.VMEM((e, 1), F32)],
        compiler_params=_cparams(("arbitrary",)),
        name="route",
    )(logits_t)


def _dispatch_kernel(pend_ref, padded_ref, nb_ref, pos_ref, h_ref, xs_ref, zero_ref, sem, zsem,
                     *, tm, rb, n_exp, nb_max):
    i = pl.program_id(0)

    def zero_block(blk_start):
        cp = pltpu.make_async_copy(zero_ref, xs_ref.at[pl.ds(pl.multiple_of(blk_start, rb), rb), :],
                                   zsem)
        cp.start()
        cp.wait()

    @pl.when(i == 0)
    def _():
        zero_ref[...] = jnp.zeros_like(zero_ref)

    @pl.when(i < n_exp)
    def _():
        e = jnp.minimum(i, n_exp - 1)

        @pl.when(padded_ref[e] > 0)
        def _():
            zero_block(pend_ref[e] - rb)

        @pl.when(nb_ref[0] + e < nb_max)
        def _():
            zero_block((nb_ref[0] + e) * rb)

    @pl.when(i >= n_exp)
    def _():
        def issue(t, _):
            for k in range(TOP_K):
                p = pos_ref[0, 0, k * tm + t]
                pltpu.make_async_copy(h_ref.at[pl.ds(t, 1), :], xs_ref.at[pl.ds(p, 1), :],
                                      sem).start(priority=k % 2)
            return 0

        lax.fori_loop(0, tm, issue, 0, unroll=4)
        for k in range(TOP_K):
            pltpu.make_async_copy(h_ref, xs_ref.at[pl.ds(0, tm), :], sem).wait()


def _dispatch(pend, padded, nblk, pos3, h2, n_rows, tm, rb):
    n, d = h2.shape
    n_exp = pend.shape[0]
    tile = lambda i, *_: (jnp.maximum(i - n_exp, 0), 0)
    return pl.pallas_call(
        functools.partial(_dispatch_kernel, tm=tm, rb=rb, n_exp=n_exp, nb_max=n_rows // rb),
        grid_spec=pltpu.PrefetchScalarGridSpec(
            num_scalar_prefetch=3,
            grid=(n_exp + n // tm,),
            in_specs=[pl.BlockSpec((1, 1, tm * TOP_K), lambda i, *_: tile(i) + (0,),
                                   memory_space=pltpu.SMEM),
                      pl.BlockSpec((tm, d), tile)],
            out_specs=pl.BlockSpec(memory_space=pl.ANY),
            scratch_shapes=[pltpu.VMEM((rb, d), F32), pltpu.SemaphoreType.DMA,
                            pltpu.SemaphoreType.DMA]),
        out_shape=jax.ShapeDtypeStruct((n_rows, d), F32),
        compiler_params=_cparams(("arbitrary",)),
        name="dispatch",
    )(pend, padded, nblk, pos3, h2)


def _experts_kernel(be_ref, xb_ref, nb_ref, x_ref, w1_ref, b1_ref, w2_ref, b2_ref, y_ref,
                    w1b, w2b, *, f):
    i = pl.program_id(0)
    prev = be_ref[jnp.maximum(i - 1, 0)]
    live = i < nb_ref[0]

    @pl.when(live & ((i == 0) | (be_ref[i] != prev)))
    def _():
        w1b[...] = w1_ref[0].astype(BF16)
        w2b[...] = w2_ref[0].astype(BF16)

    @pl.when(live)
    def _():
        x = x_ref[...].astype(BF16)
        gu = jnp.dot(x, w1b[...], preferred_element_type=F32) + b1_ref[0]
        gate = jnp.minimum(gu[:, :f], SWIGLU_LIMIT)
        up = jnp.clip(gu[:, f:], -SWIGLU_LIMIT, SWIGLU_LIMIT)
        glu = gate * _sigmoid(SWIGLU_ALPHA * gate)
        hmid = ((up + 1.0) * glu).astype(BF16)
        y_ref[...] = jnp.dot(hmid, w2b[...], preferred_element_type=F32) + b2_ref[0]

    @pl.when(jnp.logical_not(live))
    def _():
        y_ref[...] = jnp.zeros_like(y_ref)


def _experts(block_e, xblk, nblk, xs, w1, b1, w2, b2, rb):
    n_rows, d = xs.shape
    e, _, f2 = w1.shape
    f = f2 // 2
    nb_max = n_rows // rb
    return pl.pallas_call(
        functools.partial(_experts_kernel, f=f),
        grid_spec=pltpu.PrefetchScalarGridSpec(
            num_scalar_prefetch=3,
            grid=(nb_max,),
            in_specs=[pl.BlockSpec((rb, d), lambda i, be, xb, nb: (xb[i], 0)),
                      pl.BlockSpec((1, d, f2), lambda i, be, xb, nb: (be[i], 0, 0)),
                      pl.BlockSpec((1, 1, f2), lambda i, be, xb, nb: (be[i], 0, 0)),
                      pl.BlockSpec((1, f, d), lambda i, be, xb, nb: (be[i], 0, 0)),
                      pl.BlockSpec((1, 1, d), lambda i, be, xb, nb: (be[i], 0, 0))],
            out_specs=pl.BlockSpec((rb, d), lambda i, be, xb, nb: (i, 0)),
            scratch_shapes=[pltpu.VMEM((d, f2), BF16), pltpu.VMEM((f, d), BF16)]),
        out_shape=jax.ShapeDtypeStruct((n_rows, d), F32),
        compiler_params=_cparams(("arbitrary",), VMEM_LIMIT),
        name="experts",
    )(block_e, xblk, nblk, xs, w1, b1.reshape(e, 1, f2), w2, b2.reshape(e, 1, d))


def _combine_kernel(pos_ref, nxt_ref, gate_ref, x1_ref, mod_ref, fg_ref, ys_ref, o_ref, buf, sems,
                    *, tm):
    i = pl.program_id(0)
    n_tiles = pl.num_programs(0)
    slot = lax.rem(i, 2)

    def gather(p_ref, dst_slot):
        def issue(t, _):
            for k in range(TOP_K):
                p = p_ref[0, 0, k * tm + t]
                pltpu.make_async_copy(ys_ref.at[pl.ds(p, 1), :],
                                      buf.at[dst_slot, k, pl.ds(t, 1), :],
                                      sems.at[dst_slot]).start(priority=k % 2)
            return 0

        lax.fori_loop(0, tm, issue, 0, unroll=4)

    @pl.when(i == 0)
    def _():
        gather(pos_ref, 0)

    @pl.when(i + 1 < n_tiles)
    def _():
        gather(nxt_ref, 1 - slot)

    for k in range(TOP_K):
        pltpu.make_async_copy(ys_ref.at[pl.ds(0, tm), :], buf.at[slot, k], sems.at[slot]).wait()
    gate = gate_ref[...]
    moe = gate[:, 0:1] * buf[slot, 0]
    for k in range(1, TOP_K):
        moe = moe + gate[:, k:k + 1] * buf[slot, k]
    x2 = x1_ref[...] + mod_ref[0, 5:6, :] * moe
    o_ref[...] = _rms(x2, fg_ref[...])


def _combine(pos3, gates, x1, mod3, final_g, ys, tm):
    bsz, s, d = x1.shape
    n = bsz * s
    nt = s // tm
    n_tiles = n // tm
    pos_spec = lambda f: pl.BlockSpec((1, 1, tm * TOP_K), lambda i: (f(i), 0, 0),
                                      memory_space=pltpu.SMEM)
    out = pl.pallas_call(
        functools.partial(_combine_kernel, tm=tm),
        grid=(n_tiles,),
        in_specs=[pos_spec(lambda i: i),
                  pos_spec(lambda i: jnp.minimum(i + 1, n_tiles - 1)),
                  pl.BlockSpec((tm, TOP_K), lambda i: (i, 0)),
                  pl.BlockSpec((tm, d), lambda i: (i, 0)),
                  pl.BlockSpec((1, 6, d), lambda i: (i // nt, 0, 0)),
                  pl.BlockSpec((1, d), lambda i: (0, 0)),
                  pl.BlockSpec(memory_space=pl.ANY)],
        out_specs=pl.BlockSpec((tm, d), lambda i: (i, 0)),
        out_shape=jax.ShapeDtypeStruct((n, d), F32),
        scratch_shapes=[pltpu.VMEM((2, TOP_K, tm, d), F32), pltpu.SemaphoreType.DMA((2,))],
        compiler_params=_cparams(("arbitrary",), VMEM_LIMIT),
        name="combine",
    )(pos3, pos3, gates, x1.reshape(n, d), mod3, final_g.reshape(1, d), ys)
    return out.reshape(bsz, s, d)


def _layer(x, mod3, norm1_g, w_in, lq1, lk1, lq2, lk2, subln_g, ssm_a_re, ssm_a_im, ssm_log_dt,
           ssm_b_re, ssm_b_im, ssm_c_re, ssm_c_im, ssm_d, w_glu, b_glu, w_out, norm2_g,
           w_router, b_router, w1, b1, w2, b2, final_g):
    bsz, s, d = x.shape
    n = bsz * s
    d_attn = ATTN_HEADS * V_DIM
    n_qkv = 3 * d_attn

    qkv, ut4 = _inproj(x, mod3, norm1_g, w_in[:, :n_qkv].astype(BF16),
                       w_in[:, n_qkv:].T.astype(BF16))
    attn = _attention(qkv, lq1, lk1, lq2, lk2, subln_g)
    tables = _ssm_tables(ssm_a_re, ssm_a_im, ssm_log_dt, ssm_b_re, ssm_b_im, ssm_c_re, ssm_c_im,
                         SSM_CHUNK)
    yt4 = _ssm(ut4, tables)
    x1, h2, logits_t = _mix(x, attn, yt4, ut4, mod3, ssm_d, w_glu.T.astype(BF16), b_glu,
                            w_out.astype(BF16), norm2_g, w_router, b_router)

    idx, gates, rank, counts = _route(logits_t)
    rb = 512 if n * TOP_K >= 512 * N_EXPERTS else 128
    counts = counts.reshape(N_EXPERTS)
    padded = ((counts + rb - 1) // rb) * rb
    pend = jnp.cumsum(padded).astype(jnp.int32)
    pstart = pend - padded
    eids = jnp.arange(N_EXPERTS, dtype=jnp.int32)[:, None, None]
    pos = rank + jnp.sum(jnp.where(idx[None] == eids, pstart[:, None, None], 0), axis=0)
    pos = pos.astype(jnp.int32)
    nb_max = (n * TOP_K) // rb + N_EXPERTS
    n_rows = nb_max * rb
    nblk = pend[-1] // rb
    blk_ids = jnp.minimum(jnp.arange(nb_max, dtype=jnp.int32), nblk - 1)
    block_e = jnp.minimum(jnp.sum((pend[None, :] <= (blk_ids * rb)[:, None]).astype(jnp.int32), axis=1),
                          N_EXPERTS - 1).astype(jnp.int32)
    tmd = 256
    pos3 = (pos.reshape(TOP_K, n // tmd, tmd).transpose(1, 0, 2)
            .reshape(n // tmd, 1, TOP_K * tmd))
    gates_tok = gates.T

    nblk = nblk.reshape(1).astype(jnp.int32)
    xs = _dispatch(pend, padded.astype(jnp.int32), nblk, pos3, h2.reshape(n, d), n_rows, tmd, rb)
    ys = _experts(block_e, blk_ids, nblk, xs, w1, b1, w2, b2, rb)
    return _combine(pos3, gates_tok, x1, mod3, final_g, ys, tmd)


def kernel(x, c, w_ada, b_ada, norm1_g, w_in, lq1, lk1, lq2, lk2, subln_g, ssm_a_re, ssm_a_im,
           ssm_log_dt, ssm_b_re, ssm_b_im, ssm_c_re, ssm_c_im, ssm_d, w_glu, b_glu, w_out, norm2_g,
           w_router, b_router, w1, b1, w2, b2, final_g):
    assert w_ada.shape[0] == 1, "single-layer block"
    bsz, s, d = x.shape
    mod3 = _adaln(c, w_ada[0], b_ada[0]).reshape(bsz, 6, d)
    return _layer(x, mod3, norm1_g[0], w_in[0], lq1[0], lk1[0], lq2[0], lk2[0], subln_g[0],
                  ssm_a_re[0], ssm_a_im[0], ssm_log_dt[0], ssm_b_re[0], ssm_b_im[0], ssm_c_re[0],
                  ssm_c_im[0], ssm_d[0], w_glu[0], b_glu[0], w_out[0], norm2_g[0], w_router[0],
                  b_router[0], w1[0], b1[0], w2[0], b2[0], final_g)
```

```python
import functools
import math

import jax
import jax.numpy as jnp
from jax import lax
from jax.experimental import pallas as pl
from jax.experimental.pallas import tpu as pltpu

F32 = jnp.float32
BF16 = jnp.bfloat16
HIGHEST = lax.Precision.HIGHEST

RMS_EPS = 1e-6
MASK_VALUE = -1e30
ATTN_HEADS = 4
QK_DIM = 64
V_DIM = 128
SSM_GROUP_WIDTH = 16
SSM_STATE = 64
SSM_CHUNK = 128
N_EXPERTS = 32
TOP_K = 4
SWIGLU_LIMIT = 7.0
SWIGLU_ALPHA = 1.702
LAMBDA_INIT = 0.8 - 0.6 * math.exp(-0.3 * 0)

VMEM_LIMIT = 56 * 1024 * 1024
NT_DIMS = (((1,), (1,)), ((), ()))
N_QUERY_GROUPS = 1
LANES = 128
SUBLANES = 8


def _tile_rows(t):
    return pl.ds(pl.multiple_of(t * SUBLANES, SUBLANES), SUBLANES)


def _cparams(sem, vmem=None):
    return pltpu.CompilerParams(dimension_semantics=sem, vmem_limit_bytes=vmem)


def _sigmoid(x):
    return 1.0 / (1.0 + jnp.exp(-x))


def _rms(x, g):
    ms = jnp.mean(x * x, axis=-1, keepdims=True)
    return x * lax.rsqrt(ms + RMS_EPS) * g


def _adaln_kernel(c_ref, w_ref, b_ref, o_ref):
    c = c_ref[...]
    ca = c * _sigmoid(c)
    o_ref[...] = jnp.dot(ca, w_ref[...], preferred_element_type=F32, precision=HIGHEST) + b_ref[...]


def _adaln(c, w, b):
    bsz, d = c.shape
    n = w.shape[1]
    tn = 1536
    return pl.pallas_call(
        _adaln_kernel,
        grid=(n // tn,),
        in_specs=[pl.BlockSpec((bsz, d), lambda j: (0, 0)),
                  pl.BlockSpec((d, tn), lambda j: (0, j)),
                  pl.BlockSpec((1, tn), lambda j: (0, j))],
        out_specs=pl.BlockSpec((bsz, tn), lambda j: (0, j)),
        out_shape=jax.ShapeDtypeStruct((bsz, n), F32),
        compiler_params=_cparams(("arbitrary",)),
        name="adaln",
    )(c, w, b.reshape(1, n))


def _inproj_kernel(x_ref, mod_ref, g_ref, wq_ref, wut_ref, qkv_ref, ut_ref):
    x = x_ref[0]
    y = _rms(x, g_ref[...])
    h = (y * (1.0 + mod_ref[0, 1:2, :]) + mod_ref[0, 0:1, :]).astype(BF16)
    qkv_ref[0] = jnp.dot(h, wq_ref[...], preferred_element_type=F32).astype(BF16)
    ut = lax.dot_general(wut_ref[...], h, NT_DIMS, preferred_element_type=F32)
    for c in range(ut_ref.shape[2]):
        ut_ref[0, :, c, :] = ut[:, c * SSM_CHUNK:(c + 1) * SSM_CHUNK]


def _inproj(x, mod3, g, wq_bf16, wut_bf16, tm=1024):
    bsz, s, d = x.shape
    tm = min(tm, s)
    n_qkv = wq_bf16.shape[1]
    n_u = wut_bf16.shape[0]
    nck = tm // SSM_CHUNK
    return pl.pallas_call(
        _inproj_kernel,
        grid=(bsz, s // tm),
        in_specs=[pl.BlockSpec((1, tm, d), lambda b, i: (b, i, 0)),
                  pl.BlockSpec((1, 6, d), lambda b, i: (b, 0, 0)),
                  pl.BlockSpec((1, d), lambda b, i: (0, 0)),
                  pl.BlockSpec((d, n_qkv), lambda b, i: (0, 0)),
                  pl.BlockSpec((n_u, d), lambda b, i: (0, 0))],
        out_specs=[pl.BlockSpec((1, tm, n_qkv), lambda b, i: (b, i, 0)),
                   pl.BlockSpec((1, n_u, nck, SSM_CHUNK), lambda b, i: (b, 0, i, 0))],
        out_shape=[jax.ShapeDtypeStruct((bsz, s, n_qkv), BF16),
                   jax.ShapeDtypeStruct((bsz, n_u, s // SSM_CHUNK, SSM_CHUNK), F32)],
        compiler_params=_cparams(("arbitrary", "arbitrary"), VMEM_LIMIT),
        name="inproj",
    )(x, mod3, g.reshape(1, d), wq_bf16, wut_bf16)


def _attn_kernel(lq1_ref, lk1_ref, lq2_ref, lk2_ref, sg_ref, q_ref, k_ref, v_ref, o_ref,
                 vt_ref, m_ref, acc_ref, *, tq):
    qi = pl.program_id(2)
    n_acc = acc_ref.shape[0]

    @pl.when(qi == 0)
    def _():
        vt_ref[:V_DIM, :] = v_ref[0].astype(F32).T.astype(BF16)
        vt_ref[V_DIM:, :] = jnp.ones((n_acc - V_DIM, vt_ref.shape[1]), BF16)

    lam = (jnp.exp(jnp.sum(lq1_ref[...] * lk1_ref[...], axis=-1, keepdims=True))
           - jnp.exp(jnp.sum(lq2_ref[...] * lk2_ref[...], axis=-1, keepdims=True))
           + LAMBDA_INIT)
    q = q_ref[0] * (QK_DIM ** -0.5)
    lane = lax.broadcasted_iota(jnp.int32, q.shape, 1)
    zero = jnp.zeros_like(q)
    qs = jnp.concatenate([jnp.where(lane < QK_DIM, q, zero), jnp.where(lane >= QK_DIM, q, zero)],
                         axis=0)
    m_ref[...] = jnp.full(m_ref.shape, MASK_VALUE, F32)
    acc_ref[...] = jnp.zeros(acc_ref.shape, F32)

    def block(ki, mask):
        start = pl.multiple_of(ki * tq, tq)
        kb = k_ref[0, pl.ds(start, tq), :]
        vt = vt_ref[:, pl.ds(start, tq)]
        w = 2 * tq // N_QUERY_GROUPS
        for c in range(N_QUERY_GROUPS):
            cols = slice(c * w, (c + 1) * w)
            st = lax.dot_general(kb, qs[cols], NT_DIMS, preferred_element_type=F32)
            if mask is not None:
                st = jnp.where(mask[:, cols], st, MASK_VALUE)
            m_prev = m_ref[:, cols]
            m_new = jnp.maximum(m_prev, jnp.max(st, axis=0, keepdims=True))
            pt = jnp.exp(st - m_new).astype(BF16)
            acc_ref[:, cols] = (jnp.exp(m_prev - m_new) * acc_ref[:, cols]
                                + jnp.dot(vt, pt, preferred_element_type=F32))
            m_ref[:, cols] = m_new

    def body(ki, carry):
        block(ki, None)
        return carry

    lax.fori_loop(0, qi, body, 0)
    key = lax.broadcasted_iota(jnp.int32, (tq, 2 * tq), 0)
    qry = lax.broadcasted_iota(jnp.int32, (tq, 2 * tq), 1)
    block(qi, key <= jnp.where(qry >= tq, qry - tq, qry))
    a = acc_ref[...]
    ot = (a[:V_DIM, :tq] / a[V_DIM:V_DIM + 1, :tq]
          - lam * (a[:V_DIM, tq:] / a[V_DIM:V_DIM + 1, tq:]))
    o = _rms(ot.T, sg_ref[...]) * (1.0 - LAMBDA_INIT)
    o_ref[0] = o.astype(BF16)


def _attention(qkv, lq1, lk1, lq2, lk2, subln_g, tq=1024):
    bsz, s, _ = qkv.shape
    tq = min(tq, s)
    h = ATTN_HEADS
    n_acc = V_DIM + 8
    vec = lambda n: pl.BlockSpec((1, n), lambda b, hh, i: (0, 0))
    return pl.pallas_call(
        functools.partial(_attn_kernel, tq=tq),
        grid=(bsz, h, s // tq),
        in_specs=[vec(QK_DIM), vec(QK_DIM), vec(QK_DIM), vec(QK_DIM), vec(V_DIM),
                  pl.BlockSpec((1, tq, V_DIM), lambda b, hh, i: (b, i, hh)),
                  pl.BlockSpec((1, s, V_DIM), lambda b, hh, i: (b, 0, h + hh)),
                  pl.BlockSpec((1, s, V_DIM), lambda b, hh, i: (b, 0, 2 * h + hh))],
        out_specs=pl.BlockSpec((1, tq, V_DIM), lambda b, hh, i: (b, i, hh)),
        out_shape=jax.ShapeDtypeStruct((bsz, s, h * V_DIM), BF16),
        scratch_shapes=[pltpu.VMEM((n_acc, s), BF16),
                        pltpu.VMEM((1, 2 * tq), F32), pltpu.VMEM((n_acc, 2 * tq), F32)],
        compiler_params=_cparams(("arbitrary", "arbitrary", "arbitrary"), VMEM_LIMIT),
        name="attention",
    )(lq1.reshape(1, -1), lk1.reshape(1, -1), lq2.reshape(1, -1), lk2.reshape(1, -1),
      subln_g.reshape(1, -1), qkv, qkv, qkv)


def _ssm_tables(a_re, a_im, log_dt, b_re, b_im, c_re, c_im, t):
    g, p = a_re.shape
    w = b_re.shape[-1]
    dt = jnp.exp(log_dt.astype(F32))[:, None]
    lam = lax.complex(jnp.minimum(a_re.astype(F32), -1e-4), a_im.astype(F32))
    lam_dt = lam * dt
    lam_bar = jnp.exp(lam_dt)
    b_bar = ((lam_bar - 1.0) / lam)[..., None] * lax.complex(b_re.astype(F32), b_im.astype(F32))
    c_cplx = lax.complex(c_re.astype(F32), c_im.astype(F32))
    tau = jnp.arange(t + 1, dtype=F32)
    pw = jnp.exp(lam_dt[:, None, :] * tau[None, :, None])
    ktab = jnp.einsum('gcp,gtp,gpd->gdct', c_cplx, pw[:, :t], b_bar).real.reshape(g, w * w, t)
    wmat = b_bar.transpose(0, 2, 1)[:, :, None, :] * pw[:, t - 1 - jnp.arange(t)][:, None]
    wmat = wmat.reshape(g, w * t, p)
    vmat = c_cplx.transpose(0, 2, 1)[:, :, :, None] * pw[:, 1:t + 1].transpose(0, 2, 1)[:, :, None, :]
    vmat = vmat.reshape(g, p, w * t)
    lt = pw[:, t][:, None, :]
    return (ktab, wmat.real.astype(BF16), wmat.imag.astype(BF16),
            vmat.real.astype(BF16), (-vmat.imag).astype(BF16), lt.real, lt.imag)


def _ssm_kernel(u_ref, k_ref, wre_ref, wim_ref, vre_ref, vim_ref, ltre_ref, ltim_ref, y_ref,
                toep, ubuf, slre, slim, spre, spim, *, bsz, nc):
    gw = SSM_GROUP_WIDTH
    t = SSM_CHUNK
    r = bsz * nc

    jrow = lax.broadcasted_iota(jnp.int32, (t, t), 0)
    tcol = lax.broadcasted_iota(jnp.int32, (t, t), 1)
    causal = tcol >= jrow

    def build(d, carry):
        rows = pl.ds(pl.multiple_of(d * t, t), t)
        for c in range(gw):
            krow = k_ref[0, pl.ds(d * gw + c, 1), :]
            blk = pltpu.roll(jnp.broadcast_to(krow, (t, t)), 0, 1, stride=1, stride_axis=0)
            toep[rows, c * t:(c + 1) * t] = jnp.where(causal, blk, 0.0).astype(BF16)
        return carry

    lax.fori_loop(0, gw, build, 0)

    for d in range(gw):
        ubuf[:, d * t:(d + 1) * t] = u_ref[:, d].reshape(r, t).astype(BF16)
    u = ubuf[...]
    slre[...] = jnp.dot(u, wre_ref[0], preferred_element_type=F32)
    slim[...] = jnp.dot(u, wim_ref[0], preferred_element_type=F32)
    a = ltre_ref[0]
    b = ltim_ref[0]

    def step(c, carry):
        sre, sim = carry
        rows = pl.ds(c, bsz, stride=nc)
        spre[rows, :] = sre
        spim[rows, :] = sim
        nre = a * sre - b * sim + slre[rows, :]
        nim = a * sim + b * sre + slim[rows, :]
        return nre, nim

    z = jnp.zeros((bsz, a.shape[-1]), F32)
    lax.fori_loop(0, nc, step, (z, z))
    y = jnp.dot(u, toep[...], preferred_element_type=F32)
    y = y + jnp.dot(spre[...].astype(BF16), vre_ref[0], preferred_element_type=F32)
    y = y + jnp.dot(spim[...].astype(BF16), vim_ref[0], preferred_element_type=F32)
    for c in range(gw):
        y_ref[:, c] = y[:, c * t:(c + 1) * t].reshape(bsz, nc, t)


def _ssm(ut4, tables):
    ktab, wre, wim, vre, vim, ltre, ltim = tables
    bsz, d_ssm, nc, t = ut4.shape
    gw = SSM_GROUP_WIDTH
    g = d_ssm // gw
    p = wre.shape[-1]
    r = bsz * nc
    blk = lambda a, b: pl.BlockSpec((1, a, b), lambda i: (i, 0, 0))
    grp = pl.BlockSpec((bsz, gw, nc, t), lambda i: (0, i, 0, 0))
    return pl.pallas_call(
        functools.partial(_ssm_kernel, bsz=bsz, nc=nc),
        grid=(g,),
        in_specs=[grp, blk(gw * gw, t), blk(gw * t, p), blk(gw * t, p), blk(p, gw * t),
                  blk(p, gw * t), blk(1, p), blk(1, p)],
        out_specs=grp,
        out_shape=jax.ShapeDtypeStruct(ut4.shape, F32),
        scratch_shapes=[pltpu.VMEM((gw * t, gw * t), BF16), pltpu.VMEM((r, gw * t), BF16)]
        + [pltpu.VMEM((r, p), F32)] * 4,
        compiler_params=_cparams(("arbitrary",), VMEM_LIMIT),
        name="ssm",
    )(ut4, ktab, wre, wim, vre, vim, ltre, ltim)


def _mix_kernel(x_ref, attn_ref, yt_ref, ut_ref, mod_ref, d_ref, wglut_ref, bglu_ref, wout_ref,
                g2_ref, wrt_ref, br_ref, x1_ref, h2_ref, lgt_ref, *, d_attn):
    nck = yt_ref.shape[2]
    yt = jnp.concatenate([yt_ref[0, :, c, :] for c in range(nck)], axis=1)
    ut = jnp.concatenate([ut_ref[0, :, c, :] for c in range(nck)], axis=1)
    y = yt + d_ref[...] * ut
    z = 0.5 * y * (1.0 + jnp.tanh(math.sqrt(2.0 / math.pi) * (y + 0.044715 * (y * y * y))))
    gl = jnp.dot(wglut_ref[...], z.astype(BF16), preferred_element_type=F32) + bglu_ref[...]
    so = (z * _sigmoid(gl)).T
    mix = (jnp.dot(attn_ref[0], wout_ref[:d_attn, :], preferred_element_type=F32)
           + jnp.dot(so.astype(BF16), wout_ref[d_attn:, :], preferred_element_type=F32))
    x1 = x_ref[0] + mod_ref[0, 2:3, :] * mix
    x1_ref[0] = x1
    h2 = _rms(x1, g2_ref[...]) * (1.0 + mod_ref[0, 4:5, :]) + mod_ref[0, 3:4, :]
    for j in range(SUBLANES):
        h2_ref[pl.ds(j, h2.shape[0], stride=SUBLANES), :] = h2[:, j * LANES:(j + 1) * LANES]
    lgt_ref[...] = lax.dot_general(wrt_ref[...], h2, NT_DIMS, preferred_element_type=F32,
                                   precision=HIGHEST) + br_ref[...]


def _mix(x, attn, yt4, ut4, mod3, d_skip, wglu_t, bglu, wout, g2, wr, br, tm=1024):
    bsz, s, d = x.shape
    tm = min(tm, s)
    da = attn.shape[-1]
    ds_ = yt4.shape[1]
    e = wr.shape[-1]
    nt = s // tm
    tok = lambda n: pl.BlockSpec((1, tm, n), lambda b, i: (b, i, 0))
    chan = pl.BlockSpec((1, ds_, tm // SSM_CHUNK, SSM_CHUNK), lambda b, i: (b, 0, i, 0))
    full = lambda a, b_: pl.BlockSpec((a, b_), lambda b, i: (0, 0))
    return pl.pallas_call(
        functools.partial(_mix_kernel, d_attn=da),
        grid=(bsz, nt),
        in_specs=[tok(d), tok(da), chan, chan,
                  pl.BlockSpec((1, 6, d), lambda b, i: (b, 0, 0)),
                  full(ds_, 1), full(ds_, ds_), full(ds_, 1), full(da + ds_, d),
                  full(1, d), full(e, d), full(e, 1)],
        out_specs=[tok(d), pl.BlockSpec((tm * SUBLANES, LANES), lambda b, i: (b * nt + i, 0)),
                   pl.BlockSpec((e, tm), lambda b, i: (0, b * nt + i))],
        out_shape=[jax.ShapeDtypeStruct((bsz, s, d), F32),
                   jax.ShapeDtypeStruct((bsz * s * SUBLANES, LANES), F32),
                   jax.ShapeDtypeStruct((e, bsz * s), F32)],
        compiler_params=_cparams(("arbitrary", "arbitrary"), VMEM_LIMIT),
        name="mix",
    )(x, attn, yt4, ut4, mod3, d_skip.reshape(-1, 1), wglu_t, bglu.reshape(-1, 1), wout,
      g2.reshape(1, -1), wr.T, br.reshape(-1, 1))


def _route_kernel(lg_ref, idx_ref, gate_ref, rank_ref, cnt_ref, run_ref, *, tm):
    i = pl.program_id(0)

    @pl.when(i == 0)
    def _():
        run_ref[...] = jnp.zeros_like(run_ref)

    l = lg_ref[...]
    e = l.shape[0]
    sub = lax.broadcasted_iota(jnp.int32, l.shape, 0)
    vals, sels, idxs = [], [], []
    for _k in range(TOP_K):
        mx = jnp.max(l, axis=0, keepdims=True)
        ix = jnp.min(jnp.where(l == mx, sub, e), axis=0, keepdims=True)
        sel = sub == ix
        vals.append(mx)
        idxs.append(ix)
        sels.append(sel)
        l = jnp.where(sel, -jnp.inf, l)
    ex = [jnp.exp(v - vals[0]) for v in vals]
    den = ex[0] + ex[1] + ex[2] + ex[3]
    chosen = jnp.zeros(l.shape, F32)
    for sel in sels:
        chosen = chosen + jnp.where(sel, 1.0, 0.0)
    r_i = lax.broadcasted_iota(jnp.int32, (tm, tm), 0)
    c_i = lax.broadcasted_iota(jnp.int32, (tm, tm), 1)
    tri = jnp.where(r_i < c_i, 1.0, 0.0).astype(BF16)
    before = jnp.dot(chosen.astype(BF16), tri, preferred_element_type=F32) + run_ref[...]
    ksub = lax.broadcasted_iota(jnp.int32, (TOP_K, tm), 0)
    idx_o = jnp.zeros((TOP_K, tm), jnp.int32)
    gate_o = jnp.zeros((TOP_K, tm), F32)
    rank_o = jnp.zeros((TOP_K, tm), F32)
    for k in range(TOP_K):
        rk = jnp.sum(jnp.where(sels[k], before, 0.0), axis=0, keepdims=True)
        idx_o = jnp.where(ksub == k, idxs[k], idx_o)
        gate_o = jnp.where(ksub == k, ex[k] / den, gate_o)
        rank_o = jnp.where(ksub == k, rk, rank_o)
    idx_ref[...] = idx_o
    gate_ref[...] = gate_o
    rank_ref[...] = rank_o.astype(jnp.int32)
    run_ref[...] = run_ref[...] + jnp.sum(chosen, axis=1, keepdims=True)
    cnt_ref[...] = run_ref[...].astype(jnp.int32)


def _route(logits_t, tm=512):
    e, n = logits_t.shape
    tok = lambda w: pl.BlockSpec((w, tm), lambda i: (0, i))
    return pl.pallas_call(
        functools.partial(_route_kernel, tm=tm),
        grid=(n // tm,),
        in_specs=[tok(e)],
        out_specs=[tok(TOP_K), tok(TOP_K), tok(TOP_K), pl.BlockSpec((e, 1), lambda i: (0, 0))],
        out_shape=[jax.ShapeDtypeStruct((TOP_K, n), jnp.int32),
                   jax.ShapeDtypeStruct((TOP_K, n), F32),
                   jax.ShapeDtypeStruct((TOP_K, n), jnp.int32),
                   jax.ShapeDtypeStruct((e, 1), jnp.int32)],
        scratch_shapes=[pltpu.VMEM((e, 1), F32)],
        compiler_params=_cparams(("arbitrary",)),
        name="route",
    )(logits_t)


def _dispatch_kernel(pend_ref, padded_ref, nb_ref, pos_ref, h_ref, xs_ref, zero_ref, sem, zsem,
                     *, tm, rb, n_exp, nb_max):
    i = pl.program_id(0)

    def zero_block(blk_start):
        rows = pl.ds(pl.multiple_of(blk_start * SUBLANES, rb * SUBLANES), rb * SUBLANES)
        cp = pltpu.make_async_copy(zero_ref, xs_ref.at[rows, :], zsem)
        cp.start()
        cp.wait()

    @pl.when(i == 0)
    def _():
        zero_ref[...] = jnp.zeros_like(zero_ref)

    @pl.when(i < n_exp)
    def _():
        e = jnp.minimum(i, n_exp - 1)

        @pl.when(padded_ref[e] > 0)
        def _():
            zero_block(pend_ref[e] - rb)

        @pl.when(nb_ref[0] + e < nb_max)
        def _():
            zero_block((nb_ref[0] + e) * rb)

    @pl.when(i >= n_exp)
    def _():
        def issue(t, _):
            for k in range(TOP_K):
                p = pos_ref[0, 0, k * tm + t]
                pltpu.make_async_copy(h_ref.at[_tile_rows(t), :], xs_ref.at[_tile_rows(p), :],
                                      sem).start(priority=k % 2)
            return 0

        lax.fori_loop(0, tm, issue, 0, unroll=4)
        for k in range(TOP_K):
            pltpu.make_async_copy(h_ref, xs_ref.at[pl.ds(0, tm * SUBLANES), :], sem).wait()


def _dispatch(pend, padded, nblk, pos3, h2, n_rows, tm, rb):
    n = h2.shape[0] // SUBLANES
    n_exp = pend.shape[0]
    tile = lambda i, *_: (jnp.maximum(i - n_exp, 0), 0)
    return pl.pallas_call(
        functools.partial(_dispatch_kernel, tm=tm, rb=rb, n_exp=n_exp, nb_max=n_rows // rb),
        grid_spec=pltpu.PrefetchScalarGridSpec(
            num_scalar_prefetch=3,
            grid=(n_exp + n // tm,),
            in_specs=[pl.BlockSpec((1, 1, tm * TOP_K), lambda i, *_: tile(i) + (0,),
                                   memory_space=pltpu.SMEM),
                      pl.BlockSpec((tm * SUBLANES, LANES), tile)],
            out_specs=pl.BlockSpec(memory_space=pl.ANY),
            scratch_shapes=[pltpu.VMEM((rb * SUBLANES, LANES), F32), pltpu.SemaphoreType.DMA,
                            pltpu.SemaphoreType.DMA]),
        out_shape=jax.ShapeDtypeStruct((n_rows * SUBLANES, LANES), F32),
        compiler_params=_cparams(("arbitrary",)),
        name="dispatch",
    )(pend, padded, nblk, pos3, h2)


def _experts_kernel(be_ref, xb_ref, nb_ref, x_ref, w1_ref, b1_ref, w2_ref, b2_ref, y_ref,
                    w1b, w2b, *, f):
    i = pl.program_id(0)
    prev = be_ref[jnp.maximum(i - 1, 0)]
    live = i < nb_ref[0]

    @pl.when(live & ((i == 0) | (be_ref[i] != prev)))
    def _():
        w1b[...] = w1_ref[0].astype(BF16)
        w2b[...] = w2_ref[0].astype(BF16)

    @pl.when(live)
    def _():
        rb = x_ref.shape[0] // SUBLANES
        x = jnp.concatenate([x_ref[pl.ds(j, rb, stride=SUBLANES), :].astype(BF16)
                             for j in range(SUBLANES)], axis=1)
        gu = jnp.dot(x, w1b[...], preferred_element_type=F32) + b1_ref[0]
        gate = jnp.minimum(gu[:, :f], SWIGLU_LIMIT)
        up = jnp.clip(gu[:, f:], -SWIGLU_LIMIT, SWIGLU_LIMIT)
        glu = gate * _sigmoid(SWIGLU_ALPHA * gate)
        hmid = ((up + 1.0) * glu).astype(BF16)
        y = jnp.dot(hmid, w2b[...], preferred_element_type=F32) + b2_ref[0]
        for j in range(SUBLANES):
            y_ref[pl.ds(j, rb, stride=SUBLANES), :] = y[:, j * LANES:(j + 1) * LANES]

    @pl.when(jnp.logical_not(live))
    def _():
        y_ref[...] = jnp.zeros_like(y_ref)


def _experts(block_e, xblk, nblk, xs, w1, b1, w2, b2, rb):
    e, d, f2 = w1.shape
    assert d == SUBLANES * LANES, "token-tile layout holds one row per (8, 128) f32 tile"
    f = f2 // 2
    nb_max = xs.shape[0] // (rb * SUBLANES)
    rows = pl.BlockSpec((rb * SUBLANES, LANES), lambda i, be, xb, nb: (xb[i], 0))
    return pl.pallas_call(
        functools.partial(_experts_kernel, f=f),
        grid_spec=pltpu.PrefetchScalarGridSpec(
            num_scalar_prefetch=3,
            grid=(nb_max,),
            in_specs=[rows,
                      pl.BlockSpec((1, d, f2), lambda i, be, xb, nb: (be[i], 0, 0)),
                      pl.BlockSpec((1, 1, f2), lambda i, be, xb, nb: (be[i], 0, 0)),
                      pl.BlockSpec((1, f, d), lambda i, be, xb, nb: (be[i], 0, 0)),
                      pl.BlockSpec((1, 1, d), lambda i, be, xb, nb: (be[i], 0, 0))],
            out_specs=pl.BlockSpec((rb * SUBLANES, LANES), lambda i, be, xb, nb: (i, 0)),
            scratch_shapes=[pltpu.VMEM((d, f2), BF16), pltpu.VMEM((f, d), BF16)]),
        out_shape=jax.ShapeDtypeStruct(xs.shape, F32),
        compiler_params=_cparams(("arbitrary",), VMEM_LIMIT),
        name="experts",
    )(block_e, xblk, nblk, xs, w1, b1.reshape(e, 1, f2), w2, b2.reshape(e, 1, d))


def _combine_kernel(pos_ref, nxt_ref, gate_ref, x1_ref, mod_ref, fg_ref, ys_ref, o_ref, buf, sems,
                    *, tm):
    i = pl.program_id(0)
    n_tiles = pl.num_programs(0)
    slot = lax.rem(i, 2)

    def gather(p_ref, dst_slot):
        def issue(t, _):
            for k in range(TOP_K):
                p = p_ref[0, 0, k * tm + t]
                pltpu.make_async_copy(ys_ref.at[_tile_rows(p), :],
                                      buf.at[dst_slot, k, _tile_rows(t), :],
                                      sems.at[dst_slot]).start(priority=k % 2)
            return 0

        lax.fori_loop(0, tm, issue, 0, unroll=4)

    @pl.when(i == 0)
    def _():
        gather(pos_ref, 0)

    @pl.when(i + 1 < n_tiles)
    def _():
        gather(nxt_ref, 1 - slot)

    for k in range(TOP_K):
        pltpu.make_async_copy(ys_ref.at[pl.ds(0, tm * SUBLANES), :], buf.at[slot, k],
                              sems.at[slot]).wait()
    gate = gate_ref[...]
    pieces = []
    for j in range(SUBLANES):
        acc = gate[:, 0:1] * buf[slot, 0, pl.ds(j, tm, stride=SUBLANES), :]
        for k in range(1, TOP_K):
            acc = acc + gate[:, k:k + 1] * buf[slot, k, pl.ds(j, tm, stride=SUBLANES), :]
        pieces.append(acc)
    moe = jnp.concatenate(pieces, axis=1)
    x2 = x1_ref[...] + mod_ref[0, 5:6, :] * moe
    o_ref[...] = _rms(x2, fg_ref[...])


def _combine(pos3, gates, x1, mod3, final_g, ys, tm):
    bsz, s, d = x1.shape
    n = bsz * s
    nt = s // tm
    n_tiles = n // tm
    pos_spec = lambda f: pl.BlockSpec((1, 1, tm * TOP_K), lambda i: (f(i), 0, 0),
                                      memory_space=pltpu.SMEM)
    out = pl.pallas_call(
        functools.partial(_combine_kernel, tm=tm),
        grid=(n_tiles,),
        in_specs=[pos_spec(lambda i: i),
                  pos_spec(lambda i: jnp.minimum(i + 1, n_tiles - 1)),
                  pl.BlockSpec((tm, TOP_K), lambda i: (i, 0)),
                  pl.BlockSpec((tm, d), lambda i: (i, 0)),
                  pl.BlockSpec((1, 6, d), lambda i: (i // nt, 0, 0)),
                  pl.BlockSpec((1, d), lambda i: (0, 0)),
                  pl.BlockSpec(memory_space=pl.ANY)],
        out_specs=pl.BlockSpec((tm, d), lambda i: (i, 0)),
        out_shape=jax.ShapeDtypeStruct((n, d), F32),
        scratch_shapes=[pltpu.VMEM((2, TOP_K, tm * SUBLANES, LANES), F32),
                        pltpu.SemaphoreType.DMA((2,))],
        compiler_params=_cparams(("arbitrary",), VMEM_LIMIT),
        name="combine",
    )(pos3, pos3, gates, x1.reshape(n, d), mod3, final_g.reshape(1, d), ys)
    return out.reshape(bsz, s, d)


def _layer(x, mod3, norm1_g, w_in, lq1, lk1, lq2, lk2, subln_g, ssm_a_re, ssm_a_im, ssm_log_dt,
           ssm_b_re, ssm_b_im, ssm_c_re, ssm_c_im, ssm_d, w_glu, b_glu, w_out, norm2_g,
           w_router, b_router, w1, b1, w2, b2, final_g):
    bsz, s, d = x.shape
    n = bsz * s
    d_attn = ATTN_HEADS * V_DIM
    n_qkv = 3 * d_attn

    qkv, ut4 = _inproj(x, mod3, norm1_g, w_in[:, :n_qkv].astype(BF16),
                       w_in[:, n_qkv:].T.astype(BF16))
    attn = _attention(qkv, lq1, lk1, lq2, lk2, subln_g)
    tables = _ssm_tables(ssm_a_re, ssm_a_im, ssm_log_dt, ssm_b_re, ssm_b_im, ssm_c_re, ssm_c_im,
                         SSM_CHUNK)
    yt4 = _ssm(ut4, tables)
    x1, h2, logits_t = _mix(x, attn, yt4, ut4, mod3, ssm_d, w_glu.T.astype(BF16), b_glu,
                            w_out.astype(BF16), norm2_g, w_router, b_router)

    idx, gates, rank, counts = _route(logits_t)
    rb = 512 if n * TOP_K >= 512 * N_EXPERTS else 128
    counts = counts.reshape(N_EXPERTS)
    padded = ((counts + rb - 1) // rb) * rb
    pend = jnp.cumsum(padded).astype(jnp.int32)
    pstart = pend - padded
    eids = jnp.arange(N_EXPERTS, dtype=jnp.int32)[:, None, None]
    pos = rank + jnp.sum(jnp.where(idx[None] == eids, pstart[:, None, None], 0), axis=0)
    pos = pos.astype(jnp.int32)
    nb_max = (n * TOP_K) // rb + N_EXPERTS
    n_rows = nb_max * rb
    nblk = pend[-1] // rb
    blk_ids = jnp.minimum(jnp.arange(nb_max, dtype=jnp.int32), nblk - 1)
    block_e = jnp.minimum(jnp.sum((pend[None, :] <= (blk_ids * rb)[:, None]).astype(jnp.int32), axis=1),
                          N_EXPERTS - 1).astype(jnp.int32)
    tmd = 256
    pos3 = (pos.reshape(TOP_K, n // tmd, tmd).transpose(1, 0, 2)
            .reshape(n // tmd, 1, TOP_K * tmd))
    gates_tok = gates.T

    nblk = nblk.reshape(1).astype(jnp.int32)
    xs = _dispatch(pend, padded.astype(jnp.int32), nblk, pos3, h2, n_rows, tmd, rb)
    ys = _experts(block_e, blk_ids, nblk, xs, w1, b1, w2, b2, rb)
    return _combine(pos3, gates_tok, x1, mod3, final_g, ys, tmd)


def kernel(x, c, w_ada, b_ada, norm1_g, w_in, lq1, lk1, lq2, lk2, subln_g, ssm_a_re, ssm_a_im,
           ssm_log_dt, ssm_b_re, ssm_b_im, ssm_c_re, ssm_c_im, ssm_d, w_glu, b_glu, w_out, norm2_g,
           w_router, b_router, w1, b1, w2, b2, final_g):
    assert w_ada.shape[0] == 1, "single-layer block"
    bsz, s, d = x.shape
    mod3 = _adaln(c, w_ada[0], b_ada[0]).reshape(bsz, 6, d)
    return _layer(x, mod3, norm1_g[0], w_in[0], lq1[0], lk1[0], lq2[0], lk2[0], subln_g[0],
                  ssm_a_re[0], ssm_a_im[0], ssm_log_dt[0], ssm_b_re[0], ssm_b_im[0], ssm_c_re[0],
                  ssm_c_im[0], ssm_d[0], w_glu[0], b_glu[0], w_out[0], norm2_g[0], w_router[0],
                  b_router[0], w1[0], b1[0], w2[0], b2[0], final_g)
```

```python
import functools
import math

import jax
import jax.numpy as jnp
from jax import lax
from jax.experimental import pallas as pl
from jax.experimental.pallas import tpu as pltpu

F32 = jnp.float32
BF16 = jnp.bfloat16
HIGHEST = lax.Precision.HIGHEST

RMS_EPS = 1e-6
MASK_VALUE = -1e30
ATTN_HEADS = 4
QK_DIM = 64
V_DIM = 128
SSM_GROUP_WIDTH = 16
SSM_STATE = 64
SSM_CHUNK = 128
N_EXPERTS = 32
TOP_K = 4
SWIGLU_LIMIT = 7.0
SWIGLU_ALPHA = 1.702
LAMBDA_INIT = 0.8 - 0.6 * math.exp(-0.3 * 0)

VMEM_LIMIT = 56 * 1024 * 1024
NT_DIMS = (((1,), (1,)), ((), ()))
LANES = 128
SUBLANES = 8


def _tile_rows(t):
    return pl.ds(pl.multiple_of(t * SUBLANES, SUBLANES), SUBLANES)


def _cparams(sem, vmem=None, flags=None):
    return pltpu.CompilerParams(dimension_semantics=sem, vmem_limit_bytes=vmem, flags=flags)


def _sigmoid(x):
    return 1.0 / (1.0 + jnp.exp(-x))


def _rms(x, g):
    ms = jnp.mean(x * x, axis=-1, keepdims=True)
    return x * lax.rsqrt(ms + RMS_EPS) * g


def _adaln_kernel(c_ref, w_ref, b_ref, o_ref):
    c = c_ref[...]
    ca = c * _sigmoid(c)
    o_ref[...] = jnp.dot(ca, w_ref[...], preferred_element_type=F32, precision=HIGHEST) + b_ref[...]


def _adaln(c, w, b):
    bsz, d = c.shape
    n = w.shape[1]
    tn = 1536
    return pl.pallas_call(
        _adaln_kernel,
        grid=(n // tn,),
        in_specs=[pl.BlockSpec((bsz, d), lambda j: (0, 0)),
                  pl.BlockSpec((d, tn), lambda j: (0, j)),
                  pl.BlockSpec((1, tn), lambda j: (0, j))],
        out_specs=pl.BlockSpec((bsz, tn), lambda j: (0, j)),
        out_shape=jax.ShapeDtypeStruct((bsz, n), F32),
        compiler_params=_cparams(("arbitrary",)),
        name="adaln",
    )(c, w, b.reshape(1, n))


def _inproj_kernel(x_ref, mod_ref, g_ref, wq_ref, wut_ref, qkv_ref, ut_ref):
    x = x_ref[0]
    y = _rms(x, g_ref[...])
    h = (y * (1.0 + mod_ref[0, 1:2, :]) + mod_ref[0, 0:1, :]).astype(BF16)
    qkv_ref[0] = jnp.dot(h, wq_ref[...], preferred_element_type=F32).astype(BF16)
    ut = lax.dot_general(wut_ref[...], h, NT_DIMS, preferred_element_type=F32)
    for c in range(ut_ref.shape[2]):
        ut_ref[0, :, c, :] = ut[:, c * SSM_CHUNK:(c + 1) * SSM_CHUNK]


def _inproj(x, mod3, g, wq_bf16, wut_bf16, tm=1024):
    bsz, s, d = x.shape
    tm = min(tm, s)
    n_qkv = wq_bf16.shape[1]
    n_u = wut_bf16.shape[0]
    nck = tm // SSM_CHUNK
    return pl.pallas_call(
        _inproj_kernel,
        grid=(bsz, s // tm),
        in_specs=[pl.BlockSpec((1, tm, d), lambda b, i: (b, i, 0)),
                  pl.BlockSpec((1, 6, d), lambda b, i: (b, 0, 0)),
                  pl.BlockSpec((1, d), lambda b, i: (0, 0)),
                  pl.BlockSpec((d, n_qkv), lambda b, i: (0, 0)),
                  pl.BlockSpec((n_u, d), lambda b, i: (0, 0))],
        out_specs=[pl.BlockSpec((1, tm, n_qkv), lambda b, i: (b, i, 0)),
                   pl.BlockSpec((1, n_u, nck, SSM_CHUNK), lambda b, i: (b, 0, i, 0))],
        out_shape=[jax.ShapeDtypeStruct((bsz, s, n_qkv), BF16),
                   jax.ShapeDtypeStruct((bsz, n_u, s // SSM_CHUNK, SSM_CHUNK), F32)],
        compiler_params=_cparams(("arbitrary", "arbitrary"), VMEM_LIMIT),
        name="inproj",
    )(x, mod3, g.reshape(1, d), wq_bf16, wut_bf16)


def _attn_kernel(lq1_ref, lk1_ref, lq2_ref, lk2_ref, sg_ref, q_ref, k_ref, v_ref, o_ref,
                 vt_ref, m_ref, acc_ref, *, tq):
    qi = pl.program_id(2)
    n_acc = acc_ref.shape[0]

    @pl.when(qi == 0)
    def _():
        vt_ref[:V_DIM, :] = v_ref[0].astype(F32).T.astype(BF16)
        vt_ref[V_DIM:, :] = jnp.ones((n_acc - V_DIM, vt_ref.shape[1]), BF16)

    lam = (jnp.exp(jnp.sum(lq1_ref[...] * lk1_ref[...], axis=-1, keepdims=True))
           - jnp.exp(jnp.sum(lq2_ref[...] * lk2_ref[...], axis=-1, keepdims=True))
           + LAMBDA_INIT)
    q = q_ref[0] * (QK_DIM ** -0.5)
    lane = lax.broadcasted_iota(jnp.int32, q.shape, 1)
    zero = jnp.zeros_like(q)
    hq = tq // 2
    q1 = jnp.where(lane < QK_DIM, q, zero)
    q2 = jnp.where(lane >= QK_DIM, q, zero)
    qs = jnp.concatenate([q1[:hq], q2[:hq], q1[hq:], q2[hq:]], axis=0)
    m_ref[...] = jnp.full(m_ref.shape, MASK_VALUE, F32)
    acc_ref[...] = jnp.zeros(acc_ref.shape, F32)

    def update(start, n_keys, col0, mask):
        st = lax.dot_general(k_ref[0, pl.ds(start, n_keys), :], qs[col0:], NT_DIMS,
                             preferred_element_type=F32)
        if mask is not None:
            st = jnp.where(mask, st, MASK_VALUE)
        m_prev = m_ref[:, col0:]
        m_new = jnp.maximum(m_prev, jnp.max(st, axis=0, keepdims=True))
        pt = jnp.exp(st - m_new).astype(BF16)
        acc_ref[:, col0:] = (jnp.exp(m_prev - m_new) * acc_ref[:, col0:]
                             + jnp.dot(vt_ref[:, pl.ds(start, n_keys)], pt,
                                       preferred_element_type=F32))
        m_ref[:, col0:] = m_new

    def body(ki, carry):
        update(pl.multiple_of(ki * tq, tq), tq, 0, None)
        return carry

    lax.fori_loop(0, qi, body, 0)
    d0 = pl.multiple_of(qi * tq, tq)
    key = lax.broadcasted_iota(jnp.int32, (hq, 2 * tq), 0)
    col = lax.broadcasted_iota(jnp.int32, (hq, 2 * tq), 1)
    qpos = jnp.where(col >= tq, hq, 0) + lax.rem(col, hq)
    update(d0, hq, 0, key <= qpos)
    update(d0 + hq, hq, tq, (key + hq <= qpos)[:, tq:])
    a = acc_ref[...]

    def diff_half(c0):
        return (a[:V_DIM, c0:c0 + hq] / a[V_DIM:V_DIM + 1, c0:c0 + hq]
                - lam * (a[:V_DIM, c0 + hq:c0 + tq] / a[V_DIM:V_DIM + 1, c0 + hq:c0 + tq]))

    ot = jnp.concatenate([diff_half(0), diff_half(tq)], axis=1)
    o = _rms(ot.T, sg_ref[...]) * (1.0 - LAMBDA_INIT)
    o_ref[0] = o.astype(BF16)


def _attention(qkv, lq1, lk1, lq2, lk2, subln_g, tq=1024):
    bsz, s, _ = qkv.shape
    tq = min(tq, s)
    h = ATTN_HEADS
    n_acc = V_DIM + 8
    vec = lambda n: pl.BlockSpec((1, n), lambda b, hh, i: (0, 0))
    return pl.pallas_call(
        functools.partial(_attn_kernel, tq=tq),
        grid=(bsz, h, s // tq),
        in_specs=[vec(QK_DIM), vec(QK_DIM), vec(QK_DIM), vec(QK_DIM), vec(V_DIM),
                  pl.BlockSpec((1, tq, V_DIM), lambda b, hh, i: (b, i, hh)),
                  pl.BlockSpec((1, s, V_DIM), lambda b, hh, i: (b, 0, h + hh)),
                  pl.BlockSpec((1, s, V_DIM), lambda b, hh, i: (b, 0, 2 * h + hh))],
        out_specs=pl.BlockSpec((1, tq, V_DIM), lambda b, hh, i: (b, i, hh)),
        out_shape=jax.ShapeDtypeStruct((bsz, s, h * V_DIM), BF16),
        scratch_shapes=[pltpu.VMEM((n_acc, s), BF16),
                        pltpu.VMEM((1, 2 * tq), F32), pltpu.VMEM((n_acc, 2 * tq), F32)],
        compiler_params=_cparams(("arbitrary", "arbitrary", "arbitrary"), VMEM_LIMIT),
        name="attention",
    )(lq1.reshape(1, -1), lk1.reshape(1, -1), lq2.reshape(1, -1), lk2.reshape(1, -1),
      subln_g.reshape(1, -1), qkv, qkv, qkv)


def _ssm_tables(a_re, a_im, log_dt, b_re, b_im, c_re, c_im, t):
    g, p = a_re.shape
    w = b_re.shape[-1]
    dt = jnp.exp(log_dt.astype(F32))[:, None]
    lam = lax.complex(jnp.minimum(a_re.astype(F32), -1e-4), a_im.astype(F32))
    lam_dt = lam * dt
    lam_bar = jnp.exp(lam_dt)
    b_bar = ((lam_bar - 1.0) / lam)[..., None] * lax.complex(b_re.astype(F32), b_im.astype(F32))
    c_cplx = lax.complex(c_re.astype(F32), c_im.astype(F32))
    tau = jnp.arange(t + 1, dtype=F32)
    pw = jnp.exp(lam_dt[:, None, :] * tau[None, :, None])
    ktab = jnp.einsum('gcp,gtp,gpd->gdct', c_cplx, pw[:, :t], b_bar).real.reshape(g, w * w, t)
    wmat = b_bar.transpose(0, 2, 1)[:, :, None, :] * pw[:, t - 1 - jnp.arange(t)][:, None]
    wmat = wmat.reshape(g, w * t, p)
    vmat = c_cplx.transpose(0, 2, 1)[:, :, :, None] * pw[:, 1:t + 1].transpose(0, 2, 1)[:, :, None, :]
    vmat = vmat.reshape(g, p, w * t)
    lt = pw[:, t][:, None, :]
    return (ktab, wmat.real.astype(BF16), wmat.imag.astype(BF16),
            vmat.real.astype(BF16), (-vmat.imag).astype(BF16), lt.real, lt.imag)


def _ssm_kernel(u_ref, k_ref, wre_ref, wim_ref, vre_ref, vim_ref, ltre_ref, ltim_ref, y_ref,
                toep, ubuf, slre, slim, spre, spim, *, bsz, nc):
    gw = SSM_GROUP_WIDTH
    t = SSM_CHUNK
    r = bsz * nc

    jrow = lax.broadcasted_iota(jnp.int32, (t, t), 0)
    tcol = lax.broadcasted_iota(jnp.int32, (t, t), 1)
    causal = tcol >= jrow

    def build(d, carry):
        rows = pl.ds(pl.multiple_of(d * t, t), t)
        for c in range(gw):
            krow = k_ref[0, pl.ds(d * gw + c, 1), :]
            blk = pltpu.roll(jnp.broadcast_to(krow, (t, t)), 0, 1, stride=1, stride_axis=0)
            toep[rows, c * t:(c + 1) * t] = jnp.where(causal, blk, 0.0).astype(BF16)
        return carry

    lax.fori_loop(0, gw, build, 0)

    for d in range(gw):
        ubuf[:, d * t:(d + 1) * t] = u_ref[:, d].reshape(r, t).astype(BF16)
    u = ubuf[...]
    slre[...] = jnp.dot(u, wre_ref[0], preferred_element_type=F32)
    slim[...] = jnp.dot(u, wim_ref[0], preferred_element_type=F32)
    a = ltre_ref[0]
    b = ltim_ref[0]

    def step(c, carry):
        sre, sim = carry
        rows = pl.ds(c, bsz, stride=nc)
        spre[rows, :] = sre
        spim[rows, :] = sim
        nre = a * sre - b * sim + slre[rows, :]
        nim = a * sim + b * sre + slim[rows, :]
        return nre, nim

    z = jnp.zeros((bsz, a.shape[-1]), F32)
    lax.fori_loop(0, nc, step, (z, z))
    y = jnp.dot(u, toep[...], preferred_element_type=F32)
    y = y + jnp.dot(spre[...].astype(BF16), vre_ref[0], preferred_element_type=F32)
    y = y + jnp.dot(spim[...].astype(BF16), vim_ref[0], preferred_element_type=F32)
    for c in range(gw):
        y_ref[:, c] = y[:, c * t:(c + 1) * t].reshape(bsz, nc, t)


def _ssm(ut4, tables):
    ktab, wre, wim, vre, vim, ltre, ltim = tables
    bsz, d_ssm, nc, t = ut4.shape
    gw = SSM_GROUP_WIDTH
    g = d_ssm // gw
    p = wre.shape[-1]
    r = bsz * nc
    blk = lambda a, b: pl.BlockSpec((1, a, b), lambda i: (i, 0, 0))
    grp = pl.BlockSpec((bsz, gw, nc, t), lambda i: (0, i, 0, 0))
    return pl.pallas_call(
        functools.partial(_ssm_kernel, bsz=bsz, nc=nc),
        grid=(g,),
        in_specs=[grp, blk(gw * gw, t), blk(gw * t, p), blk(gw * t, p), blk(p, gw * t),
                  blk(p, gw * t), blk(1, p), blk(1, p)],
        out_specs=grp,
        out_shape=jax.ShapeDtypeStruct(ut4.shape, F32),
        scratch_shapes=[pltpu.VMEM((gw * t, gw * t), BF16), pltpu.VMEM((r, gw * t), BF16)]
        + [pltpu.VMEM((r, p), F32)] * 4,
        compiler_params=_cparams(("arbitrary",), VMEM_LIMIT),
        name="ssm",
    )(ut4, ktab, wre, wim, vre, vim, ltre, ltim)


def _mix_kernel(x_ref, attn_ref, yt_ref, ut_ref, mod_ref, d_ref, wglut_ref, bglu_ref, wout_ref,
                g2_ref, wrt_ref, br_ref, x1_ref, h2_ref, lgt_ref, *, d_attn):
    nck = yt_ref.shape[2]
    yt = jnp.concatenate([yt_ref[0, :, c, :] for c in range(nck)], axis=1)
    ut = jnp.concatenate([ut_ref[0, :, c, :] for c in range(nck)], axis=1)
    y = yt + d_ref[...] * ut
    z = 0.5 * y * (1.0 + jnp.tanh(math.sqrt(2.0 / math.pi) * (y + 0.044715 * (y * y * y))))
    gl = jnp.dot(wglut_ref[...], z.astype(BF16), preferred_element_type=F32) + bglu_ref[...]
    so = (z * _sigmoid(gl)).T
    mix = (jnp.dot(attn_ref[0], wout_ref[:d_attn, :], preferred_element_type=F32)
           + jnp.dot(so.astype(BF16), wout_ref[d_attn:, :], preferred_element_type=F32))
    x1 = x_ref[0] + mod_ref[0, 2:3, :] * mix
    x1_ref[0] = x1
    h2 = _rms(x1, g2_ref[...]) * (1.0 + mod_ref[0, 4:5, :]) + mod_ref[0, 3:4, :]
    for j in range(SUBLANES):
        h2_ref[pl.ds(j, h2.shape[0], stride=SUBLANES), :] = h2[:, j * LANES:(j + 1) * LANES]
    lgt_ref[...] = lax.dot_general(wrt_ref[...], h2, NT_DIMS, preferred_element_type=F32,
                                   precision=HIGHEST) + br_ref[...]


def _mix(x, attn, yt4, ut4, mod3, d_skip, wglu_t, bglu, wout, g2, wr, br, tm=1024):
    bsz, s, d = x.shape
    tm = min(tm, s)
    da = attn.shape[-1]
    ds_ = yt4.shape[1]
    e = wr.shape[-1]
    nt = s // tm
    tok = lambda n: pl.BlockSpec((1, tm, n), lambda b, i: (b, i, 0))
    chan = pl.BlockSpec((1, ds_, tm // SSM_CHUNK, SSM_CHUNK), lambda b, i: (b, 0, i, 0))
    full = lambda a, b_: pl.BlockSpec((a, b_), lambda b, i: (0, 0))
    return pl.pallas_call(
        functools.partial(_mix_kernel, d_attn=da),
        grid=(bsz, nt),
        in_specs=[tok(d), tok(da), chan, chan,
                  pl.BlockSpec((1, 6, d), lambda b, i: (b, 0, 0)),
                  full(ds_, 1), full(ds_, ds_), full(ds_, 1), full(da + ds_, d),
                  full(1, d), full(e, d), full(e, 1)],
        out_specs=[tok(d), pl.BlockSpec((tm * SUBLANES, LANES), lambda b, i: (b * nt + i, 0)),
                   pl.BlockSpec((e, tm), lambda b, i: (0, b * nt + i))],
        out_shape=[jax.ShapeDtypeStruct((bsz, s, d), F32),
                   jax.ShapeDtypeStruct((bsz * s * SUBLANES, LANES), F32),
                   jax.ShapeDtypeStruct((e, bsz * s), F32)],
        compiler_params=_cparams(("arbitrary", "arbitrary"), VMEM_LIMIT),
        name="mix",
    )(x, attn, yt4, ut4, mod3, d_skip.reshape(-1, 1), wglu_t, bglu.reshape(-1, 1), wout,
      g2.reshape(1, -1), wr.T, br.reshape(-1, 1))


def _route_kernel(lg_ref, idx_ref, gate_ref, rank_ref, cnt_ref, run_ref, *, tm):
    i = pl.program_id(0)

    @pl.when(i == 0)
    def _():
        run_ref[...] = jnp.zeros_like(run_ref)

    l = lg_ref[...]
    e = l.shape[0]
    sub = lax.broadcasted_iota(jnp.int32, l.shape, 0)
    vals, sels, idxs = [], [], []
    for _k in range(TOP_K):
        mx = jnp.max(l, axis=0, keepdims=True)
        ix = jnp.min(jnp.where(l == mx, sub, e), axis=0, keepdims=True)
        sel = sub == ix
        vals.append(mx)
        idxs.append(ix)
        sels.append(sel)
        l = jnp.where(sel, -jnp.inf, l)
    ex = [jnp.exp(v - vals[0]) for v in vals]
    den = ex[0] + ex[1] + ex[2] + ex[3]
    chosen = jnp.zeros(l.shape, F32)
    for sel in sels:
        chosen = chosen + jnp.where(sel, 1.0, 0.0)
    r_i = lax.broadcasted_iota(jnp.int32, (tm, tm), 0)
    c_i = lax.broadcasted_iota(jnp.int32, (tm, tm), 1)
    tri = jnp.where(r_i < c_i, 1.0, 0.0).astype(BF16)
    before = jnp.dot(chosen.astype(BF16), tri, preferred_element_type=F32) + run_ref[...]
    ksub = lax.broadcasted_iota(jnp.int32, (TOP_K, tm), 0)
    idx_o = jnp.zeros((TOP_K, tm), jnp.int32)
    gate_o = jnp.zeros((TOP_K, tm), F32)
    rank_o = jnp.zeros((TOP_K, tm), F32)
    for k in range(TOP_K):
        rk = jnp.sum(jnp.where(sels[k], before, 0.0), axis=0, keepdims=True)
        idx_o = jnp.where(ksub == k, idxs[k], idx_o)
        gate_o = jnp.where(ksub == k, ex[k] / den, gate_o)
        rank_o = jnp.where(ksub == k, rk, rank_o)
    idx_ref[...] = idx_o
    gate_ref[...] = gate_o
    rank_ref[...] = rank_o.astype(jnp.int32)
    run_ref[...] = run_ref[...] + jnp.sum(chosen, axis=1, keepdims=True)
    cnt_ref[...] = run_ref[...].astype(jnp.int32)


def _route(logits_t, tm=512):
    e, n = logits_t.shape
    tok = lambda w: pl.BlockSpec((w, tm), lambda i: (0, i))
    return pl.pallas_call(
        functools.partial(_route_kernel, tm=tm),
        grid=(n // tm,),
        in_specs=[tok(e)],
        out_specs=[tok(TOP_K), tok(TOP_K), tok(TOP_K), pl.BlockSpec((e, 1), lambda i: (0, 0))],
        out_shape=[jax.ShapeDtypeStruct((TOP_K, n), jnp.int32),
                   jax.ShapeDtypeStruct((TOP_K, n), F32),
                   jax.ShapeDtypeStruct((TOP_K, n), jnp.int32),
                   jax.ShapeDtypeStruct((e, 1), jnp.int32)],
        scratch_shapes=[pltpu.VMEM((e, 1), F32)],
        compiler_params=_cparams(("arbitrary",)),
        name="route",
    )(logits_t)


def _dispatch_kernel(pend_ref, padded_ref, nb_ref, pos_ref, h_ref, xs_ref, zero_ref, sem, zsem,
                     *, tm, rb, n_exp, nb_max):
    i = pl.program_id(0)

    def zero_block(blk_start):
        rows = pl.ds(pl.multiple_of(blk_start * SUBLANES, rb * SUBLANES), rb * SUBLANES)
        cp = pltpu.make_async_copy(zero_ref, xs_ref.at[rows, :], zsem)
        cp.start()
        cp.wait()

    @pl.when(i == 0)
    def _():
        zero_ref[...] = jnp.zeros_like(zero_ref)

    @pl.when(i < n_exp)
    def _():
        e = jnp.minimum(i, n_exp - 1)

        @pl.when(padded_ref[e] > 0)
        def _():
            zero_block(pend_ref[e] - rb)

        @pl.when(nb_ref[0] + e < nb_max)
        def _():
            zero_block((nb_ref[0] + e) * rb)

    @pl.when(i >= n_exp)
    def _():
        def issue(t, _):
            for k in range(TOP_K):
                p = pos_ref[0, 0, k * tm + t]
                pltpu.make_async_copy(h_ref.at[_tile_rows(t), :], xs_ref.at[_tile_rows(p), :],
                                      sem).start(priority=k % 2)
            return 0

        lax.fori_loop(0, tm, issue, 0, unroll=4)
        for k in range(TOP_K):
            pltpu.make_async_copy(h_ref, xs_ref.at[pl.ds(0, tm * SUBLANES), :], sem).wait()


def _dispatch(pend, padded, nblk, pos3, h2, n_rows, tm, rb):
    n = h2.shape[0] // SUBLANES
    n_exp = pend.shape[0]
    tile = lambda i, *_: (jnp.maximum(i - n_exp, 0), 0)
    return pl.pallas_call(
        functools.partial(_dispatch_kernel, tm=tm, rb=rb, n_exp=n_exp, nb_max=n_rows // rb),
        grid_spec=pltpu.PrefetchScalarGridSpec(
            num_scalar_prefetch=3,
            grid=(n_exp + n // tm,),
            in_specs=[pl.BlockSpec((1, 1, tm * TOP_K), lambda i, *_: tile(i) + (0,),
                                   memory_space=pltpu.SMEM),
                      pl.BlockSpec((tm * SUBLANES, LANES), tile)],
            out_specs=pl.BlockSpec(memory_space=pl.ANY),
            scratch_shapes=[pltpu.VMEM((rb * SUBLANES, LANES), F32), pltpu.SemaphoreType.DMA,
                            pltpu.SemaphoreType.DMA]),
        out_shape=jax.ShapeDtypeStruct((n_rows * SUBLANES, LANES), F32),
        compiler_params=_cparams(("arbitrary",)),
        name="dispatch",
    )(pend, padded, nblk, pos3, h2)


def _experts_kernel(be_ref, xb_ref, nb_ref, x_ref, w1_ref, b1_ref, w2_ref, b2_ref, y_ref,
                    w1b, w2b, *, f):
    i = pl.program_id(0)
    prev = be_ref[jnp.maximum(i - 1, 0)]
    live = i < nb_ref[0]

    @pl.when(live & ((i == 0) | (be_ref[i] != prev)))
    def _():
        w1b[...] = w1_ref[0].astype(BF16)
        w2b[...] = w2_ref[0].astype(BF16)

    @pl.when(live)
    def _():
        rb = x_ref.shape[0] // SUBLANES
        x = jnp.concatenate([x_ref[pl.ds(j, rb, stride=SUBLANES), :].astype(BF16)
                             for j in range(SUBLANES)], axis=1)
        gu = jnp.dot(x, w1b[...], preferred_element_type=F32) + b1_ref[0]
        gate = jnp.minimum(gu[:, :f], SWIGLU_LIMIT)
        up = jnp.clip(gu[:, f:], -SWIGLU_LIMIT, SWIGLU_LIMIT)
        glu = gate * _sigmoid(SWIGLU_ALPHA * gate)
        hmid = ((up + 1.0) * glu).astype(BF16)
        y = jnp.dot(hmid, w2b[...], preferred_element_type=F32) + b2_ref[0]
        for j in range(SUBLANES):
            y_ref[pl.ds(j, rb, stride=SUBLANES), :] = y[:, j * LANES:(j + 1) * LANES]

    @pl.when(jnp.logical_not(live))
    def _():
        y_ref[...] = jnp.zeros_like(y_ref)


def _experts(block_e, xblk, nblk, xs, w1, b1, w2, b2, rb):
    e, d, f2 = w1.shape
    assert d == SUBLANES * LANES, "token-tile layout holds one row per (8, 128) f32 tile"
    f = f2 // 2
    nb_max = xs.shape[0] // (rb * SUBLANES)
    rows = pl.BlockSpec((rb * SUBLANES, LANES), lambda i, be, xb, nb: (xb[i], 0))
    return pl.pallas_call(
        functools.partial(_experts_kernel, f=f),
        grid_spec=pltpu.PrefetchScalarGridSpec(
            num_scalar_prefetch=3,
            grid=(nb_max,),
            in_specs=[rows,
                      pl.BlockSpec((1, d, f2), lambda i, be, xb, nb: (be[i], 0, 0)),
                      pl.BlockSpec((1, 1, f2), lambda i, be, xb, nb: (be[i], 0, 0)),
                      pl.BlockSpec((1, f, d), lambda i, be, xb, nb: (be[i], 0, 0)),
                      pl.BlockSpec((1, 1, d), lambda i, be, xb, nb: (be[i], 0, 0))],
            out_specs=pl.BlockSpec((rb * SUBLANES, LANES), lambda i, be, xb, nb: (i, 0)),
            scratch_shapes=[pltpu.VMEM((d, f2), BF16), pltpu.VMEM((f, d), BF16)]),
        out_shape=jax.ShapeDtypeStruct(xs.shape, F32),
        compiler_params=_cparams(("arbitrary",), VMEM_LIMIT),
        name="experts",
    )(block_e, xblk, nblk, xs, w1, b1.reshape(e, 1, f2), w2, b2.reshape(e, 1, d))


def _combine_kernel(pos_ref, nxt_ref, gate_ref, x1_ref, mod_ref, fg_ref, ys_ref, o_ref, buf, sems,
                    *, tm):
    i = pl.program_id(0)
    n_tiles = pl.num_programs(0)
    slot = lax.rem(i, 2)

    def gather(p_ref, dst_slot):
        def issue(t, _):
            for k in range(TOP_K):
                p = p_ref[0, 0, k * tm + t]
                pltpu.make_async_copy(ys_ref.at[_tile_rows(p), :],
                                      buf.at[dst_slot, k, _tile_rows(t), :],
                                      sems.at[dst_slot]).start(priority=k % 2)
            return 0

        lax.fori_loop(0, tm, issue, 0, unroll=4)

    @pl.when(i == 0)
    def _():
        gather(pos_ref, 0)

    @pl.when(i + 1 < n_tiles)
    def _():
        gather(nxt_ref, 1 - slot)

    for k in range(TOP_K):
        pltpu.make_async_copy(ys_ref.at[pl.ds(0, tm * SUBLANES), :], buf.at[slot, k],
                              sems.at[slot]).wait()
    gate = gate_ref[...]
    pieces = []
    for j in range(SUBLANES):
        acc = gate[:, 0:1] * buf[slot, 0, pl.ds(j, tm, stride=SUBLANES), :]
        for k in range(1, TOP_K):
            acc = acc + gate[:, k:k + 1] * buf[slot, k, pl.ds(j, tm, stride=SUBLANES), :]
        pieces.append(acc)
    moe = jnp.concatenate(pieces, axis=1)
    x2 = x1_ref[...] + mod_ref[0, 5:6, :] * moe
    o_ref[...] = _rms(x2, fg_ref[...])


def _combine(pos3, gates, x1, mod3, final_g, ys, tm):
    bsz, s, d = x1.shape
    n = bsz * s
    nt = s // tm
    n_tiles = n // tm
    pos_spec = lambda f: pl.BlockSpec((1, 1, tm * TOP_K), lambda i: (f(i), 0, 0),
                                      memory_space=pltpu.SMEM)
    out = pl.pallas_call(
        functools.partial(_combine_kernel, tm=tm),
        grid=(n_tiles,),
        in_specs=[pos_spec(lambda i: i),
                  pos_spec(lambda i: jnp.minimum(i + 1, n_tiles - 1)),
                  pl.BlockSpec((tm, TOP_K), lambda i: (i, 0)),
                  pl.BlockSpec((tm, d), lambda i: (i, 0)),
                  pl.BlockSpec((1, 6, d), lambda i: (i // nt, 0, 0)),
                  pl.BlockSpec((1, d), lambda i: (0, 0)),
                  pl.BlockSpec(memory_space=pl.ANY)],
        out_specs=pl.BlockSpec((tm, d), lambda i: (i, 0)),
        out_shape=jax.ShapeDtypeStruct((n, d), F32),
        scratch_shapes=[pltpu.VMEM((2, TOP_K, tm * SUBLANES, LANES), F32),
                        pltpu.SemaphoreType.DMA((2,))],
        compiler_params=_cparams(("arbitrary",), VMEM_LIMIT),
        name="combine",
    )(pos3, pos3, gates, x1.reshape(n, d), mod3, final_g.reshape(1, d), ys)
    return out.reshape(bsz, s, d)


def _layer(x, mod3, norm1_g, w_in, lq1, lk1, lq2, lk2, subln_g, ssm_a_re, ssm_a_im, ssm_log_dt,
           ssm_b_re, ssm_b_im, ssm_c_re, ssm_c_im, ssm_d, w_glu, b_glu, w_out, norm2_g,
           w_router, b_router, w1, b1, w2, b2, final_g):
    bsz, s, d = x.shape
    n = bsz * s
    d_attn = ATTN_HEADS * V_DIM
    n_qkv = 3 * d_attn

    qkv, ut4 = _inproj(x, mod3, norm1_g, w_in[:, :n_qkv].astype(BF16),
                       w_in[:, n_qkv:].T.astype(BF16))
    attn = _attention(qkv, lq1, lk1, lq2, lk2, subln_g)
    tables = _ssm_tables(ssm_a_re, ssm_a_im, ssm_log_dt, ssm_b_re, ssm_b_im, ssm_c_re, ssm_c_im,
                         SSM_CHUNK)
    yt4 = _ssm(ut4, tables)
    x1, h2, logits_t = _mix(x, attn, yt4, ut4, mod3, ssm_d, w_glu.T.astype(BF16), b_glu,
                            w_out.astype(BF16), norm2_g, w_router, b_router)

    idx, gates, rank, counts = _route(logits_t)
    rb = 512 if n * TOP_K >= 512 * N_EXPERTS else 128
    counts = counts.reshape(N_EXPERTS)
    padded = ((counts + rb - 1) // rb) * rb
    pend = jnp.cumsum(padded).astype(jnp.int32)
    pstart = pend - padded
    eids = jnp.arange(N_EXPERTS, dtype=jnp.int32)[:, None, None]
    pos = rank + jnp.sum(jnp.where(idx[None] == eids, pstart[:, None, None], 0), axis=0)
    pos = pos.astype(jnp.int32)
    nb_max = (n * TOP_K) // rb + N_EXPERTS
    n_rows = nb_max * rb
    nblk = pend[-1] // rb
    blk_ids = jnp.minimum(jnp.arange(nb_max, dtype=jnp.int32), nblk - 1)
    block_e = jnp.minimum(jnp.sum((pend[None, :] <= (blk_ids * rb)[:, None]).astype(jnp.int32), axis=1),
                          N_EXPERTS - 1).astype(jnp.int32)
    tmd = 512
    pos3 = (pos.reshape(TOP_K, n // tmd, tmd).transpose(1, 0, 2)
            .reshape(n // tmd, 1, TOP_K * tmd))
    gates_tok = gates.T

    nblk = nblk.reshape(1).astype(jnp.int32)
    xs = _dispatch(pend, padded.astype(jnp.int32), nblk, pos3, h2, n_rows, tmd, rb)
    ys = _experts(block_e, blk_ids, nblk, xs, w1, b1, w2, b2, rb)
    return _combine(pos3, gates_tok, x1, mod3, final_g, ys, tmd)


def kernel(x, c, w_ada, b_ada, norm1_g, w_in, lq1, lk1, lq2, lk2, subln_g, ssm_a_re, ssm_a_im,
           ssm_log_dt, ssm_b_re, ssm_b_im, ssm_c_re, ssm_c_im, ssm_d, w_glu, b_glu, w_out, norm2_g,
           w_router, b_router, w1, b1, w2, b2, final_g):
    assert w_ada.shape[0] == 1, "single-layer block"
    bsz, s, d = x.shape
    mod3 = _adaln(c, w_ada[0], b_ada[0]).reshape(bsz, 6, d)
    return _layer(x, mod3, norm1_g[0], w_in[0], lq1[0], lk1[0], lq2[0], lk2[0], subln_g[0],
                  ssm_a_re[0], ssm_a_im[0], ssm_log_dt[0], ssm_b_re[0], ssm_b_im[0], ssm_c_re[0],
                  ssm_c_im[0], ssm_d[0], w_glu[0], b_glu[0], w_out[0], norm2_g[0], w_router[0],
                  b_router[0], w1[0], b1[0], w2[0], b2[0], final_g)
```

```python
import functools
import math

import jax
import jax.numpy as jnp
from jax import lax
from jax.experimental import pallas as pl
from jax.experimental.pallas import tpu as pltpu

F32 = jnp.float32
BF16 = jnp.bfloat16
HIGHEST = lax.Precision.HIGHEST

RMS_EPS = 1e-6
MASK_VALUE = -1e30
ATTN_HEADS = 4
QK_DIM = 64
V_DIM = 128
SSM_GROUP_WIDTH = 16
SSM_STATE = 64
SSM_CHUNK = 128
N_EXPERTS = 32
TOP_K = 4
SWIGLU_LIMIT = 7.0
SWIGLU_ALPHA = 1.702
LAMBDA_INIT = 0.8 - 0.6 * math.exp(-0.3 * 0)

VMEM_LIMIT = 56 * 1024 * 1024
NT_DIMS = (((1,), (1,)), ((), ()))
LANES = 128
SUBLANES = 8


def _tile_rows(t):
    return pl.ds(pl.multiple_of(t * SUBLANES, SUBLANES), SUBLANES)


def _cparams(sem, vmem=None, flags=None):
    return pltpu.CompilerParams(dimension_semantics=sem, vmem_limit_bytes=vmem, flags=flags)


def _sigmoid(x):
    return 1.0 / (1.0 + jnp.exp(-x))


def _rms(x, g):
    ms = jnp.mean(x * x, axis=-1, keepdims=True)
    return x * lax.rsqrt(ms + RMS_EPS) * g


def _adaln_kernel(c_ref, w_ref, b_ref, o_ref):
    c = c_ref[...]
    ca = c * _sigmoid(c)
    o_ref[...] = jnp.dot(ca, w_ref[...], preferred_element_type=F32, precision=HIGHEST) + b_ref[...]


def _adaln(c, w, b):
    bsz, d = c.shape
    n = w.shape[1]
    tn = 1536
    return pl.pallas_call(
        _adaln_kernel,
        grid=(n // tn,),
        in_specs=[pl.BlockSpec((bsz, d), lambda j: (0, 0)),
                  pl.BlockSpec((d, tn), lambda j: (0, j)),
                  pl.BlockSpec((1, tn), lambda j: (0, j))],
        out_specs=pl.BlockSpec((bsz, tn), lambda j: (0, j)),
        out_shape=jax.ShapeDtypeStruct((bsz, n), F32),
        compiler_params=_cparams(("arbitrary",)),
        name="adaln",
    )(c, w, b.reshape(1, n))


def _inproj_kernel(x_ref, mod_ref, g_ref, wq_ref, wut_ref, qkv_ref, ut_ref):
    x = x_ref[0]
    y = _rms(x, g_ref[...])
    h = (y * (1.0 + mod_ref[0, 1:2, :]) + mod_ref[0, 0:1, :]).astype(BF16)
    qkv_ref[0] = jnp.dot(h, wq_ref[...], preferred_element_type=F32).astype(BF16)
    ut = lax.dot_general(wut_ref[...], h, NT_DIMS, preferred_element_type=F32)
    for c in range(ut_ref.shape[2]):
        ut_ref[0, :, c, :] = ut[:, c * SSM_CHUNK:(c + 1) * SSM_CHUNK]


def _inproj(x, mod3, g, wq_bf16, wut_bf16, tm=1024):
    bsz, s, d = x.shape
    tm = min(tm, s)
    n_qkv = wq_bf16.shape[1]
    n_u = wut_bf16.shape[0]
    nck = tm // SSM_CHUNK
    return pl.pallas_call(
        _inproj_kernel,
        grid=(bsz, s // tm),
        in_specs=[pl.BlockSpec((1, tm, d), lambda b, i: (b, i, 0)),
                  pl.BlockSpec((1, 6, d), lambda b, i: (b, 0, 0)),
                  pl.BlockSpec((1, d), lambda b, i: (0, 0)),
                  pl.BlockSpec((d, n_qkv), lambda b, i: (0, 0)),
                  pl.BlockSpec((n_u, d), lambda b, i: (0, 0))],
        out_specs=[pl.BlockSpec((1, tm, n_qkv), lambda b, i: (b, i, 0)),
                   pl.BlockSpec((1, n_u, nck, SSM_CHUNK), lambda b, i: (b, 0, i, 0))],
        out_shape=[jax.ShapeDtypeStruct((bsz, s, n_qkv), BF16),
                   jax.ShapeDtypeStruct((bsz, n_u, s // SSM_CHUNK, SSM_CHUNK), F32)],
        compiler_params=_cparams(("arbitrary", "arbitrary"), VMEM_LIMIT),
        name="inproj",
    )(x, mod3, g.reshape(1, d), wq_bf16, wut_bf16)


def _attn_kernel(lq1_ref, lk1_ref, lq2_ref, lk2_ref, sg_ref, q_ref, k_ref, v_ref, o_ref,
                 vt_ref, m_ref, acc_ref, p_ref, al_ref, *, tq):
    qi = pl.program_id(2)
    n_acc = acc_ref.shape[0]

    @pl.when(qi == 0)
    def _():
        vt_ref[:V_DIM, :] = v_ref[0].astype(F32).T.astype(BF16)
        vt_ref[V_DIM:, :] = jnp.ones((n_acc - V_DIM, vt_ref.shape[1]), BF16)

    lam = (jnp.exp(jnp.sum(lq1_ref[...] * lk1_ref[...], axis=-1, keepdims=True))
           - jnp.exp(jnp.sum(lq2_ref[...] * lk2_ref[...], axis=-1, keepdims=True))
           + LAMBDA_INIT)
    q = q_ref[0] * (QK_DIM ** -0.5)
    lane = lax.broadcasted_iota(jnp.int32, q.shape, 1)
    zero = jnp.zeros_like(q)
    hq = tq // 2
    q1 = jnp.where(lane < QK_DIM, q, zero)
    q2 = jnp.where(lane >= QK_DIM, q, zero)
    qs = jnp.concatenate([q1[:hq], q2[:hq], q1[hq:], q2[hq:]], axis=0)
    m_ref[...] = jnp.full(m_ref.shape, MASK_VALUE, F32)
    acc_ref[...] = jnp.zeros(acc_ref.shape, F32)

    def softmax_stage(ki, mask):
        start = pl.multiple_of(ki * tq, tq)
        st = lax.dot_general(k_ref[0, pl.ds(start, tq), :], qs, NT_DIMS,
                             preferred_element_type=F32)
        if mask is not None:
            st = jnp.where(mask, st, MASK_VALUE)
        m_prev = m_ref[...]
        m_new = jnp.maximum(m_prev, jnp.max(st, axis=0, keepdims=True))
        m_ref[...] = m_new
        return jnp.exp(st - m_new).astype(BF16), jnp.exp(m_prev - m_new)

    def value_stage(ki, pt, alpha):
        start = pl.multiple_of(ki * tq, tq)
        acc_ref[...] = alpha * acc_ref[...] + jnp.dot(vt_ref[:, pl.ds(start, tq)], pt,
                                                      preferred_element_type=F32)

    key = lax.broadcasted_iota(jnp.int32, (tq, 2 * tq), 0)
    col = lax.broadcasted_iota(jnp.int32, (tq, 2 * tq), 1)
    causal = key <= jnp.where(col >= tq, hq, 0) + lax.rem(col, hq)

    @pl.when(qi == 0)
    def _():
        value_stage(0, *softmax_stage(0, causal))

    @pl.when(qi > 0)
    def _():
        p_ref[...], al_ref[...] = softmax_stage(0, None)

        def body(ki, carry):
            pt_prev = p_ref[...]
            al_prev = al_ref[...]
            pt, alpha = softmax_stage(ki, None)
            value_stage(ki - 1, pt_prev, al_prev)
            p_ref[...] = pt
            al_ref[...] = alpha
            return carry

        lax.fori_loop(1, qi, body, 0)
        pt_prev = p_ref[...]
        al_prev = al_ref[...]
        pt, alpha = softmax_stage(qi, causal)
        value_stage(qi - 1, pt_prev, al_prev)
        value_stage(qi, pt, alpha)

    a = acc_ref[...]

    def diff_half(c0):
        return (a[:V_DIM, c0:c0 + hq] / a[V_DIM:V_DIM + 1, c0:c0 + hq]
                - lam * (a[:V_DIM, c0 + hq:c0 + tq] / a[V_DIM:V_DIM + 1, c0 + hq:c0 + tq]))

    ot = jnp.concatenate([diff_half(0), diff_half(tq)], axis=1)
    o = _rms(ot.T, sg_ref[...]) * (1.0 - LAMBDA_INIT)
    o_ref[0] = o.astype(BF16)


def _attention(qkv, lq1, lk1, lq2, lk2, subln_g, tq=1024):
    bsz, s, _ = qkv.shape
    tq = min(tq, s)
    h = ATTN_HEADS
    n_acc = V_DIM + 8
    vec = lambda n: pl.BlockSpec((1, n), lambda b, hh, i: (0, 0))
    return pl.pallas_call(
        functools.partial(_attn_kernel, tq=tq),
        grid=(bsz, h, s // tq),
        in_specs=[vec(QK_DIM), vec(QK_DIM), vec(QK_DIM), vec(QK_DIM), vec(V_DIM),
                  pl.BlockSpec((1, tq, V_DIM), lambda b, hh, i: (b, i, hh)),
                  pl.BlockSpec((1, s, V_DIM), lambda b, hh, i: (b, 0, h + hh)),
                  pl.BlockSpec((1, s, V_DIM), lambda b, hh, i: (b, 0, 2 * h + hh))],
        out_specs=pl.BlockSpec((1, tq, V_DIM), lambda b, hh, i: (b, i, hh)),
        out_shape=jax.ShapeDtypeStruct((bsz, s, h * V_DIM), BF16),
        scratch_shapes=[pltpu.VMEM((n_acc, s), BF16),
                        pltpu.VMEM((1, 2 * tq), F32), pltpu.VMEM((n_acc, 2 * tq), F32),
                        pltpu.VMEM((tq, 2 * tq), BF16), pltpu.VMEM((1, 2 * tq), F32)],
        compiler_params=_cparams(("arbitrary", "arbitrary", "arbitrary"), VMEM_LIMIT),
        name="attention",
    )(lq1.reshape(1, -1), lk1.reshape(1, -1), lq2.reshape(1, -1), lk2.reshape(1, -1),
      subln_g.reshape(1, -1), qkv, qkv, qkv)


def _ssm_tables(a_re, a_im, log_dt, b_re, b_im, c_re, c_im, t):
    g, p = a_re.shape
    w = b_re.shape[-1]
    dt = jnp.exp(log_dt.astype(F32))[:, None]
    lam = lax.complex(jnp.minimum(a_re.astype(F32), -1e-4), a_im.astype(F32))
    lam_dt = lam * dt
    lam_bar = jnp.exp(lam_dt)
    b_bar = ((lam_bar - 1.0) / lam)[..., None] * lax.complex(b_re.astype(F32), b_im.astype(F32))
    c_cplx = lax.complex(c_re.astype(F32), c_im.astype(F32))
    tau = jnp.arange(t + 1, dtype=F32)
    pw = jnp.exp(lam_dt[:, None, :] * tau[None, :, None])
    ktab = jnp.einsum('gcp,gtp,gpd->gdct', c_cplx, pw[:, :t], b_bar).real.reshape(g, w * w, t)
    wmat = b_bar.transpose(0, 2, 1)[:, :, None, :] * pw[:, t - 1 - jnp.arange(t)][:, None]
    wmat = wmat.reshape(g, w * t, p)
    vmat = c_cplx.transpose(0, 2, 1)[:, :, :, None] * pw[:, 1:t + 1].transpose(0, 2, 1)[:, :, None, :]
    vmat = vmat.reshape(g, p, w * t)
    lt = pw[:, t][:, None, :]
    return (ktab, wmat.real.astype(BF16), wmat.imag.astype(BF16),
            vmat.real.astype(BF16), (-vmat.imag).astype(BF16), lt.real, lt.imag)


def _ssm_kernel(u_ref, k_ref, wre_ref, wim_ref, vre_ref, vim_ref, ltre_ref, ltim_ref, d_ref, y_ref,
                toep, ubuf, slre, slim, spre, spim, *, bsz, nc):
    gw = SSM_GROUP_WIDTH
    t = SSM_CHUNK
    r = bsz * nc

    jrow = lax.broadcasted_iota(jnp.int32, (t, t), 0)
    tcol = lax.broadcasted_iota(jnp.int32, (t, t), 1)
    causal = tcol >= jrow

    def build(d, carry):
        rows = pl.ds(pl.multiple_of(d * t, t), t)
        for c in range(gw):
            krow = k_ref[0, pl.ds(d * gw + c, 1), :]
            blk = pltpu.roll(jnp.broadcast_to(krow, (t, t)), 0, 1, stride=1, stride_axis=0)
            toep[rows, c * t:(c + 1) * t] = jnp.where(causal, blk, 0.0).astype(BF16)
        return carry

    lax.fori_loop(0, gw, build, 0)

    for d in range(gw):
        ubuf[:, d * t:(d + 1) * t] = u_ref[:, d].reshape(r, t).astype(BF16)
    u = ubuf[...]
    slre[...] = jnp.dot(u, wre_ref[0], preferred_element_type=F32)
    slim[...] = jnp.dot(u, wim_ref[0], preferred_element_type=F32)
    a = ltre_ref[0]
    b = ltim_ref[0]

    def step(c, carry):
        sre, sim = carry
        rows = pl.ds(c, bsz, stride=nc)
        spre[rows, :] = sre
        spim[rows, :] = sim
        nre = a * sre - b * sim + slre[rows, :]
        nim = a * sim + b * sre + slim[rows, :]
        return nre, nim

    z = jnp.zeros((bsz, a.shape[-1]), F32)
    lax.fori_loop(0, nc, step, (z, z))
    y = jnp.dot(u, toep[...], preferred_element_type=F32)
    y = y + jnp.dot(spre[...].astype(BF16), vre_ref[0], preferred_element_type=F32)
    y = y + jnp.dot(spim[...].astype(BF16), vim_ref[0], preferred_element_type=F32)
    for c in range(gw):
        y_ref[:, c] = (y[:, c * t:(c + 1) * t].reshape(bsz, nc, t)
                       + d_ref[0, c:c + 1, :] * u_ref[:, c])


def _ssm(ut4, tables, d_skip):
    ktab, wre, wim, vre, vim, ltre, ltim = tables
    bsz, d_ssm, nc, t = ut4.shape
    d_tab = jnp.broadcast_to(d_skip.reshape(-1, SSM_GROUP_WIDTH, 1), (d_skip.size // SSM_GROUP_WIDTH,
                                                                    SSM_GROUP_WIDTH, t))
    gw = SSM_GROUP_WIDTH
    g = d_ssm // gw
    p = wre.shape[-1]
    r = bsz * nc
    blk = lambda a, b: pl.BlockSpec((1, a, b), lambda i: (i, 0, 0))
    grp = pl.BlockSpec((bsz, gw, nc, t), lambda i: (0, i, 0, 0))
    return pl.pallas_call(
        functools.partial(_ssm_kernel, bsz=bsz, nc=nc),
        grid=(g,),
        in_specs=[grp, blk(gw * gw, t), blk(gw * t, p), blk(gw * t, p), blk(p, gw * t),
                  blk(p, gw * t), blk(1, p), blk(1, p), blk(gw, t)],
        out_specs=grp,
        out_shape=jax.ShapeDtypeStruct(ut4.shape, F32),
        scratch_shapes=[pltpu.VMEM((gw * t, gw * t), BF16), pltpu.VMEM((r, gw * t), BF16)]
        + [pltpu.VMEM((r, p), F32)] * 4,
        compiler_params=_cparams(("arbitrary",), VMEM_LIMIT),
        name="ssm",
    )(ut4, ktab, wre, wim, vre, vim, ltre, ltim, d_tab)


def _mix_kernel(x_ref, attn_ref, yt_ref, mod_ref, wglut_ref, bglu_ref, wout_ref,
                g2_ref, wrt_ref, br_ref, x1_ref, h2_ref, lgt_ref, *, d_attn):
    nck = yt_ref.shape[2]
    y = jnp.concatenate([yt_ref[0, :, c, :] for c in range(nck)], axis=1)
    z = 0.5 * y * (1.0 + jnp.tanh(math.sqrt(2.0 / math.pi) * (y + 0.044715 * (y * y * y))))
    gl = jnp.dot(wglut_ref[...], z.astype(BF16), preferred_element_type=F32) + bglu_ref[...]
    so = (z * _sigmoid(gl)).T
    mix = (jnp.dot(attn_ref[0], wout_ref[:d_attn, :], preferred_element_type=F32)
           + jnp.dot(so.astype(BF16), wout_ref[d_attn:, :], preferred_element_type=F32))
    x1 = x_ref[0] + mod_ref[0, 2:3, :] * mix
    x1_ref[0] = x1
    h2 = _rms(x1, g2_ref[...]) * (1.0 + mod_ref[0, 4:5, :]) + mod_ref[0, 3:4, :]
    for j in range(SUBLANES):
        h2_ref[pl.ds(j, h2.shape[0], stride=SUBLANES), :] = h2[:, j * LANES:(j + 1) * LANES]
    lgt_ref[...] = lax.dot_general(wrt_ref[...], h2, NT_DIMS, preferred_element_type=F32,
                                   precision=HIGHEST) + br_ref[...]


def _mix(x, attn, yt4, mod3, wglu_t, bglu, wout, g2, wr, br, tm=1024):
    bsz, s, d = x.shape
    tm = min(tm, s)
    da = attn.shape[-1]
    ds_ = yt4.shape[1]
    e = wr.shape[-1]
    nt = s // tm
    tok = lambda n: pl.BlockSpec((1, tm, n), lambda b, i: (b, i, 0))
    chan = pl.BlockSpec((1, ds_, tm // SSM_CHUNK, SSM_CHUNK), lambda b, i: (b, 0, i, 0))
    full = lambda a, b_: pl.BlockSpec((a, b_), lambda b, i: (0, 0))
    return pl.pallas_call(
        functools.partial(_mix_kernel, d_attn=da),
        grid=(bsz, nt),
        in_specs=[tok(d), tok(da), chan,
                  pl.BlockSpec((1, 6, d), lambda b, i: (b, 0, 0)),
                  full(ds_, ds_), full(ds_, 1), full(da + ds_, d),
                  full(1, d), full(e, d), full(e, 1)],
        out_specs=[tok(d), pl.BlockSpec((tm * SUBLANES, LANES), lambda b, i: (b * nt + i, 0)),
                   pl.BlockSpec((e, tm), lambda b, i: (0, b * nt + i))],
        out_shape=[jax.ShapeDtypeStruct((bsz, s, d), F32),
                   jax.ShapeDtypeStruct((bsz * s * SUBLANES, LANES), F32),
                   jax.ShapeDtypeStruct((e, bsz * s), F32)],
        compiler_params=_cparams(("arbitrary", "arbitrary"), VMEM_LIMIT),
        name="mix",
    )(x, attn, yt4, mod3, wglu_t, bglu.reshape(-1, 1), wout,
      g2.reshape(1, -1), wr.T, br.reshape(-1, 1))


def _route_kernel(lg_ref, idx_ref, gate_ref, rank_ref, cnt_ref, run_ref, *, tm):
    i = pl.program_id(0)

    @pl.when(i == 0)
    def _():
        run_ref[...] = jnp.zeros_like(run_ref)

    l = lg_ref[...]
    e = l.shape[0]
    sub = lax.broadcasted_iota(jnp.int32, l.shape, 0)
    vals, sels, idxs = [], [], []
    for _k in range(TOP_K):
        mx = jnp.max(l, axis=0, keepdims=True)
        ix = jnp.min(jnp.where(l == mx, sub, e), axis=0, keepdims=True)
        sel = sub == ix
        vals.append(mx)
        idxs.append(ix)
        sels.append(sel)
        l = jnp.where(sel, -jnp.inf, l)
    ex = [jnp.exp(v - vals[0]) for v in vals]
    den = ex[0] + ex[1] + ex[2] + ex[3]
    chosen = jnp.zeros(l.shape, F32)
    for sel in sels:
        chosen = chosen + jnp.where(sel, 1.0, 0.0)
    r_i = lax.broadcasted_iota(jnp.int32, (tm, tm), 0)
    c_i = lax.broadcasted_iota(jnp.int32, (tm, tm), 1)
    tri = jnp.where(r_i < c_i, 1.0, 0.0).astype(BF16)
    before = jnp.dot(chosen.astype(BF16), tri, preferred_element_type=F32) + run_ref[...]
    ksub = lax.broadcasted_iota(jnp.int32, (TOP_K, tm), 0)
    idx_o = jnp.zeros((TOP_K, tm), jnp.int32)
    gate_o = jnp.zeros((TOP_K, tm), F32)
    rank_o = jnp.zeros((TOP_K, tm), F32)
    for k in range(TOP_K):
        rk = jnp.sum(jnp.where(sels[k], before, 0.0), axis=0, keepdims=True)
        idx_o = jnp.where(ksub == k, idxs[k], idx_o)
        gate_o = jnp.where(ksub == k, ex[k] / den, gate_o)
        rank_o = jnp.where(ksub == k, rk, rank_o)
    idx_ref[...] = idx_o
    gate_ref[...] = gate_o
    rank_ref[...] = rank_o.astype(jnp.int32)
    run_ref[...] = run_ref[...] + jnp.sum(chosen, axis=1, keepdims=True)
    cnt_ref[...] = run_ref[...].astype(jnp.int32)


def _route(logits_t, tm=512):
    e, n = logits_t.shape
    tok = lambda w: pl.BlockSpec((w, tm), lambda i: (0, i))
    return pl.pallas_call(
        functools.partial(_route_kernel, tm=tm),
        grid=(n // tm,),
        in_specs=[tok(e)],
        out_specs=[tok(TOP_K), tok(TOP_K), tok(TOP_K), pl.BlockSpec((e, 1), lambda i: (0, 0))],
        out_shape=[jax.ShapeDtypeStruct((TOP_K, n), jnp.int32),
                   jax.ShapeDtypeStruct((TOP_K, n), F32),
                   jax.ShapeDtypeStruct((TOP_K, n), jnp.int32),
                   jax.ShapeDtypeStruct((e, 1), jnp.int32)],
        scratch_shapes=[pltpu.VMEM((e, 1), F32)],
        compiler_params=_cparams(("arbitrary",)),
        name="route",
    )(logits_t)


def _dispatch_kernel(pend_ref, padded_ref, nb_ref, pos_ref, h_ref, xs_ref, zero_ref, sem, zsem,
                     *, tm, rb, n_exp, nb_max):
    i = pl.program_id(0)

    def zero_block(blk_start):
        rows = pl.ds(pl.multiple_of(blk_start * SUBLANES, rb * SUBLANES), rb * SUBLANES)
        cp = pltpu.make_async_copy(zero_ref, xs_ref.at[rows, :], zsem)
        cp.start()
        cp.wait()

    @pl.when(i == 0)
    def _():
        zero_ref[...] = jnp.zeros_like(zero_ref)

    @pl.when(i < n_exp)
    def _():
        e = jnp.minimum(i, n_exp - 1)

        @pl.when(padded_ref[e] > 0)
        def _():
            zero_block(pend_ref[e] - rb)

        @pl.when(nb_ref[0] + e < nb_max)
        def _():
            zero_block((nb_ref[0] + e) * rb)

    @pl.when(i >= n_exp)
    def _():
        def issue(t, _):
            for k in range(TOP_K):
                p = pos_ref[0, 0, k * tm + t]
                pltpu.make_async_copy(h_ref.at[_tile_rows(t), :], xs_ref.at[_tile_rows(p), :],
                                      sem).start(priority=k % 2)
            return 0

        lax.fori_loop(0, tm, issue, 0, unroll=4)
        for k in range(TOP_K):
            pltpu.make_async_copy(h_ref, xs_ref.at[pl.ds(0, tm * SUBLANES), :], sem).wait()


def _dispatch(pend, padded, nblk, pos3, h2, n_rows, tm, rb):
    n = h2.shape[0] // SUBLANES
    n_exp = pend.shape[0]
    tile = lambda i, *_: (jnp.maximum(i - n_exp, 0), 0)
    return pl.pallas_call(
        functools.partial(_dispatch_kernel, tm=tm, rb=rb, n_exp=n_exp, nb_max=n_rows // rb),
        grid_spec=pltpu.PrefetchScalarGridSpec(
            num_scalar_prefetch=3,
            grid=(n_exp + n // tm,),
            in_specs=[pl.BlockSpec((1, 1, tm * TOP_K), lambda i, *_: tile(i) + (0,),
                                   memory_space=pltpu.SMEM),
                      pl.BlockSpec((tm * SUBLANES, LANES), tile)],
            out_specs=pl.BlockSpec(memory_space=pl.ANY),
            scratch_shapes=[pltpu.VMEM((rb * SUBLANES, LANES), F32), pltpu.SemaphoreType.DMA,
                            pltpu.SemaphoreType.DMA]),
        out_shape=jax.ShapeDtypeStruct((n_rows * SUBLANES, LANES), F32),
        compiler_params=_cparams(("arbitrary",)),
        name="dispatch",
    )(pend, padded, nblk, pos3, h2)


def _experts_kernel(be_ref, xb_ref, nb_ref, x_ref, w1_ref, b1_ref, w2_ref, b2_ref, y_ref,
                    w1b, w2b, *, f):
    i = pl.program_id(0)
    prev = be_ref[jnp.maximum(i - 1, 0)]
    live = i < nb_ref[0]

    @pl.when(live & ((i == 0) | (be_ref[i] != prev)))
    def _():
        w1b[...] = w1_ref[0].astype(BF16)
        w2b[...] = w2_ref[0].astype(BF16)

    @pl.when(live)
    def _():
        rb = x_ref.shape[0] // SUBLANES
        x = jnp.concatenate([x_ref[pl.ds(j, rb, stride=SUBLANES), :].astype(BF16)
                             for j in range(SUBLANES)], axis=1)
        gu = jnp.dot(x, w1b[...], preferred_element_type=F32) + b1_ref[0]
        gate = jnp.minimum(gu[:, :f], SWIGLU_LIMIT)
        up = jnp.clip(gu[:, f:], -SWIGLU_LIMIT, SWIGLU_LIMIT)
        glu = gate * _sigmoid(SWIGLU_ALPHA * gate)
        hmid = ((up + 1.0) * glu).astype(BF16)
        y = jnp.dot(hmid, w2b[...], preferred_element_type=F32) + b2_ref[0]
        for j in range(SUBLANES):
            y_ref[pl.ds(j, rb, stride=SUBLANES), :] = y[:, j * LANES:(j + 1) * LANES]

    @pl.when(jnp.logical_not(live))
    def _():
        y_ref[...] = jnp.zeros_like(y_ref)


def _experts(block_e, xblk, nblk, xs, w1, b1, w2, b2, rb):
    e, d, f2 = w1.shape
    assert d == SUBLANES * LANES, "token-tile layout holds one row per (8, 128) f32 tile"
    f = f2 // 2
    nb_max = xs.shape[0] // (rb * SUBLANES)
    rows = pl.BlockSpec((rb * SUBLANES, LANES), lambda i, be, xb, nb: (xb[i], 0))
    return pl.pallas_call(
        functools.partial(_experts_kernel, f=f),
        grid_spec=pltpu.PrefetchScalarGridSpec(
            num_scalar_prefetch=3,
            grid=(nb_max,),
            in_specs=[rows,
                      pl.BlockSpec((1, d, f2), lambda i, be, xb, nb: (be[i], 0, 0)),
                      pl.BlockSpec((1, 1, f2), lambda i, be, xb, nb: (be[i], 0, 0)),
                      pl.BlockSpec((1, f, d), lambda i, be, xb, nb: (be[i], 0, 0)),
                      pl.BlockSpec((1, 1, d), lambda i, be, xb, nb: (be[i], 0, 0))],
            out_specs=pl.BlockSpec((rb * SUBLANES, LANES), lambda i, be, xb, nb: (i, 0)),
            scratch_shapes=[pltpu.VMEM((d, f2), BF16), pltpu.VMEM((f, d), BF16)]),
        out_shape=jax.ShapeDtypeStruct(xs.shape, F32),
        compiler_params=_cparams(("arbitrary",), VMEM_LIMIT),
        name="experts",
    )(block_e, xblk, nblk, xs, w1, b1.reshape(e, 1, f2), w2, b2.reshape(e, 1, d))


def _combine_kernel(pos_ref, nxt_ref, gate_ref, x1_ref, mod_ref, fg_ref, ys_ref, o_ref, buf, sems,
                    *, tm):
    i = pl.program_id(0)
    n_tiles = pl.num_programs(0)
    slot = lax.rem(i, 2)

    def gather(p_ref, dst_slot):
        def issue(t, _):
            for k in range(TOP_K):
                p = p_ref[0, 0, k * tm + t]
                pltpu.make_async_copy(ys_ref.at[_tile_rows(p), :],
                                      buf.at[dst_slot, k, _tile_rows(t), :],
                                      sems.at[dst_slot]).start(priority=k % 2)
            return 0

        lax.fori_loop(0, tm, issue, 0, unroll=4)

    @pl.when(i == 0)
    def _():
        gather(pos_ref, 0)

    @pl.when(i + 1 < n_tiles)
    def _():
        gather(nxt_ref, 1 - slot)

    for k in range(TOP_K):
        pltpu.make_async_copy(ys_ref.at[pl.ds(0, tm * SUBLANES), :], buf.at[slot, k],
                              sems.at[slot]).wait()
    gate = gate_ref[...]
    pieces = []
    for j in range(SUBLANES):
        acc = gate[:, 0:1] * buf[slot, 0, pl.ds(j, tm, stride=SUBLANES), :]
        for k in range(1, TOP_K):
            acc = acc + gate[:, k:k + 1] * buf[slot, k, pl.ds(j, tm, stride=SUBLANES), :]
        pieces.append(acc)
    moe = jnp.concatenate(pieces, axis=1)
    x2 = x1_ref[...] + mod_ref[0, 5:6, :] * moe
    o_ref[...] = _rms(x2, fg_ref[...])


def _combine(pos3, gates, x1, mod3, final_g, ys, tm):
    bsz, s, d = x1.shape
    n = bsz * s
    nt = s // tm
    n_tiles = n // tm
    pos_spec = lambda f: pl.BlockSpec((1, 1, tm * TOP_K), lambda i: (f(i), 0, 0),
                                      memory_space=pltpu.SMEM)
    out = pl.pallas_call(
        functools.partial(_combine_kernel, tm=tm),
        grid=(n_tiles,),
        in_specs=[pos_spec(lambda i: i),
                  pos_spec(lambda i: jnp.minimum(i + 1, n_tiles - 1)),
                  pl.BlockSpec((tm, TOP_K), lambda i: (i, 0)),
                  pl.BlockSpec((tm, d), lambda i: (i, 0)),
                  pl.BlockSpec((1, 6, d), lambda i: (i // nt, 0, 0)),
                  pl.BlockSpec((1, d), lambda i: (0, 0)),
                  pl.BlockSpec(memory_space=pl.ANY)],
        out_specs=pl.BlockSpec((tm, d), lambda i: (i, 0)),
        out_shape=jax.ShapeDtypeStruct((n, d), F32),
        scratch_shapes=[pltpu.VMEM((2, TOP_K, tm * SUBLANES, LANES), F32),
                        pltpu.SemaphoreType.DMA((2,))],
        compiler_params=_cparams(("arbitrary",), VMEM_LIMIT),
        name="combine",
    )(pos3, pos3, gates, x1.reshape(n, d), mod3, final_g.reshape(1, d), ys)
    return out.reshape(bsz, s, d)


def _layer(x, mod3, norm1_g, w_in, lq1, lk1, lq2, lk2, subln_g, ssm_a_re, ssm_a_im, ssm_log_dt,
           ssm_b_re, ssm_b_im, ssm_c_re, ssm_c_im, ssm_d, w_glu, b_glu, w_out, norm2_g,
           w_router, b_router, w1, b1, w2, b2, final_g):
    bsz, s, d = x.shape
    n = bsz * s
    d_attn = ATTN_HEADS * V_DIM
    n_qkv = 3 * d_attn

    qkv, ut4 = _inproj(x, mod3, norm1_g, w_in[:, :n_qkv].astype(BF16),
                       w_in[:, n_qkv:].T.astype(BF16))
    attn = _attention(qkv, lq1, lk1, lq2, lk2, subln_g)
    tables = _ssm_tables(ssm_a_re, ssm_a_im, ssm_log_dt, ssm_b_re, ssm_b_im, ssm_c_re, ssm_c_im,
                         SSM_CHUNK)
    yt4 = _ssm(ut4, tables, ssm_d)
    x1, h2, logits_t = _mix(x, attn, yt4, mod3, w_glu.T.astype(BF16), b_glu,
                            w_out.astype(BF16), norm2_g, w_router, b_router)

    idx, gates, rank, counts = _route(logits_t)
    rb = 512 if n * TOP_K >= 512 * N_EXPERTS else 128
    counts = counts.reshape(N_EXPERTS)
    padded = ((counts + rb - 1) // rb) * rb
    pend = jnp.cumsum(padded).astype(jnp.int32)
    pstart = pend - padded
    eids = jnp.arange(N_EXPERTS, dtype=jnp.int32)[:, None, None]
    pos = rank + jnp.sum(jnp.where(idx[None] == eids, pstart[:, None, None], 0), axis=0)
    pos = pos.astype(jnp.int32)
    nb_max = (n * TOP_K) // rb + N_EXPERTS
    n_rows = nb_max * rb
    nblk = pend[-1] // rb
    blk_ids = jnp.minimum(jnp.arange(nb_max, dtype=jnp.int32), nblk - 1)
    block_e = jnp.minimum(jnp.sum((pend[None, :] <= (blk_ids * rb)[:, None]).astype(jnp.int32), axis=1),
                          N_EXPERTS - 1).astype(jnp.int32)
    tmd = 512
    pos3 = (pos.reshape(TOP_K, n // tmd, tmd).transpose(1, 0, 2)
            .reshape(n // tmd, 1, TOP_K * tmd))
    gates_tok = gates.T

    nblk = nblk.reshape(1).astype(jnp.int32)
    xs = _dispatch(pend, padded.astype(jnp.int32), nblk, pos3, h2, n_rows, tmd, rb)
    ys = _experts(block_e, blk_ids, nblk, xs, w1, b1, w2, b2, rb)
    return _combine(pos3, gates_tok, x1, mod3, final_g, ys, tmd)


def kernel(x, c, w_ada, b_ada, norm1_g, w_in, lq1, lk1, lq2, lk2, subln_g, ssm_a_re, ssm_a_im,
           ssm_log_dt, ssm_b_re, ssm_b_im, ssm_c_re, ssm_c_im, ssm_d, w_glu, b_glu, w_out, norm2_g,
           w_router, b_router, w1, b1, w2, b2, final_g):
    assert w_ada.shape[0] == 1, "single-layer block"
    bsz, s, d = x.shape
    mod3 = _adaln(c, w_ada[0], b_ada[0]).reshape(bsz, 6, d)
    return _layer(x, mod3, norm1_g[0], w_in[0], lq1[0], lk1[0], lq2[0], lk2[0], subln_g[0],
                  ssm_a_re[0], ssm_a_im[0], ssm_log_dt[0], ssm_b_re[0], ssm_b_im[0], ssm_c_re[0],
                  ssm_c_im[0], ssm_d[0], w_glu[0], b_glu[0], w_out[0], norm2_g[0], w_router[0],
                  b_router[0], w1[0], b1[0], w2[0], b2[0], final_g)
```

```python
import functools
import math

import jax
import jax.numpy as jnp
from jax import lax
from jax.experimental import pallas as pl
from jax.experimental.pallas import tpu as pltpu

F32 = jnp.float32
BF16 = jnp.bfloat16
HIGHEST = lax.Precision.HIGHEST

RMS_EPS = 1e-6
MASK_VALUE = -1e30
ATTN_HEADS = 4
QK_DIM = 64
V_DIM = 128
SSM_GROUP_WIDTH = 16
SSM_STATE = 64
SSM_CHUNK = 128
N_EXPERTS = 32
TOP_K = 4
SWIGLU_LIMIT = 7.0
SWIGLU_ALPHA = 1.702
LAMBDA_INIT = 0.8 - 0.6 * math.exp(-0.3 * 0)

VMEM_LIMIT = 56 * 1024 * 1024
NT_DIMS = (((1,), (1,)), ((), ()))
LANES = 128
SUBLANES = 8


def _tile_rows(t):
    return pl.ds(pl.multiple_of(t * SUBLANES, SUBLANES), SUBLANES)


def _cparams(sem, vmem=None, flags=None):
    return pltpu.CompilerParams(dimension_semantics=sem, vmem_limit_bytes=vmem, flags=flags)


def _sigmoid(x):
    return 1.0 / (1.0 + jnp.exp(-x))


def _rms(x, g):
    ms = jnp.mean(x * x, axis=-1, keepdims=True)
    return x * lax.rsqrt(ms + RMS_EPS) * g


def _adaln_kernel(c_ref, w_ref, b_ref, o_ref):
    c = c_ref[...]
    ca = c * _sigmoid(c)
    o_ref[...] = jnp.dot(ca, w_ref[...], preferred_element_type=F32, precision=HIGHEST) + b_ref[...]


def _adaln(c, w, b):
    bsz, d = c.shape
    n = w.shape[1]
    tn = 1536
    return pl.pallas_call(
        _adaln_kernel,
        grid=(n // tn,),
        in_specs=[pl.BlockSpec((bsz, d), lambda j: (0, 0)),
                  pl.BlockSpec((d, tn), lambda j: (0, j)),
                  pl.BlockSpec((1, tn), lambda j: (0, j))],
        out_specs=pl.BlockSpec((bsz, tn), lambda j: (0, j)),
        out_shape=jax.ShapeDtypeStruct((bsz, n), F32),
        compiler_params=_cparams(("arbitrary",)),
        name="adaln",
    )(c, w, b.reshape(1, n))


def _inproj_kernel(x_ref, mod_ref, g_ref, wq_ref, wut_ref, qkv_ref, ut_ref):
    x = x_ref[0]
    y = _rms(x, g_ref[...])
    h = (y * (1.0 + mod_ref[0, 1:2, :]) + mod_ref[0, 0:1, :]).astype(BF16)
    qkv_ref[0] = jnp.dot(h, wq_ref[...], preferred_element_type=F32).astype(BF16)
    ut = lax.dot_general(wut_ref[...], h, NT_DIMS, preferred_element_type=F32)
    for c in range(ut_ref.shape[2]):
        ut_ref[0, :, c, :] = ut[:, c * SSM_CHUNK:(c + 1) * SSM_CHUNK]


def _inproj(x, mod3, g, wq_bf16, wut_bf16, tm=1024):
    bsz, s, d = x.shape
    tm = min(tm, s)
    n_qkv = wq_bf16.shape[1]
    n_u = wut_bf16.shape[0]
    nck = tm // SSM_CHUNK
    return pl.pallas_call(
        _inproj_kernel,
        grid=(bsz, s // tm),
        in_specs=[pl.BlockSpec((1, tm, d), lambda b, i: (b, i, 0)),
                  pl.BlockSpec((1, 6, d), lambda b, i: (b, 0, 0)),
                  pl.BlockSpec((1, d), lambda b, i: (0, 0)),
                  pl.BlockSpec((d, n_qkv), lambda b, i: (0, 0)),
                  pl.BlockSpec((n_u, d), lambda b, i: (0, 0))],
        out_specs=[pl.BlockSpec((1, tm, n_qkv), lambda b, i: (b, i, 0)),
                   pl.BlockSpec((1, n_u, nck, SSM_CHUNK), lambda b, i: (b, 0, i, 0))],
        out_shape=[jax.ShapeDtypeStruct((bsz, s, n_qkv), BF16),
                   jax.ShapeDtypeStruct((bsz, n_u, s // SSM_CHUNK, SSM_CHUNK), F32)],
        compiler_params=_cparams(("arbitrary", "arbitrary"), VMEM_LIMIT),
        name="inproj",
    )(x, mod3, g.reshape(1, d), wq_bf16, wut_bf16)


def _attn_kernel(lq1_ref, lk1_ref, lq2_ref, lk2_ref, sg_ref, q_ref, k_ref, v_ref, o_ref,
                 vt_ref, m_ref, acc_ref, p_ref, al_ref, *, tq):
    qi = pl.program_id(2)
    n_acc = acc_ref.shape[0]

    @pl.when(qi == 0)
    def _():
        vt_ref[:V_DIM, :] = v_ref[0].astype(F32).T.astype(BF16)
        vt_ref[V_DIM:, :] = jnp.ones((n_acc - V_DIM, vt_ref.shape[1]), BF16)

    lam = (jnp.exp(jnp.sum(lq1_ref[...] * lk1_ref[...], axis=-1, keepdims=True))
           - jnp.exp(jnp.sum(lq2_ref[...] * lk2_ref[...], axis=-1, keepdims=True))
           + LAMBDA_INIT)
    q = q_ref[0] * (QK_DIM ** -0.5)
    lane = lax.broadcasted_iota(jnp.int32, q.shape, 1)
    zero = jnp.zeros_like(q)
    hq = tq // 2
    q1 = jnp.where(lane < QK_DIM, q, zero)
    q2 = jnp.where(lane >= QK_DIM, q, zero)
    qs = jnp.concatenate([q1[:hq], q2[:hq], q1[hq:], q2[hq:]], axis=0)
    m_ref[...] = jnp.full(m_ref.shape, MASK_VALUE, F32)
    acc_ref[...] = jnp.zeros(acc_ref.shape, F32)

    def softmax_stage(ki, mask):
        start = pl.multiple_of(ki * tq, tq)
        st = lax.dot_general(k_ref[0, pl.ds(start, tq), :], qs, NT_DIMS,
                             preferred_element_type=F32)
        if mask is not None:
            st = jnp.where(mask, st, MASK_VALUE)
        m_prev = m_ref[...]
        m_new = jnp.maximum(m_prev, jnp.max(st, axis=0, keepdims=True))
        m_ref[...] = m_new
        return jnp.exp(st - m_new).astype(BF16), jnp.exp(m_prev - m_new)

    def value_stage(ki, pt, alpha):
        start = pl.multiple_of(ki * tq, tq)
        acc_ref[...] = alpha * acc_ref[...] + jnp.dot(vt_ref[:, pl.ds(start, tq)], pt,
                                                      preferred_element_type=F32)

    key = lax.broadcasted_iota(jnp.int32, (tq, 2 * tq), 0)
    col = lax.broadcasted_iota(jnp.int32, (tq, 2 * tq), 1)
    causal = key <= jnp.where(col >= tq, hq, 0) + lax.rem(col, hq)

    @pl.when(qi == 0)
    def _():
        value_stage(0, *softmax_stage(0, causal))

    @pl.when(qi > 0)
    def _():
        p_ref[...], al_ref[...] = softmax_stage(0, None)

        def body(ki, carry):
            pt_prev = p_ref[...]
            al_prev = al_ref[...]
            pt, alpha = softmax_stage(ki, None)
            value_stage(ki - 1, pt_prev, al_prev)
            p_ref[...] = pt
            al_ref[...] = alpha
            return carry

        lax.fori_loop(1, qi, body, 0)
        pt_prev = p_ref[...]
        al_prev = al_ref[...]
        pt, alpha = softmax_stage(qi, causal)
        value_stage(qi - 1, pt_prev, al_prev)
        value_stage(qi, pt, alpha)

    a = acc_ref[...]

    def diff_half(c0):
        return (a[:V_DIM, c0:c0 + hq] / a[V_DIM:V_DIM + 1, c0:c0 + hq]
                - lam * (a[:V_DIM, c0 + hq:c0 + tq] / a[V_DIM:V_DIM + 1, c0 + hq:c0 + tq]))

    ot = jnp.concatenate([diff_half(0), diff_half(tq)], axis=1)
    o = _rms(ot.T, sg_ref[...]) * (1.0 - LAMBDA_INIT)
    o_ref[0] = o.astype(BF16)


def _attention(qkv, lq1, lk1, lq2, lk2, subln_g, tq=1024):
    bsz, s, _ = qkv.shape
    tq = min(tq, s)
    h = ATTN_HEADS
    n_acc = V_DIM + 8
    vec = lambda n: pl.BlockSpec((1, n), lambda b, hh, i: (0, 0))
    return pl.pallas_call(
        functools.partial(_attn_kernel, tq=tq),
        grid=(bsz, h, s // tq),
        in_specs=[vec(QK_DIM), vec(QK_DIM), vec(QK_DIM), vec(QK_DIM), vec(V_DIM),
                  pl.BlockSpec((1, tq, V_DIM), lambda b, hh, i: (b, i, hh)),
                  pl.BlockSpec((1, s, V_DIM), lambda b, hh, i: (b, 0, h + hh)),
                  pl.BlockSpec((1, s, V_DIM), lambda b, hh, i: (b, 0, 2 * h + hh))],
        out_specs=pl.BlockSpec((1, tq, V_DIM), lambda b, hh, i: (b, i, hh)),
        out_shape=jax.ShapeDtypeStruct((bsz, s, h * V_DIM), BF16),
        scratch_shapes=[pltpu.VMEM((n_acc, s), BF16),
                        pltpu.VMEM((1, 2 * tq), F32), pltpu.VMEM((n_acc, 2 * tq), F32),
                        pltpu.VMEM((tq, 2 * tq), BF16), pltpu.VMEM((1, 2 * tq), F32)],
        compiler_params=_cparams(("arbitrary", "arbitrary", "arbitrary"), VMEM_LIMIT),
        name="attention",
    )(lq1.reshape(1, -1), lk1.reshape(1, -1), lq2.reshape(1, -1), lk2.reshape(1, -1),
      subln_g.reshape(1, -1), qkv, qkv, qkv)


def _ssm_tables(a_re, a_im, log_dt, b_re, b_im, c_re, c_im, t):
    g, p = a_re.shape
    w = b_re.shape[-1]
    dt = jnp.exp(log_dt.astype(F32))[:, None]
    lam = lax.complex(jnp.minimum(a_re.astype(F32), -1e-4), a_im.astype(F32))
    lam_dt = lam * dt
    lam_bar = jnp.exp(lam_dt)
    b_bar = ((lam_bar - 1.0) / lam)[..., None] * lax.complex(b_re.astype(F32), b_im.astype(F32))
    c_cplx = lax.complex(c_re.astype(F32), c_im.astype(F32))
    tau = jnp.arange(t + 1, dtype=F32)
    pw = jnp.exp(lam_dt[:, None, :] * tau[None, :, None])
    ktab = jnp.einsum('gcp,gtp,gpd->gdct', c_cplx, pw[:, :t], b_bar).real.reshape(g, w * w, t)
    pw_rev = pw[:, t - 1 - jnp.arange(t)]
    pw_nxt = pw[:, 1:t + 1].transpose(0, 2, 1)
    bb = b_bar.transpose(0, 2, 1)
    cc = c_cplx.transpose(0, 2, 1)
    lt = pw[:, t][:, None, :]
    return (ktab, pw_rev.real, pw_rev.imag, pw_nxt.real, pw_nxt.imag, bb.real, bb.imag,
            cc.real, cc.imag, lt.real, lt.imag)


def _ssm_kernel(u_ref, k_ref, prr_ref, pri_ref, pnr_ref, pni_ref, bbr_ref, bbi_ref, ccr_ref, cci_ref,
                ltre_ref, ltim_ref, d_ref, y_ref,
                toep, ubuf, wre, wim, vre, vim, slre, slim, spre, spim, *, bsz, nc):
    gw = SSM_GROUP_WIDTH
    t = SSM_CHUNK
    r = bsz * nc

    jrow = lax.broadcasted_iota(jnp.int32, (t, t), 0)
    tcol = lax.broadcasted_iota(jnp.int32, (t, t), 1)
    causal = tcol >= jrow

    def build(d, carry):
        rows = pl.ds(pl.multiple_of(d * t, t), t)
        for c in range(gw):
            krow = k_ref[0, pl.ds(d * gw + c, 1), :]
            blk = pltpu.roll(jnp.broadcast_to(krow, (t, t)), 0, 1, stride=1, stride_axis=0)
            toep[rows, c * t:(c + 1) * t] = jnp.where(causal, blk, 0.0).astype(BF16)
        return carry

    lax.fori_loop(0, gw, build, 0)

    for d in range(gw):
        ubuf[:, d * t:(d + 1) * t] = u_ref[:, d].reshape(r, t).astype(BF16)
    u = ubuf[...]

    prr, pri = prr_ref[0], pri_ref[0]
    for d in range(gw):
        br, bi = bbr_ref[0, d:d + 1, :], bbi_ref[0, d:d + 1, :]
        wre[d * t:(d + 1) * t, :] = (prr * br - pri * bi).astype(BF16)
        wim[d * t:(d + 1) * t, :] = (prr * bi + pri * br).astype(BF16)
    pnr, pni = pnr_ref[0], pni_ref[0]
    for c in range(gw):
        cr, ci = ccr_ref[0, :, c:c + 1], cci_ref[0, :, c:c + 1]
        vre[:, c * t:(c + 1) * t] = (cr * pnr - ci * pni).astype(BF16)
        vim[:, c * t:(c + 1) * t] = (-(cr * pni + ci * pnr)).astype(BF16)

    slre[...] = jnp.dot(u, wre[...], preferred_element_type=F32)
    slim[...] = jnp.dot(u, wim[...], preferred_element_type=F32)
    a = ltre_ref[0]
    b = ltim_ref[0]

    def step(c, carry):
        sre, sim = carry
        rows = pl.ds(c, bsz, stride=nc)
        spre[rows, :] = sre
        spim[rows, :] = sim
        nre = a * sre - b * sim + slre[rows, :]
        nim = a * sim + b * sre + slim[rows, :]
        return nre, nim

    z = jnp.zeros((bsz, a.shape[-1]), F32)
    lax.fori_loop(0, nc, step, (z, z))
    y = jnp.dot(u, toep[...], preferred_element_type=F32)
    y = y + jnp.dot(spre[...].astype(BF16), vre[...], preferred_element_type=F32)
    y = y + jnp.dot(spim[...].astype(BF16), vim[...], preferred_element_type=F32)
    for c in range(gw):
        y_ref[:, c] = (y[:, c * t:(c + 1) * t].reshape(bsz, nc, t)
                       + d_ref[0, c:c + 1, :] * u_ref[:, c])


def _ssm(ut4, tables, d_skip):
    bsz, d_ssm, nc, t = ut4.shape
    d_tab = jnp.broadcast_to(d_skip.reshape(-1, SSM_GROUP_WIDTH, 1), (d_skip.size // SSM_GROUP_WIDTH,
                                                                    SSM_GROUP_WIDTH, t))
    gw = SSM_GROUP_WIDTH
    g = d_ssm // gw
    p = tables[-1].shape[-1]
    r = bsz * nc
    blk = lambda a, b: pl.BlockSpec((1, a, b), lambda i: (i, 0, 0))
    grp = pl.BlockSpec((bsz, gw, nc, t), lambda i: (0, i, 0, 0))
    return pl.pallas_call(
        functools.partial(_ssm_kernel, bsz=bsz, nc=nc),
        grid=(g,),
        in_specs=[grp, blk(gw * gw, t), blk(t, p), blk(t, p), blk(p, t), blk(p, t),
                  blk(gw, p), blk(gw, p), blk(p, gw), blk(p, gw), blk(1, p), blk(1, p), blk(gw, t)],
        out_specs=grp,
        out_shape=jax.ShapeDtypeStruct(ut4.shape, F32),
        scratch_shapes=[pltpu.VMEM((gw * t, gw * t), BF16), pltpu.VMEM((r, gw * t), BF16),
                        pltpu.VMEM((gw * t, p), BF16), pltpu.VMEM((gw * t, p), BF16),
                        pltpu.VMEM((p, gw * t), BF16), pltpu.VMEM((p, gw * t), BF16)]
        + [pltpu.VMEM((r, p), F32)] * 4,
        compiler_params=_cparams(("arbitrary",), VMEM_LIMIT),
        name="ssm",
    )(ut4, *tables, d_tab)


def _mix_kernel(x_ref, attn_ref, yt_ref, mod_ref, wglut_ref, bglu_ref, wout_ref,
                g2_ref, wrt_ref, br_ref, x1_ref, h2_ref, lgt_ref, *, d_attn):
    nck = yt_ref.shape[2]
    y = jnp.concatenate([yt_ref[0, :, c, :] for c in range(nck)], axis=1)
    z = 0.5 * y * (1.0 + jnp.tanh(math.sqrt(2.0 / math.pi) * (y + 0.044715 * (y * y * y))))
    gl = jnp.dot(wglut_ref[...], z.astype(BF16), preferred_element_type=F32) + bglu_ref[...]
    so = (z * _sigmoid(gl)).T
    mix = (jnp.dot(attn_ref[0], wout_ref[:d_attn, :], preferred_element_type=F32)
           + jnp.dot(so.astype(BF16), wout_ref[d_attn:, :], preferred_element_type=F32))
    x1 = x_ref[0] + mod_ref[0, 2:3, :] * mix
    x1_ref[0] = x1
    h2 = _rms(x1, g2_ref[...]) * (1.0 + mod_ref[0, 4:5, :]) + mod_ref[0, 3:4, :]
    for j in range(SUBLANES):
        h2_ref[pl.ds(j, h2.shape[0], stride=SUBLANES), :] = h2[:, j * LANES:(j + 1) * LANES]
    lgt_ref[...] = lax.dot_general(wrt_ref[...], h2, NT_DIMS, preferred_element_type=F32,
                                   precision=HIGHEST) + br_ref[...]


def _mix(x, attn, yt4, mod3, wglu_t, bglu, wout, g2, wr, br, tm=1024):
    bsz, s, d = x.shape
    tm = min(tm, s)
    da = attn.shape[-1]
    ds_ = yt4.shape[1]
    e = wr.shape[-1]
    nt = s // tm
    tok = lambda n: pl.BlockSpec((1, tm, n), lambda b, i: (b, i, 0))
    chan = pl.BlockSpec((1, ds_, tm // SSM_CHUNK, SSM_CHUNK), lambda b, i: (b, 0, i, 0))
    full = lambda a, b_: pl.BlockSpec((a, b_), lambda b, i: (0, 0))
    return pl.pallas_call(
        functools.partial(_mix_kernel, d_attn=da),
        grid=(bsz, nt),
        in_specs=[tok(d), tok(da), chan,
                  pl.BlockSpec((1, 6, d), lambda b, i: (b, 0, 0)),
                  full(ds_, ds_), full(ds_, 1), full(da + ds_, d),
                  full(1, d), full(e, d), full(e, 1)],
        out_specs=[tok(d), pl.BlockSpec((tm * SUBLANES, LANES), lambda b, i: (b * nt + i, 0)),
                   pl.BlockSpec((e, tm), lambda b, i: (0, b * nt + i))],
        out_shape=[jax.ShapeDtypeStruct((bsz, s, d), F32),
                   jax.ShapeDtypeStruct((bsz * s * SUBLANES, LANES), F32),
                   jax.ShapeDtypeStruct((e, bsz * s), F32)],
        compiler_params=_cparams(("arbitrary", "arbitrary"), VMEM_LIMIT),
        name="mix",
    )(x, attn, yt4, mod3, wglu_t, bglu.reshape(-1, 1), wout,
      g2.reshape(1, -1), wr.T, br.reshape(-1, 1))


def _route_kernel(lg_ref, idx_ref, gate_ref, rank_ref, cnt_ref, run_ref, *, tm):
    i = pl.program_id(0)

    @pl.when(i == 0)
    def _():
        run_ref[...] = jnp.zeros_like(run_ref)

    l = lg_ref[...]
    e = l.shape[0]
    sub = lax.broadcasted_iota(jnp.int32, l.shape, 0)
    vals, sels, idxs = [], [], []
    for _k in range(TOP_K):
        mx = jnp.max(l, axis=0, keepdims=True)
        ix = jnp.min(jnp.where(l == mx, sub, e), axis=0, keepdims=True)
        sel = sub == ix
        vals.append(mx)
        idxs.append(ix)
        sels.append(sel)
        l = jnp.where(sel, -jnp.inf, l)
    ex = [jnp.exp(v - vals[0]) for v in vals]
    den = ex[0] + ex[1] + ex[2] + ex[3]
    chosen = jnp.zeros(l.shape, F32)
    for sel in sels:
        chosen = chosen + jnp.where(sel, 1.0, 0.0)
    r_i = lax.broadcasted_iota(jnp.int32, (tm, tm), 0)
    c_i = lax.broadcasted_iota(jnp.int32, (tm, tm), 1)
    tri = jnp.where(r_i < c_i, 1.0, 0.0).astype(BF16)
    before = jnp.dot(chosen.astype(BF16), tri, preferred_element_type=F32) + run_ref[...]
    ksub = lax.broadcasted_iota(jnp.int32, (TOP_K, tm), 0)
    idx_o = jnp.zeros((TOP_K, tm), jnp.int32)
    gate_o = jnp.zeros((TOP_K, tm), F32)
    rank_o = jnp.zeros((TOP_K, tm), F32)
    for k in range(TOP_K):
        rk = jnp.sum(jnp.where(sels[k], before, 0.0), axis=0, keepdims=True)
        idx_o = jnp.where(ksub == k, idxs[k], idx_o)
        gate_o = jnp.where(ksub == k, ex[k] / den, gate_o)
        rank_o = jnp.where(ksub == k, rk, rank_o)
    idx_ref[...] = idx_o
    gate_ref[...] = gate_o
    rank_ref[...] = rank_o.astype(jnp.int32)
    run_ref[...] = run_ref[...] + jnp.sum(chosen, axis=1, keepdims=True)
    cnt_ref[...] = run_ref[...].astype(jnp.int32)


def _route(logits_t, tm=512):
    e, n = logits_t.shape
    tok = lambda w: pl.BlockSpec((w, tm), lambda i: (0, i))
    return pl.pallas_call(
        functools.partial(_route_kernel, tm=tm),
        grid=(n // tm,),
        in_specs=[tok(e)],
        out_specs=[tok(TOP_K), tok(TOP_K), tok(TOP_K), pl.BlockSpec((e, 1), lambda i: (0, 0))],
        out_shape=[jax.ShapeDtypeStruct((TOP_K, n), jnp.int32),
                   jax.ShapeDtypeStruct((TOP_K, n), F32),
                   jax.ShapeDtypeStruct((TOP_K, n), jnp.int32),
                   jax.ShapeDtypeStruct((e, 1), jnp.int32)],
        scratch_shapes=[pltpu.VMEM((e, 1), F32)],
        compiler_params=_cparams(("arbitrary",)),
        name="route",
    )(logits_t)


def _dispatch_kernel(pend_ref, padded_ref, nb_ref, pos_ref, h_ref, xs_ref, zero_ref, sem, zsem,
                     *, tm, rb, n_exp, nb_max):
    i = pl.program_id(0)

    def zero_block(blk_start):
        rows = pl.ds(pl.multiple_of(blk_start * SUBLANES, rb * SUBLANES), rb * SUBLANES)
        cp = pltpu.make_async_copy(zero_ref, xs_ref.at[rows, :], zsem)
        cp.start()
        cp.wait()

    @pl.when(i == 0)
    def _():
        zero_ref[...] = jnp.zeros_like(zero_ref)

    @pl.when(i < n_exp)
    def _():
        e = jnp.minimum(i, n_exp - 1)

        @pl.when(padded_ref[e] > 0)
        def _():
            zero_block(pend_ref[e] - rb)

        @pl.when(nb_ref[0] + e < nb_max)
        def _():
            zero_block((nb_ref[0] + e) * rb)

    @pl.when(i >= n_exp)
    def _():
        def issue(t, _):
            for k in range(TOP_K):
                p = pos_ref[0, 0, k * tm + t]
                pltpu.make_async_copy(h_ref.at[_tile_rows(t), :], xs_ref.at[_tile_rows(p), :],
                                      sem).start(priority=k % 2)
            return 0

        lax.fori_loop(0, tm, issue, 0, unroll=4)
        for k in range(TOP_K):
            pltpu.make_async_copy(h_ref, xs_ref.at[pl.ds(0, tm * SUBLANES), :], sem).wait()


def _dispatch(pend, padded, nblk, pos3, h2, n_rows, tm, rb):
    n = h2.shape[0] // SUBLANES
    n_exp = pend.shape[0]
    tile = lambda i, *_: (jnp.maximum(i - n_exp, 0), 0)
    return pl.pallas_call(
        functools.partial(_dispatch_kernel, tm=tm, rb=rb, n_exp=n_exp, nb_max=n_rows // rb),
        grid_spec=pltpu.PrefetchScalarGridSpec(
            num_scalar_prefetch=3,
            grid=(n_exp + n // tm,),
            in_specs=[pl.BlockSpec((1, 1, tm * TOP_K), lambda i, *_: tile(i) + (0,),
                                   memory_space=pltpu.SMEM),
                      pl.BlockSpec((tm * SUBLANES, LANES), tile)],
            out_specs=pl.BlockSpec(memory_space=pl.ANY),
            scratch_shapes=[pltpu.VMEM((rb * SUBLANES, LANES), F32), pltpu.SemaphoreType.DMA,
                            pltpu.SemaphoreType.DMA]),
        out_shape=jax.ShapeDtypeStruct((n_rows * SUBLANES, LANES), F32),
        compiler_params=_cparams(("arbitrary",)),
        name="dispatch",
    )(pend, padded, nblk, pos3, h2)


def _experts_kernel(be_ref, xb_ref, nb_ref, x_ref, w1_ref, b1_ref, w2_ref, b2_ref, y_ref,
                    w1b, w2b, *, f):
    i = pl.program_id(0)
    prev = be_ref[jnp.maximum(i - 1, 0)]
    live = i < nb_ref[0]

    @pl.when(live & ((i == 0) | (be_ref[i] != prev)))
    def _():
        w1b[...] = w1_ref[0].astype(BF16)
        w2b[...] = w2_ref[0].astype(BF16)

    @pl.when(live)
    def _():
        rb = x_ref.shape[0] // SUBLANES
        x = jnp.concatenate([x_ref[pl.ds(j, rb, stride=SUBLANES), :].astype(BF16)
                             for j in range(SUBLANES)], axis=1)
        gu = jnp.dot(x, w1b[...], preferred_element_type=F32) + b1_ref[0]
        gate = jnp.minimum(gu[:, :f], SWIGLU_LIMIT)
        up = jnp.clip(gu[:, f:], -SWIGLU_LIMIT, SWIGLU_LIMIT)
        glu = gate * _sigmoid(SWIGLU_ALPHA * gate)
        hmid = ((up + 1.0) * glu).astype(BF16)
        y = jnp.dot(hmid, w2b[...], preferred_element_type=F32) + b2_ref[0]
        for j in range(SUBLANES):
            y_ref[pl.ds(j, rb, stride=SUBLANES), :] = y[:, j * LANES:(j + 1) * LANES]

    @pl.when(jnp.logical_not(live))
    def _():
        y_ref[...] = jnp.zeros_like(y_ref)


def _experts(block_e, xblk, nblk, xs, w1, b1, w2, b2, rb):
    e, d, f2 = w1.shape
    assert d == SUBLANES * LANES, "token-tile layout holds one row per (8, 128) f32 tile"
    f = f2 // 2
    nb_max = xs.shape[0] // (rb * SUBLANES)
    rows = pl.BlockSpec((rb * SUBLANES, LANES), lambda i, be, xb, nb: (xb[i], 0))
    return pl.pallas_call(
        functools.partial(_experts_kernel, f=f),
        grid_spec=pltpu.PrefetchScalarGridSpec(
            num_scalar_prefetch=3,
            grid=(nb_max,),
            in_specs=[rows,
                      pl.BlockSpec((1, d, f2), lambda i, be, xb, nb: (be[i], 0, 0)),
                      pl.BlockSpec((1, 1, f2), lambda i, be, xb, nb: (be[i], 0, 0)),
                      pl.BlockSpec((1, f, d), lambda i, be, xb, nb: (be[i], 0, 0)),
                      pl.BlockSpec((1, 1, d), lambda i, be, xb, nb: (be[i], 0, 0))],
            out_specs=pl.BlockSpec((rb * SUBLANES, LANES), lambda i, be, xb, nb: (i, 0)),
            scratch_shapes=[pltpu.VMEM((d, f2), BF16), pltpu.VMEM((f, d), BF16)]),
        out_shape=jax.ShapeDtypeStruct(xs.shape, F32),
        compiler_params=_cparams(("arbitrary",), VMEM_LIMIT),
        name="experts",
    )(block_e, xblk, nblk, xs, w1, b1.reshape(e, 1, f2), w2, b2.reshape(e, 1, d))


def _combine_kernel(pos_ref, nxt_ref, gate_ref, x1_ref, mod_ref, fg_ref, ys_ref, o_ref, buf, sems,
                    *, tm):
    i = pl.program_id(0)
    n_tiles = pl.num_programs(0)
    slot = lax.rem(i, 2)

    def gather(p_ref, dst_slot):
        def issue(t, _):
            for k in range(TOP_K):
                p = p_ref[0, 0, k * tm + t]
                pltpu.make_async_copy(ys_ref.at[_tile_rows(p), :],
                                      buf.at[dst_slot, k, _tile_rows(t), :],
                                      sems.at[dst_slot]).start(priority=k % 2)
            return 0

        lax.fori_loop(0, tm, issue, 0, unroll=4)

    @pl.when(i == 0)
    def _():
        gather(pos_ref, 0)

    @pl.when(i + 1 < n_tiles)
    def _():
        gather(nxt_ref, 1 - slot)

    for k in range(TOP_K):
        pltpu.make_async_copy(ys_ref.at[pl.ds(0, tm * SUBLANES), :], buf.at[slot, k],
                              sems.at[slot]).wait()
    gate = gate_ref[...]
    pieces = []
    for j in range(SUBLANES):
        acc = gate[:, 0:1] * buf[slot, 0, pl.ds(j, tm, stride=SUBLANES), :]
        for k in range(1, TOP_K):
            acc = acc + gate[:, k:k + 1] * buf[slot, k, pl.ds(j, tm, stride=SUBLANES), :]
        pieces.append(acc)
    moe = jnp.concatenate(pieces, axis=1)
    x2 = x1_ref[...] + mod_ref[0, 5:6, :] * moe
    o_ref[...] = _rms(x2, fg_ref[...])


def _combine(pos3, gates, x1, mod3, final_g, ys, tm):
    bsz, s, d = x1.shape
    n = bsz * s
    nt = s // tm
    n_tiles = n // tm
    pos_spec = lambda f: pl.BlockSpec((1, 1, tm * TOP_K), lambda i: (f(i), 0, 0),
                                      memory_space=pltpu.SMEM)
    out = pl.pallas_call(
        functools.partial(_combine_kernel, tm=tm),
        grid=(n_tiles,),
        in_specs=[pos_spec(lambda i: i),
                  pos_spec(lambda i: jnp.minimum(i + 1, n_tiles - 1)),
                  pl.BlockSpec((tm, TOP_K), lambda i: (i, 0)),
                  pl.BlockSpec((tm, d), lambda i: (i, 0)),
                  pl.BlockSpec((1, 6, d), lambda i: (i // nt, 0, 0)),
                  pl.BlockSpec((1, d), lambda i: (0, 0)),
                  pl.BlockSpec(memory_space=pl.ANY)],
        out_specs=pl.BlockSpec((tm, d), lambda i: (i, 0)),
        out_shape=jax.ShapeDtypeStruct((n, d), F32),
        scratch_shapes=[pltpu.VMEM((2, TOP_K, tm * SUBLANES, LANES), F32),
                        pltpu.SemaphoreType.DMA((2,))],
        compiler_params=_cparams(("arbitrary",), VMEM_LIMIT),
        name="combine",
    )(pos3, pos3, gates, x1.reshape(n, d), mod3, final_g.reshape(1, d), ys)
    return out.reshape(bsz, s, d)


def _layer(x, mod3, norm1_g, w_in, lq1, lk1, lq2, lk2, subln_g, ssm_a_re, ssm_a_im, ssm_log_dt,
           ssm_b_re, ssm_b_im, ssm_c_re, ssm_c_im, ssm_d, w_glu, b_glu, w_out, norm2_g,
           w_router, b_router, w1, b1, w2, b2, final_g):
    bsz, s, d = x.shape
    n = bsz * s
    d_attn = ATTN_HEADS * V_DIM
    n_qkv = 3 * d_attn

    qkv, ut4 = _inproj(x, mod3, norm1_g, w_in[:, :n_qkv].astype(BF16),
                       w_in[:, n_qkv:].T.astype(BF16))
    attn = _attention(qkv, lq1, lk1, lq2, lk2, subln_g)
    tables = _ssm_tables(ssm_a_re, ssm_a_im, ssm_log_dt, ssm_b_re, ssm_b_im, ssm_c_re, ssm_c_im,
                         SSM_CHUNK)
    yt4 = _ssm(ut4, tables, ssm_d)
    x1, h2, logits_t = _mix(x, attn, yt4, mod3, w_glu.T.astype(BF16), b_glu,
                            w_out.astype(BF16), norm2_g, w_router, b_router)

    idx, gates, rank, counts = _route(logits_t)
    rb = 512 if n * TOP_K >= 512 * N_EXPERTS else 128
    counts = counts.reshape(N_EXPERTS)
    padded = ((counts + rb - 1) // rb) * rb
    pend = jnp.cumsum(padded).astype(jnp.int32)
    pstart = pend - padded
    eids = jnp.arange(N_EXPERTS, dtype=jnp.int32)[:, None, None]
    pos = rank + jnp.sum(jnp.where(idx[None] == eids, pstart[:, None, None], 0), axis=0)
    pos = pos.astype(jnp.int32)
    nb_max = (n * TOP_K) // rb + N_EXPERTS
    n_rows = nb_max * rb
    nblk = pend[-1] // rb
    blk_ids = jnp.minimum(jnp.arange(nb_max, dtype=jnp.int32), nblk - 1)
    block_e = jnp.minimum(jnp.sum((pend[None, :] <= (blk_ids * rb)[:, None]).astype(jnp.int32), axis=1),
                          N_EXPERTS - 1).astype(jnp.int32)
    tmd = 512
    pos3 = (pos.reshape(TOP_K, n // tmd, tmd).transpose(1, 0, 2)
            .reshape(n // tmd, 1, TOP_K * tmd))
    gates_tok = gates.T

    nblk = nblk.reshape(1).astype(jnp.int32)
    xs = _dispatch(pend, padded.astype(jnp.int32), nblk, pos3, h2, n_rows, tmd, rb)
    ys = _experts(block_e, blk_ids, nblk, xs, w1, b1, w2, b2, rb)
    return _combine(pos3, gates_tok, x1, mod3, final_g, ys, tmd)


def kernel(x, c, w_ada, b_ada, norm1_g, w_in, lq1, lk1, lq2, lk2, subln_g, ssm_a_re, ssm_a_im,
           ssm_log_dt, ssm_b_re, ssm_b_im, ssm_c_re, ssm_c_im, ssm_d, w_glu, b_glu, w_out, norm2_g,
           w_router, b_router, w1, b1, w2, b2, final_g):
    assert w_ada.shape[0] == 1, "single-layer block"
    bsz, s, d = x.shape
    mod3 = _adaln(c, w_ada[0], b_ada[0]).reshape(bsz, 6, d)
    return _layer(x, mod3, norm1_g[0], w_in[0], lq1[0], lk1[0], lq2[0], lk2[0], subln_g[0],
                  ssm_a_re[0], ssm_a_im[0], ssm_log_dt[0], ssm_b_re[0], ssm_b_im[0], ssm_c_re[0],
                  ssm_c_im[0], ssm_d[0], w_glu[0], b_glu[0], w_out[0], norm2_g[0], w_router[0],
                  b_router[0], w1[0], b1[0], w2[0], b2[0], final_g)
```

```python
import functools
import math

import jax
import jax.numpy as jnp
from jax import lax
from jax.experimental import pallas as pl
from jax.experimental.pallas import tpu as pltpu

F32 = jnp.float32
BF16 = jnp.bfloat16
HIGHEST = lax.Precision.HIGHEST

RMS_EPS = 1e-6
MASK_VALUE = -1e30
ATTN_HEADS = 4
QK_DIM = 64
V_DIM = 128
SSM_GROUP_WIDTH = 16
SSM_CHUNK = 128
N_EXPERTS = 32
TOP_K = 4
SWIGLU_LIMIT = 7.0
SWIGLU_ALPHA = 1.702
LAMBDA_INIT = 0.8 - 0.6 * math.exp(-0.3 * 0)

VMEM_LIMIT = 56 * 1024 * 1024
NT_DIMS = (((1,), (1,)), ((), ()))
LANES = 128
SUBLANES = 8


def _tile_rows(t):
    return pl.ds(pl.multiple_of(t * SUBLANES, SUBLANES), SUBLANES)


def _cparams(sem, vmem=None):
    return pltpu.CompilerParams(dimension_semantics=sem, vmem_limit_bytes=vmem)


def _sigmoid(x):
    return 1.0 / (1.0 + jnp.exp(-x))


def _rms(x, g):
    ms = jnp.mean(x * x, axis=-1, keepdims=True)
    return x * lax.rsqrt(ms + RMS_EPS) * g


def _adaln_kernel(c_ref, w_ref, b_ref, o_ref):
    c = c_ref[...]
    ca = c * _sigmoid(c)
    o_ref[...] = jnp.dot(ca, w_ref[...], preferred_element_type=F32, precision=HIGHEST) + b_ref[...]


def _adaln(c, w, b):
    bsz, d = c.shape
    n = w.shape[1]
    tn = 1536
    return pl.pallas_call(
        _adaln_kernel,
        grid=(n // tn,),
        in_specs=[pl.BlockSpec((bsz, d), lambda j: (0, 0)),
                  pl.BlockSpec((d, tn), lambda j: (0, j)),
                  pl.BlockSpec((1, tn), lambda j: (0, j))],
        out_specs=pl.BlockSpec((bsz, tn), lambda j: (0, j)),
        out_shape=jax.ShapeDtypeStruct((bsz, n), F32),
        compiler_params=_cparams(("arbitrary",)),
        name="adaln",
    )(c, w, b.reshape(1, n))


def _inproj_kernel(x_ref, mod_ref, g_ref, wq_ref, wut_ref, qkv_ref, ut_ref):
    x = x_ref[0]
    y = _rms(x, g_ref[...])
    h = (y * (1.0 + mod_ref[0, 1:2, :]) + mod_ref[0, 0:1, :]).astype(BF16)
    qkv_ref[0] = jnp.dot(h, wq_ref[...], preferred_element_type=F32).astype(BF16)
    ut = lax.dot_general(wut_ref[...], h, NT_DIMS, preferred_element_type=F32)
    for c in range(ut_ref.shape[2]):
        ut_ref[0, :, c, :] = ut[:, c * SSM_CHUNK:(c + 1) * SSM_CHUNK]


def _inproj(x, mod3, g, wq_bf16, wut_bf16, tm=1024):
    bsz, s, d = x.shape
    tm = min(tm, s)
    n_qkv = wq_bf16.shape[1]
    n_u = wut_bf16.shape[0]
    nck = tm // SSM_CHUNK
    return pl.pallas_call(
        _inproj_kernel,
        grid=(bsz, s // tm),
        in_specs=[pl.BlockSpec((1, tm, d), lambda b, i: (b, i, 0)),
                  pl.BlockSpec((1, 6, d), lambda b, i: (b, 0, 0)),
                  pl.BlockSpec((1, d), lambda b, i: (0, 0)),
                  pl.BlockSpec((d, n_qkv), lambda b, i: (0, 0)),
                  pl.BlockSpec((n_u, d), lambda b, i: (0, 0))],
        out_specs=[pl.BlockSpec((1, tm, n_qkv), lambda b, i: (b, i, 0)),
                   pl.BlockSpec((1, n_u, nck, SSM_CHUNK), lambda b, i: (b, 0, i, 0))],
        out_shape=[jax.ShapeDtypeStruct((bsz, s, n_qkv), BF16),
                   jax.ShapeDtypeStruct((bsz, n_u, s // SSM_CHUNK, SSM_CHUNK), F32)],
        compiler_params=_cparams(("arbitrary", "arbitrary"), VMEM_LIMIT),
        name="inproj",
    )(x, mod3, g.reshape(1, d), wq_bf16, wut_bf16)


def _attn_kernel(lq1_ref, lk1_ref, lq2_ref, lk2_ref, sg_ref, q_ref, k_ref, v_ref, o_ref,
                 vt_ref, m_ref, acc_ref, p_ref, al_ref, *, tq):
    qi = pl.program_id(2)
    n_acc = acc_ref.shape[0]

    @pl.when(qi == 0)
    def _():
        vt_ref[:V_DIM, :] = v_ref[0].astype(F32).T.astype(BF16)
        vt_ref[V_DIM:, :] = jnp.ones((n_acc - V_DIM, vt_ref.shape[1]), BF16)

    lam = (jnp.exp(jnp.sum(lq1_ref[...] * lk1_ref[...], axis=-1, keepdims=True))
           - jnp.exp(jnp.sum(lq2_ref[...] * lk2_ref[...], axis=-1, keepdims=True))
           + LAMBDA_INIT)
    q = q_ref[0] * (QK_DIM ** -0.5)
    lane = lax.broadcasted_iota(jnp.int32, q.shape, 1)
    zero = jnp.zeros_like(q)
    qs = jnp.concatenate([jnp.where(lane < QK_DIM, q, zero), jnp.where(lane >= QK_DIM, q, zero)],
                         axis=0)
    m_ref[...] = jnp.full(m_ref.shape, MASK_VALUE, F32)
    acc_ref[...] = jnp.zeros(acc_ref.shape, F32)

    def softmax_stage(ki, mask):
        start = pl.multiple_of(ki * tq, tq)
        st = lax.dot_general(k_ref[0, pl.ds(start, tq), :], qs, NT_DIMS,
                             preferred_element_type=F32)
        if mask is not None:
            st = jnp.where(mask, st, MASK_VALUE)
        m_prev = m_ref[...]
        m_new = jnp.maximum(m_prev, jnp.max(st, axis=0, keepdims=True))
        m_ref[...] = m_new
        return jnp.exp(st - m_new).astype(BF16), jnp.exp(m_prev - m_new)

    def value_stage(ki, pt, alpha):
        start = pl.multiple_of(ki * tq, tq)
        acc_ref[...] = alpha * acc_ref[...] + jnp.dot(vt_ref[:, pl.ds(start, tq)], pt,
                                                      preferred_element_type=F32)

    key = lax.broadcasted_iota(jnp.int32, (tq, 2 * tq), 0)
    col = lax.broadcasted_iota(jnp.int32, (tq, 2 * tq), 1)
    causal = key <= jnp.where(col >= tq, col - tq, col)

    @pl.when(qi == 0)
    def _():
        value_stage(0, *softmax_stage(0, causal))

    @pl.when(qi > 0)
    def _():
        p_ref[...], al_ref[...] = softmax_stage(0, None)

        def body(ki, carry):
            pt_prev = p_ref[...]
            al_prev = al_ref[...]
            pt, alpha = softmax_stage(ki, None)
            value_stage(ki - 1, pt_prev, al_prev)
            p_ref[...] = pt
            al_ref[...] = alpha
            return carry

        lax.fori_loop(1, qi, body, 0)
        pt_prev = p_ref[...]
        al_prev = al_ref[...]
        pt, alpha = softmax_stage(qi, causal)
        value_stage(qi - 1, pt_prev, al_prev)
        value_stage(qi, pt, alpha)

    a = acc_ref[...]

    ot = (a[:V_DIM, :tq] / a[V_DIM:V_DIM + 1, :tq]
          - lam * (a[:V_DIM, tq:] / a[V_DIM:V_DIM + 1, tq:]))
    o = _rms(ot.T, sg_ref[...]) * (1.0 - LAMBDA_INIT)
    o_ref[0] = o.astype(BF16)


def _attention(qkv, lq1, lk1, lq2, lk2, subln_g, tq=1024):
    bsz, s, _ = qkv.shape
    tq = min(tq, s)
    h = ATTN_HEADS
    n_acc = V_DIM + 8
    vec = lambda n: pl.BlockSpec((1, n), lambda b, hh, i: (0, 0))
    return pl.pallas_call(
        functools.partial(_attn_kernel, tq=tq),
        grid=(bsz, h, s // tq),
        in_specs=[vec(QK_DIM), vec(QK_DIM), vec(QK_DIM), vec(QK_DIM), vec(V_DIM),
                  pl.BlockSpec((1, tq, V_DIM), lambda b, hh, i: (b, i, hh)),
                  pl.BlockSpec((1, s, V_DIM), lambda b, hh, i: (b, 0, h + hh)),
                  pl.BlockSpec((1, s, V_DIM), lambda b, hh, i: (b, 0, 2 * h + hh))],
        out_specs=pl.BlockSpec((1, tq, V_DIM), lambda b, hh, i: (b, i, hh)),
        out_shape=jax.ShapeDtypeStruct((bsz, s, h * V_DIM), BF16),
        scratch_shapes=[pltpu.VMEM((n_acc, s), BF16),
                        pltpu.VMEM((1, 2 * tq), F32), pltpu.VMEM((n_acc, 2 * tq), F32),
                        pltpu.VMEM((tq, 2 * tq), BF16), pltpu.VMEM((1, 2 * tq), F32)],
        compiler_params=_cparams(("arbitrary", "arbitrary", "arbitrary"), VMEM_LIMIT),
        name="attention",
    )(lq1.reshape(1, -1), lk1.reshape(1, -1), lq2.reshape(1, -1), lk2.reshape(1, -1),
      subln_g.reshape(1, -1), qkv, qkv, qkv)


def _ssm_tables(a_re, a_im, log_dt, b_re, b_im, c_re, c_im, t):
    g, p = a_re.shape
    w = b_re.shape[-1]
    dt = jnp.exp(log_dt.astype(F32))[:, None]
    lam = lax.complex(jnp.minimum(a_re.astype(F32), -1e-4), a_im.astype(F32))
    lam_dt = lam * dt
    lam_bar = jnp.exp(lam_dt)
    b_bar = ((lam_bar - 1.0) / lam)[..., None] * lax.complex(b_re.astype(F32), b_im.astype(F32))
    c_cplx = lax.complex(c_re.astype(F32), c_im.astype(F32))
    tau = jnp.arange(t + 1, dtype=F32)
    pw = jnp.exp(lam_dt[:, None, :] * tau[None, :, None])
    ktab = jnp.einsum('gcp,gtp,gpd->gdct', c_cplx, pw[:, :t], b_bar).real.reshape(g, w * w, t)
    pw_rev = pw[:, t - 1 - jnp.arange(t)]
    pw_nxt = pw[:, 1:t + 1].transpose(0, 2, 1)
    bb = b_bar.transpose(0, 2, 1)
    cc = c_cplx.transpose(0, 2, 1)
    lt = pw[:, t][:, None, :]
    return (ktab, pw_rev.real, pw_rev.imag, pw_nxt.real, pw_nxt.imag, bb.real, bb.imag,
            cc.real, cc.imag, lt.real, lt.imag)


def _ssm_kernel(u_ref, k_ref, prr_ref, pri_ref, pnr_ref, pni_ref, bbr_ref, bbi_ref, ccr_ref, cci_ref,
                ltre_ref, ltim_ref, d_ref, y_ref,
                toep, ubuf, wre, wim, vre, vim, slre, slim, spre, spim, *, bsz, nc):
    gw = SSM_GROUP_WIDTH
    t = SSM_CHUNK
    r = bsz * nc

    jrow = lax.broadcasted_iota(jnp.int32, (t, t), 0)
    tcol = lax.broadcasted_iota(jnp.int32, (t, t), 1)
    causal = tcol >= jrow

    def build(d, carry):
        rows = pl.ds(pl.multiple_of(d * t, t), t)
        for c in range(gw):
            krow = k_ref[0, pl.ds(d * gw + c, 1), :]
            blk = pltpu.roll(jnp.broadcast_to(krow, (t, t)), 0, 1, stride=1, stride_axis=0)
            toep[rows, c * t:(c + 1) * t] = jnp.where(causal, blk, 0.0).astype(BF16)
        return carry

    lax.fori_loop(0, gw, build, 0)

    for d in range(gw):
        ubuf[:, d * t:(d + 1) * t] = u_ref[:, d].reshape(r, t).astype(BF16)
    u = ubuf[...]

    prr, pri = prr_ref[0], pri_ref[0]
    for d in range(gw):
        br, bi = bbr_ref[0, d:d + 1, :], bbi_ref[0, d:d + 1, :]
        wre[d * t:(d + 1) * t, :] = (prr * br - pri * bi).astype(BF16)
        wim[d * t:(d + 1) * t, :] = (prr * bi + pri * br).astype(BF16)
    pnr, pni = pnr_ref[0], pni_ref[0]
    for c in range(gw):
        cr, ci = ccr_ref[0, :, c:c + 1], cci_ref[0, :, c:c + 1]
        vre[:, c * t:(c + 1) * t] = (cr * pnr - ci * pni).astype(BF16)
        vim[:, c * t:(c + 1) * t] = (-(cr * pni + ci * pnr)).astype(BF16)

    slre[...] = jnp.dot(u, wre[...], preferred_element_type=F32)
    slim[...] = jnp.dot(u, wim[...], preferred_element_type=F32)
    a = ltre_ref[0]
    b = ltim_ref[0]

    def step(c, carry):
        sre, sim = carry
        rows = pl.ds(c, bsz, stride=nc)
        spre[rows, :] = sre
        spim[rows, :] = sim
        nre = a * sre - b * sim + slre[rows, :]
        nim = a * sim + b * sre + slim[rows, :]
        return nre, nim

    z = jnp.zeros((bsz, a.shape[-1]), F32)
    lax.fori_loop(0, nc, step, (z, z))
    y = jnp.dot(u, toep[...], preferred_element_type=F32)
    y = y + jnp.dot(spre[...].astype(BF16), vre[...], preferred_element_type=F32)
    y = y + jnp.dot(spim[...].astype(BF16), vim[...], preferred_element_type=F32)
    for c in range(gw):
        y_ref[:, c] = (y[:, c * t:(c + 1) * t].reshape(bsz, nc, t)
                       + d_ref[0, c:c + 1, :] * u_ref[:, c])


def _ssm(ut4, tables, d_skip):
    bsz, d_ssm, nc, t = ut4.shape
    d_tab = jnp.broadcast_to(d_skip.reshape(-1, SSM_GROUP_WIDTH, 1), (d_skip.size // SSM_GROUP_WIDTH,
                                                                    SSM_GROUP_WIDTH, t))
    gw = SSM_GROUP_WIDTH
    g = d_ssm // gw
    p = tables[-1].shape[-1]
    r = bsz * nc
    blk = lambda a, b: pl.BlockSpec((1, a, b), lambda i: (i, 0, 0))
    grp = pl.BlockSpec((bsz, gw, nc, t), lambda i: (0, i, 0, 0))
    return pl.pallas_call(
        functools.partial(_ssm_kernel, bsz=bsz, nc=nc),
        grid=(g,),
        in_specs=[grp, blk(gw * gw, t), blk(t, p), blk(t, p), blk(p, t), blk(p, t),
                  blk(gw, p), blk(gw, p), blk(p, gw), blk(p, gw), blk(1, p), blk(1, p), blk(gw, t)],
        out_specs=grp,
        out_shape=jax.ShapeDtypeStruct(ut4.shape, F32),
        scratch_shapes=[pltpu.VMEM((gw * t, gw * t), BF16), pltpu.VMEM((r, gw * t), BF16),
                        pltpu.VMEM((gw * t, p), BF16), pltpu.VMEM((gw * t, p), BF16),
                        pltpu.VMEM((p, gw * t), BF16), pltpu.VMEM((p, gw * t), BF16)]
        + [pltpu.VMEM((r, p), F32)] * 4,
        compiler_params=_cparams(("arbitrary",), VMEM_LIMIT),
        name="ssm",
    )(ut4, *tables, d_tab)


def _mix_kernel(x_ref, attn_ref, yt_ref, mod_ref, wglut_ref, bglu_ref, wout_ref,
                g2_ref, wrt_ref, br_ref, x1_ref, h2_ref, lgt_ref, *, d_attn):
    nck = yt_ref.shape[2]
    y = jnp.concatenate([yt_ref[0, :, c, :] for c in range(nck)], axis=1)
    z = 0.5 * y * (1.0 + jnp.tanh(math.sqrt(2.0 / math.pi) * (y + 0.044715 * (y * y * y))))
    gl = jnp.dot(wglut_ref[...], z.astype(BF16), preferred_element_type=F32) + bglu_ref[...]
    so = (z * _sigmoid(gl)).T
    mix = (jnp.dot(attn_ref[0], wout_ref[:d_attn, :], preferred_element_type=F32)
           + jnp.dot(so.astype(BF16), wout_ref[d_attn:, :], preferred_element_type=F32))
    x1 = x_ref[0] + mod_ref[0, 2:3, :] * mix
    x1_ref[0] = x1
    h2 = _rms(x1, g2_ref[...]) * (1.0 + mod_ref[0, 4:5, :]) + mod_ref[0, 3:4, :]
    for j in range(SUBLANES):
        h2_ref[pl.ds(j, h2.shape[0], stride=SUBLANES), :] = h2[:, j * LANES:(j + 1) * LANES]
    lgt_ref[...] = lax.dot_general(wrt_ref[...], h2, NT_DIMS, preferred_element_type=F32,
                                   precision=HIGHEST) + br_ref[...]


def _mix(x, attn, yt4, mod3, wglu_t, bglu, wout, g2, wr, br, tm=1024):
    bsz, s, d = x.shape
    tm = min(tm, s)
    da = attn.shape[-1]
    ds_ = yt4.shape[1]
    e = wr.shape[-1]
    nt = s // tm
    tok = lambda n: pl.BlockSpec((1, tm, n), lambda b, i: (b, i, 0))
    chan = pl.BlockSpec((1, ds_, tm // SSM_CHUNK, SSM_CHUNK), lambda b, i: (b, 0, i, 0))
    full = lambda a, b_: pl.BlockSpec((a, b_), lambda b, i: (0, 0))
    return pl.pallas_call(
        functools.partial(_mix_kernel, d_attn=da),
        grid=(bsz, nt),
        in_specs=[tok(d), tok(da), chan,
                  pl.BlockSpec((1, 6, d), lambda b, i: (b, 0, 0)),
                  full(ds_, ds_), full(ds_, 1), full(da + ds_, d),
                  full(1, d), full(e, d), full(e, 1)],
        out_specs=[tok(d), pl.BlockSpec((tm * SUBLANES, LANES), lambda b, i: (b * nt + i, 0)),
                   pl.BlockSpec((e, tm), lambda b, i: (0, b * nt + i))],
        out_shape=[jax.ShapeDtypeStruct((bsz, s, d), F32),
                   jax.ShapeDtypeStruct((bsz * s * SUBLANES, LANES), F32),
                   jax.ShapeDtypeStruct((e, bsz * s), F32)],
        compiler_params=_cparams(("arbitrary", "arbitrary"), VMEM_LIMIT),
        name="mix",
    )(x, attn, yt4, mod3, wglu_t, bglu.reshape(-1, 1), wout,
      g2.reshape(1, -1), wr.T, br.reshape(-1, 1))


def _route_kernel(lg_ref, idx_ref, gate_ref, rank_ref, cnt_ref, run_ref, *, tm):
    i = pl.program_id(0)

    @pl.when(i == 0)
    def _():
        run_ref[...] = jnp.zeros_like(run_ref)

    l = lg_ref[...]
    e = l.shape[0]
    sub = lax.broadcasted_iota(jnp.int32, l.shape, 0)
    vals, sels, idxs = [], [], []
    for _k in range(TOP_K):
        mx = jnp.max(l, axis=0, keepdims=True)
        ix = jnp.min(jnp.where(l == mx, sub, e), axis=0, keepdims=True)
        sel = sub == ix
        vals.append(mx)
        idxs.append(ix)
        sels.append(sel)
        l = jnp.where(sel, -jnp.inf, l)
    ex = [jnp.exp(v - vals[0]) for v in vals]
    den = ex[0] + ex[1] + ex[2] + ex[3]
    chosen = jnp.zeros(l.shape, F32)
    for sel in sels:
        chosen = chosen + jnp.where(sel, 1.0, 0.0)
    r_i = lax.broadcasted_iota(jnp.int32, (tm, tm), 0)
    c_i = lax.broadcasted_iota(jnp.int32, (tm, tm), 1)
    tri = jnp.where(r_i < c_i, 1.0, 0.0).astype(BF16)
    before = jnp.dot(chosen.astype(BF16), tri, preferred_element_type=F32) + run_ref[...]
    ksub = lax.broadcasted_iota(jnp.int32, (TOP_K, tm), 0)
    idx_o = jnp.zeros((TOP_K, tm), jnp.int32)
    gate_o = jnp.zeros((TOP_K, tm), F32)
    rank_o = jnp.zeros((TOP_K, tm), F32)
    for k in range(TOP_K):
        rk = jnp.sum(jnp.where(sels[k], before, 0.0), axis=0, keepdims=True)
        idx_o = jnp.where(ksub == k, idxs[k], idx_o)
        gate_o = jnp.where(ksub == k, ex[k] / den, gate_o)
        rank_o = jnp.where(ksub == k, rk, rank_o)
    idx_ref[...] = idx_o
    gate_ref[...] = gate_o
    rank_ref[...] = rank_o.astype(jnp.int32)
    run_ref[...] = run_ref[...] + jnp.sum(chosen, axis=1, keepdims=True)
    cnt_ref[...] = run_ref[...].astype(jnp.int32)


def _route(logits_t, tm=512):
    e, n = logits_t.shape
    tok = lambda w: pl.BlockSpec((w, tm), lambda i: (0, i))
    return pl.pallas_call(
        functools.partial(_route_kernel, tm=tm),
        grid=(n // tm,),
        in_specs=[tok(e)],
        out_specs=[tok(TOP_K), tok(TOP_K), tok(TOP_K), pl.BlockSpec((e, 1), lambda i: (0, 0))],
        out_shape=[jax.ShapeDtypeStruct((TOP_K, n), jnp.int32),
                   jax.ShapeDtypeStruct((TOP_K, n), F32),
                   jax.ShapeDtypeStruct((TOP_K, n), jnp.int32),
                   jax.ShapeDtypeStruct((e, 1), jnp.int32)],
        scratch_shapes=[pltpu.VMEM((e, 1), F32)],
        compiler_params=_cparams(("arbitrary",)),
        name="route",
    )(logits_t)


def _dispatch_kernel(pend_ref, padded_ref, nb_ref, pos_ref, h_ref, xs_ref, zero_ref, sem, zsem,
                     *, tm, rb, n_exp, nb_max):
    i = pl.program_id(0)

    def zero_copy(blk_start):
        rows = pl.ds(pl.multiple_of(blk_start * SUBLANES, rb * SUBLANES), rb * SUBLANES)
        return pltpu.make_async_copy(zero_ref, xs_ref.at[rows, :], zsem)

    @pl.when(i == 0)
    def _():
        zero_ref[...] = jnp.zeros_like(zero_ref)
        for wait in (False, True):
            for e in range(n_exp):
                @pl.when(padded_ref[e] > 0)
                def _():
                    cp = zero_copy(pend_ref[e] - rb)
                    cp.wait() if wait else cp.start()

                @pl.when(nb_ref[0] + e < nb_max)
                def _():
                    cp = zero_copy((nb_ref[0] + e) * rb)
                    cp.wait() if wait else cp.start()

    @pl.when(i > 0)
    def _():
        def issue(t, _):
            for k in range(TOP_K):
                p = pos_ref[0, 0, k * tm + t]
                pltpu.make_async_copy(h_ref.at[_tile_rows(t), :], xs_ref.at[_tile_rows(p), :],
                                      sem).start(priority=k % 2)
            return 0

        lax.fori_loop(0, tm, issue, 0, unroll=4)
        for k in range(TOP_K):
            pltpu.make_async_copy(h_ref, xs_ref.at[pl.ds(0, tm * SUBLANES), :], sem).wait()


def _dispatch(pend, padded, nblk, pos3, h2, n_rows, tm, rb):
    n = h2.shape[0] // SUBLANES
    n_exp = pend.shape[0]
    tile = lambda i, *_: (jnp.maximum(i - 1, 0), 0)
    return pl.pallas_call(
        functools.partial(_dispatch_kernel, tm=tm, rb=rb, n_exp=n_exp, nb_max=n_rows // rb),
        grid_spec=pltpu.PrefetchScalarGridSpec(
            num_scalar_prefetch=3,
            grid=(1 + n // tm,),
            in_specs=[pl.BlockSpec((1, 1, tm * TOP_K), lambda i, *_: tile(i) + (0,),
                                   memory_space=pltpu.SMEM),
                      pl.BlockSpec((tm * SUBLANES, LANES), tile)],
            out_specs=pl.BlockSpec(memory_space=pl.ANY),
            scratch_shapes=[pltpu.VMEM((rb * SUBLANES, LANES), F32), pltpu.SemaphoreType.DMA,
                            pltpu.SemaphoreType.DMA]),
        out_shape=jax.ShapeDtypeStruct((n_rows * SUBLANES, LANES), F32),
        compiler_params=_cparams(("arbitrary",)),
        name="dispatch",
    )(pend, padded, nblk, pos3, h2)


def _experts_kernel(be_ref, xb_ref, nb_ref, x_ref, w1_ref, b1_ref, w2_ref, b2_ref, y_ref,
                    w1b, w2b, *, f):
    i = pl.program_id(0)
    prev = be_ref[jnp.maximum(i - 1, 0)]
    live = i < nb_ref[0]

    @pl.when(live & ((i == 0) | (be_ref[i] != prev)))
    def _():
        w1b[...] = w1_ref[0].astype(BF16)
        w2b[...] = w2_ref[0].astype(BF16)

    @pl.when(live)
    def _():
        rb = x_ref.shape[0] // SUBLANES
        x = jnp.concatenate([x_ref[pl.ds(j, rb, stride=SUBLANES), :].astype(BF16)
                             for j in range(SUBLANES)], axis=1)
        gu = jnp.dot(x, w1b[...], preferred_element_type=F32) + b1_ref[0]
        gate = jnp.minimum(gu[:, :f], SWIGLU_LIMIT)
        up = jnp.clip(gu[:, f:], -SWIGLU_LIMIT, SWIGLU_LIMIT)
        glu = gate * _sigmoid(SWIGLU_ALPHA * gate)
        hmid = ((up + 1.0) * glu).astype(BF16)
        y = jnp.dot(hmid, w2b[...], preferred_element_type=F32) + b2_ref[0]
        for j in range(SUBLANES):
            y_ref[pl.ds(j, rb, stride=SUBLANES), :] = y[:, j * LANES:(j + 1) * LANES]

    @pl.when(jnp.logical_not(live))
    def _():
        y_ref[...] = jnp.zeros_like(y_ref)


def _experts(block_e, xblk, nblk, xs, w1, b1, w2, b2, rb):
    e, d, f2 = w1.shape
    assert d == SUBLANES * LANES, "token-tile layout holds one row per (8, 128) f32 tile"
    f = f2 // 2
    nb_max = xs.shape[0] // (rb * SUBLANES)
    rows = pl.BlockSpec((rb * SUBLANES, LANES), lambda i, be, xb, nb: (xb[i], 0))
    return pl.pallas_call(
        functools.partial(_experts_kernel, f=f),
        grid_spec=pltpu.PrefetchScalarGridSpec(
            num_scalar_prefetch=3,
            grid=(nb_max,),
            in_specs=[rows,
                      pl.BlockSpec((1, d, f2), lambda i, be, xb, nb: (be[i], 0, 0)),
                      pl.BlockSpec((1, 1, f2), lambda i, be, xb, nb: (be[i], 0, 0)),
                      pl.BlockSpec((1, f, d), lambda i, be, xb, nb: (be[i], 0, 0)),
                      pl.BlockSpec((1, 1, d), lambda i, be, xb, nb: (be[i], 0, 0))],
            out_specs=pl.BlockSpec((rb * SUBLANES, LANES), lambda i, be, xb, nb: (i, 0)),
            scratch_shapes=[pltpu.VMEM((d, f2), BF16), pltpu.VMEM((f, d), BF16)]),
        out_shape=jax.ShapeDtypeStruct(xs.shape, F32),
        compiler_params=_cparams(("arbitrary",), VMEM_LIMIT),
        name="experts",
    )(block_e, xblk, nblk, xs, w1, b1.reshape(e, 1, f2), w2, b2.reshape(e, 1, d))


def _combine_kernel(pos_ref, nxt_ref, gate_ref, x1_ref, mod_ref, fg_ref, ys_ref, o_ref, buf, sems,
                    *, tm):
    i = pl.program_id(0)
    n_tiles = pl.num_programs(0)
    slot = lax.rem(i, 2)

    def gather(p_ref, dst_slot):
        def issue(t, _):
            for k in range(TOP_K):
                p = p_ref[0, 0, k * tm + t]
                pltpu.make_async_copy(ys_ref.at[_tile_rows(p), :],
                                      buf.at[dst_slot, k, _tile_rows(t), :],
                                      sems.at[dst_slot]).start(priority=k % 2)
            return 0

        lax.fori_loop(0, tm, issue, 0, unroll=4)

    @pl.when(i == 0)
    def _():
        gather(pos_ref, 0)

    @pl.when(i + 1 < n_tiles)
    def _():
        gather(nxt_ref, 1 - slot)

    for k in range(TOP_K):
        pltpu.make_async_copy(ys_ref.at[pl.ds(0, tm * SUBLANES), :], buf.at[slot, k],
                              sems.at[slot]).wait()
    gate = gate_ref[...]
    pieces = []
    for j in range(SUBLANES):
        acc = gate[:, 0:1] * buf[slot, 0, pl.ds(j, tm, stride=SUBLANES), :]
        for k in range(1, TOP_K):
            acc = acc + gate[:, k:k + 1] * buf[slot, k, pl.ds(j, tm, stride=SUBLANES), :]
        pieces.append(acc)
    moe = jnp.concatenate(pieces, axis=1)
    x2 = x1_ref[...] + mod_ref[0, 5:6, :] * moe
    o_ref[...] = _rms(x2, fg_ref[...])


def _combine(pos3, gates, x1, mod3, final_g, ys, tm):
    bsz, s, d = x1.shape
    n = bsz * s
    nt = s // tm
    n_tiles = n // tm
    pos_spec = lambda f: pl.BlockSpec((1, 1, tm * TOP_K), lambda i: (f(i), 0, 0),
                                      memory_space=pltpu.SMEM)
    out = pl.pallas_call(
        functools.partial(_combine_kernel, tm=tm),
        grid=(n_tiles,),
        in_specs=[pos_spec(lambda i: i),
                  pos_spec(lambda i: jnp.minimum(i + 1, n_tiles - 1)),
                  pl.BlockSpec((tm, TOP_K), lambda i: (i, 0)),
                  pl.BlockSpec((tm, d), lambda i: (i, 0)),
                  pl.BlockSpec((1, 6, d), lambda i: (i // nt, 0, 0)),
                  pl.BlockSpec((1, d), lambda i: (0, 0)),
                  pl.BlockSpec(memory_space=pl.ANY)],
        out_specs=pl.BlockSpec((tm, d), lambda i: (i, 0)),
        out_shape=jax.ShapeDtypeStruct((n, d), F32),
        scratch_shapes=[pltpu.VMEM((2, TOP_K, tm * SUBLANES, LANES), F32),
                        pltpu.SemaphoreType.DMA((2,))],
        compiler_params=_cparams(("arbitrary",), VMEM_LIMIT),
        name="combine",
    )(pos3, pos3, gates, x1.reshape(n, d), mod3, final_g.reshape(1, d), ys)
    return out.reshape(bsz, s, d)


def _layer(x, mod3, norm1_g, w_in, lq1, lk1, lq2, lk2, subln_g, ssm_a_re, ssm_a_im, ssm_log_dt,
           ssm_b_re, ssm_b_im, ssm_c_re, ssm_c_im, ssm_d, w_glu, b_glu, w_out, norm2_g,
           w_router, b_router, w1, b1, w2, b2, final_g):
    bsz, s, d = x.shape
    n = bsz * s
    d_attn = ATTN_HEADS * V_DIM
    n_qkv = 3 * d_attn

    qkv, ut4 = _inproj(x, mod3, norm1_g, w_in[:, :n_qkv].astype(BF16),
                       w_in[:, n_qkv:].T.astype(BF16))
    attn = _attention(qkv, lq1, lk1, lq2, lk2, subln_g)
    tables = _ssm_tables(ssm_a_re, ssm_a_im, ssm_log_dt, ssm_b_re, ssm_b_im, ssm_c_re, ssm_c_im,
                         SSM_CHUNK)
    yt4 = _ssm(ut4, tables, ssm_d)
    x1, h2, logits_t = _mix(x, attn, yt4, mod3, w_glu.T.astype(BF16), b_glu,
                            w_out.astype(BF16), norm2_g, w_router, b_router)

    idx, gates, rank, counts = _route(logits_t)
    rb = 512 if n * TOP_K >= 512 * N_EXPERTS else 128
    counts = counts.reshape(N_EXPERTS)
    padded = ((counts + rb - 1) // rb) * rb
    pend = jnp.cumsum(padded).astype(jnp.int32)
    pstart = pend - padded
    eids = jnp.arange(N_EXPERTS, dtype=jnp.int32)[:, None, None]
    pos = rank + jnp.sum(jnp.where(idx[None] == eids, pstart[:, None, None], 0), axis=0)
    pos = pos.astype(jnp.int32)
    nb_max = (n * TOP_K) // rb + N_EXPERTS
    n_rows = nb_max * rb
    nblk = pend[-1] // rb
    blk_ids = jnp.minimum(jnp.arange(nb_max, dtype=jnp.int32), nblk - 1)
    block_e = jnp.minimum(jnp.sum((pend[None, :] <= (blk_ids * rb)[:, None]).astype(jnp.int32), axis=1),
                          N_EXPERTS - 1).astype(jnp.int32)
    tmd = 512
    pos3 = (pos.reshape(TOP_K, n // tmd, tmd).transpose(1, 0, 2)
            .reshape(n // tmd, 1, TOP_K * tmd))
    gates_tok = gates.T

    nblk = nblk.reshape(1).astype(jnp.int32)
    xs = _dispatch(pend, padded.astype(jnp.int32), nblk, pos3, h2, n_rows, tmd, rb)
    ys = _experts(block_e, blk_ids, nblk, xs, w1, b1, w2, b2, rb)
    return _combine(pos3, gates_tok, x1, mod3, final_g, ys, tmd)


def kernel(x, c, w_ada, b_ada, norm1_g, w_in, lq1, lk1, lq2, lk2, subln_g, ssm_a_re, ssm_a_im,
           ssm_log_dt, ssm_b_re, ssm_b_im, ssm_c_re, ssm_c_im, ssm_d, w_glu, b_glu, w_out, norm2_g,
           w_router, b_router, w1, b1, w2, b2, final_g):
    assert w_ada.shape[0] == 1, "single-layer block"
    bsz, s, d = x.shape
    mod3 = _adaln(c, w_ada[0], b_ada[0]).reshape(bsz, 6, d)
    return _layer(x, mod3, norm1_g[0], w_in[0], lq1[0], lk1[0], lq2[0], lk2[0], subln_g[0],
                  ssm_a_re[0], ssm_a_im[0], ssm_log_dt[0], ssm_b_re[0], ssm_b_im[0], ssm_c_re[0],
                  ssm_c_im[0], ssm_d[0], w_glu[0], b_glu[0], w_out[0], norm2_g[0], w_router[0],
                  b_router[0], w1[0], b1[0], w2[0], b2[0], final_g)
```

```python
import functools
import math

import jax
import jax.numpy as jnp
from jax import lax
from jax.experimental import pallas as pl
from jax.experimental.pallas import tpu as pltpu

F32 = jnp.float32
BF16 = jnp.bfloat16
HIGHEST = lax.Precision.HIGHEST

RMS_EPS = 1e-6
MASK_VALUE = -1e30
ATTN_HEADS = 4
QK_DIM = 64
V_DIM = 128
SSM_GROUP_WIDTH = 16
SSM_CHUNK = 128
N_EXPERTS = 32
TOP_K = 4
SWIGLU_LIMIT = 7.0
SWIGLU_ALPHA = 1.702
LAMBDA_INIT = 0.8 - 0.6 * math.exp(-0.3 * 0)

VMEM_LIMIT = 56 * 1024 * 1024
NT_DIMS = (((1,), (1,)), ((), ()))
LANES = 128
SUBLANES = 8


def _tile_rows(t):
    return pl.ds(pl.multiple_of(t * SUBLANES, SUBLANES), SUBLANES)


def _cparams(sem, vmem=None):
    return pltpu.CompilerParams(dimension_semantics=sem, vmem_limit_bytes=vmem)


def _sigmoid(x):
    return 1.0 / (1.0 + jnp.exp(-x))


def _rms(x, g):
    ms = jnp.mean(x * x, axis=-1, keepdims=True)
    return x * lax.rsqrt(ms + RMS_EPS) * g


def _adaln_kernel(c_ref, w_ref, b_ref, o_ref):
    c = c_ref[...]
    ca = c * _sigmoid(c)
    o_ref[...] = jnp.dot(ca, w_ref[...], preferred_element_type=F32, precision=HIGHEST) + b_ref[...]


def _adaln(c, w, b):
    bsz, d = c.shape
    n = w.shape[1]
    tn = 1536
    return pl.pallas_call(
        _adaln_kernel,
        grid=(n // tn,),
        in_specs=[pl.BlockSpec((bsz, d), lambda j: (0, 0)),
                  pl.BlockSpec((d, tn), lambda j: (0, j)),
                  pl.BlockSpec((1, tn), lambda j: (0, j))],
        out_specs=pl.BlockSpec((bsz, tn), lambda j: (0, j)),
        out_shape=jax.ShapeDtypeStruct((bsz, n), F32),
        compiler_params=_cparams(("arbitrary",)),
        name="adaln",
    )(c, w, b.reshape(1, n))


def _inproj_kernel(x_ref, mod_ref, g_ref, wq_ref, wut_ref, qkv_ref, ut_ref):
    x = x_ref[0]
    y = _rms(x, g_ref[...])
    h = (y * (1.0 + mod_ref[0, 1:2, :]) + mod_ref[0, 0:1, :]).astype(BF16)
    qkv_ref[0] = jnp.dot(h, wq_ref[...], preferred_element_type=F32).astype(BF16)
    ut = lax.dot_general(wut_ref[...], h, NT_DIMS, preferred_element_type=F32)
    ut_ref[0] = pltpu.einshape("m(cl)->mcl", ut, l=SSM_CHUNK)


def _inproj(x, mod3, g, wq_bf16, wut_bf16, tm=1024):
    bsz, s, d = x.shape
    tm = min(tm, s)
    n_qkv = wq_bf16.shape[1]
    n_u = wut_bf16.shape[0]
    nck = tm // SSM_CHUNK
    return pl.pallas_call(
        _inproj_kernel,
        grid=(bsz, s // tm),
        in_specs=[pl.BlockSpec((1, tm, d), lambda b, i: (b, i, 0)),
                  pl.BlockSpec((1, 6, d), lambda b, i: (b, 0, 0)),
                  pl.BlockSpec((1, d), lambda b, i: (0, 0)),
                  pl.BlockSpec((d, n_qkv), lambda b, i: (0, 0)),
                  pl.BlockSpec((n_u, d), lambda b, i: (0, 0))],
        out_specs=[pl.BlockSpec((1, tm, n_qkv), lambda b, i: (b, i, 0)),
                   pl.BlockSpec((1, n_u, nck, SSM_CHUNK), lambda b, i: (b, 0, i, 0))],
        out_shape=[jax.ShapeDtypeStruct((bsz, s, n_qkv), BF16),
                   jax.ShapeDtypeStruct((bsz, n_u, s // SSM_CHUNK, SSM_CHUNK), F32)],
        compiler_params=_cparams(("arbitrary", "arbitrary"), VMEM_LIMIT),
        name="inproj",
    )(x, mod3, g.reshape(1, d), wq_bf16, wut_bf16)


def _attn_kernel(lq1_ref, lk1_ref, lq2_ref, lk2_ref, sg_ref, q_ref, k_ref, v_ref, o_ref,
                 vt_ref, m_ref, acc_ref, p_ref, al_ref, *, tq):
    qi = pl.program_id(2)
    n_acc = acc_ref.shape[0]

    @pl.when(qi == 0)
    def _():
        vt_ref[:V_DIM, :] = v_ref[0].astype(F32).T.astype(BF16)
        vt_ref[V_DIM:, :] = jnp.ones((n_acc - V_DIM, vt_ref.shape[1]), BF16)

    lam = (jnp.exp(jnp.sum(lq1_ref[...] * lk1_ref[...], axis=-1, keepdims=True))
           - jnp.exp(jnp.sum(lq2_ref[...] * lk2_ref[...], axis=-1, keepdims=True))
           + LAMBDA_INIT)
    q = q_ref[0] * (QK_DIM ** -0.5)
    lane = lax.broadcasted_iota(jnp.int32, q.shape, 1)
    zero = jnp.zeros_like(q)
    qs = jnp.concatenate([jnp.where(lane < QK_DIM, q, zero), jnp.where(lane >= QK_DIM, q, zero)],
                         axis=0)
    m_ref[...] = jnp.full(m_ref.shape, MASK_VALUE, F32)
    acc_ref[...] = jnp.zeros(acc_ref.shape, F32)

    def softmax_stage(ki, mask):
        start = pl.multiple_of(ki * tq, tq)
        st = lax.dot_general(k_ref[0, pl.ds(start, tq), :], qs, NT_DIMS,
                             preferred_element_type=F32)
        if mask is not None:
            st = jnp.where(mask, st, MASK_VALUE)
        m_prev = m_ref[...]
        m_new = jnp.maximum(m_prev, jnp.max(st, axis=0, keepdims=True))
        m_ref[...] = m_new
        return jnp.exp(st - m_new).astype(BF16), jnp.exp(m_prev - m_new)

    def value_stage(ki, pt, alpha):
        start = pl.multiple_of(ki * tq, tq)
        acc_ref[...] = alpha * acc_ref[...] + jnp.dot(vt_ref[:, pl.ds(start, tq)], pt,
                                                      preferred_element_type=F32)

    key = lax.broadcasted_iota(jnp.int32, (tq, 2 * tq), 0)
    col = lax.broadcasted_iota(jnp.int32, (tq, 2 * tq), 1)
    causal = key <= jnp.where(col >= tq, col - tq, col)

    @pl.when(qi == 0)
    def _():
        value_stage(0, *softmax_stage(0, causal))

    @pl.when(qi > 0)
    def _():
        p_ref[...], al_ref[...] = softmax_stage(0, None)

        def body(ki, carry):
            pt_prev = p_ref[...]
            al_prev = al_ref[...]
            pt, alpha = softmax_stage(ki, None)
            value_stage(ki - 1, pt_prev, al_prev)
            p_ref[...] = pt
            al_ref[...] = alpha
            return carry

        lax.fori_loop(1, qi, body, 0)
        pt_prev = p_ref[...]
        al_prev = al_ref[...]
        pt, alpha = softmax_stage(qi, causal)
        value_stage(qi - 1, pt_prev, al_prev)
        value_stage(qi, pt, alpha)

    a = acc_ref[...]

    ot = (a[:V_DIM, :tq] / a[V_DIM:V_DIM + 1, :tq]
          - lam * (a[:V_DIM, tq:] / a[V_DIM:V_DIM + 1, tq:]))
    o = _rms(ot.T, sg_ref[...]) * (1.0 - LAMBDA_INIT)
    o_ref[0] = o.astype(BF16)


def _attention(qkv, lq1, lk1, lq2, lk2, subln_g, tq=1024):
    bsz, s, _ = qkv.shape
    tq = min(tq, s)
    h = ATTN_HEADS
    n_acc = V_DIM + 8
    vec = lambda n: pl.BlockSpec((1, n), lambda b, hh, i: (0, 0))
    return pl.pallas_call(
        functools.partial(_attn_kernel, tq=tq),
        grid=(bsz, h, s // tq),
        in_specs=[vec(QK_DIM), vec(QK_DIM), vec(QK_DIM), vec(QK_DIM), vec(V_DIM),
                  pl.BlockSpec((1, tq, V_DIM), lambda b, hh, i: (b, i, hh)),
                  pl.BlockSpec((1, s, V_DIM), lambda b, hh, i: (b, 0, h + hh)),
                  pl.BlockSpec((1, s, V_DIM), lambda b, hh, i: (b, 0, 2 * h + hh))],
        out_specs=pl.BlockSpec((1, tq, V_DIM), lambda b, hh, i: (b, i, hh)),
        out_shape=jax.ShapeDtypeStruct((bsz, s, h * V_DIM), BF16),
        scratch_shapes=[pltpu.VMEM((n_acc, s), BF16),
                        pltpu.VMEM((1, 2 * tq), F32), pltpu.VMEM((n_acc, 2 * tq), F32),
                        pltpu.VMEM((tq, 2 * tq), BF16), pltpu.VMEM((1, 2 * tq), F32)],
        compiler_params=_cparams(("arbitrary", "arbitrary", "arbitrary"), VMEM_LIMIT),
        name="attention",
    )(lq1.reshape(1, -1), lk1.reshape(1, -1), lq2.reshape(1, -1), lk2.reshape(1, -1),
      subln_g.reshape(1, -1), qkv, qkv, qkv)


def _ssm_tables(a_re, a_im, log_dt, b_re, b_im, c_re, c_im, t):
    g, p = a_re.shape
    w = b_re.shape[-1]
    dt = jnp.exp(log_dt.astype(F32))[:, None]
    lam = lax.complex(jnp.minimum(a_re.astype(F32), -1e-4), a_im.astype(F32))
    lam_dt = lam * dt
    lam_bar = jnp.exp(lam_dt)
    b_bar = ((lam_bar - 1.0) / lam)[..., None] * lax.complex(b_re.astype(F32), b_im.astype(F32))
    c_cplx = lax.complex(c_re.astype(F32), c_im.astype(F32))
    tau = jnp.arange(t + 1, dtype=F32)
    pw = jnp.exp(lam_dt[:, None, :] * tau[None, :, None])
    ktab = jnp.einsum('gcp,gtp,gpd->gdct', c_cplx, pw[:, :t], b_bar).real.reshape(g, w * w, t)
    pw_rev = pw[:, t - 1 - jnp.arange(t)]
    pw_nxt = pw[:, 1:t + 1].transpose(0, 2, 1)
    bb = b_bar.transpose(0, 2, 1)
    cc = c_cplx.transpose(0, 2, 1)
    lt = pw[:, t][:, None, :]
    return (ktab, pw_rev.real, pw_rev.imag, pw_nxt.real, pw_nxt.imag, bb.real, bb.imag,
            cc.real, cc.imag, lt.real, lt.imag)


def _ssm_kernel(u_ref, k_ref, prr_ref, pri_ref, pnr_ref, pni_ref, bbr_ref, bbi_ref, ccr_ref, cci_ref,
                ltre_ref, ltim_ref, d_ref, y_ref,
                toep, ubuf, wre, wim, vre, vim, slre, slim, spre, spim, *, bsz, nc):
    gw = SSM_GROUP_WIDTH
    t = SSM_CHUNK
    r = bsz * nc

    jrow = lax.broadcasted_iota(jnp.int32, (t, t), 0)
    tcol = lax.broadcasted_iota(jnp.int32, (t, t), 1)
    causal = tcol >= jrow

    def build(d, carry):
        rows = pl.ds(pl.multiple_of(d * t, t), t)
        for c in range(gw):
            krow = k_ref[0, pl.ds(d * gw + c, 1), :]
            blk = pltpu.roll(jnp.broadcast_to(krow, (t, t)), 0, 1, stride=1, stride_axis=0)
            toep[rows, c * t:(c + 1) * t] = jnp.where(causal, blk, 0.0).astype(BF16)
        return carry

    lax.fori_loop(0, gw, build, 0)

    for d in range(gw):
        ubuf[:, d * t:(d + 1) * t] = u_ref[:, d].reshape(r, t).astype(BF16)
    u = ubuf[...]

    prr, pri = prr_ref[0], pri_ref[0]
    for d in range(gw):
        br, bi = bbr_ref[0, d:d + 1, :], bbi_ref[0, d:d + 1, :]
        wre[d * t:(d + 1) * t, :] = (prr * br - pri * bi).astype(BF16)
        wim[d * t:(d + 1) * t, :] = (prr * bi + pri * br).astype(BF16)
    pnr, pni = pnr_ref[0], pni_ref[0]
    for c in range(gw):
        cr, ci = ccr_ref[0, :, c:c + 1], cci_ref[0, :, c:c + 1]
        vre[:, c * t:(c + 1) * t] = (cr * pnr - ci * pni).astype(BF16)
        vim[:, c * t:(c + 1) * t] = (-(cr * pni + ci * pnr)).astype(BF16)

    slre[...] = jnp.dot(u, wre[...], preferred_element_type=F32)
    slim[...] = jnp.dot(u, wim[...], preferred_element_type=F32)
    a = ltre_ref[0]
    b = ltim_ref[0]

    def step(c, carry):
        sre, sim = carry
        rows = pl.ds(c, bsz, stride=nc)
        spre[rows, :] = sre
        spim[rows, :] = sim
        nre = a * sre - b * sim + slre[rows, :]
        nim = a * sim + b * sre + slim[rows, :]
        return nre, nim

    z = jnp.zeros((bsz, a.shape[-1]), F32)
    lax.fori_loop(0, nc, step, (z, z))
    y = jnp.dot(u, toep[...], preferred_element_type=F32)
    y = y + jnp.dot(spre[...].astype(BF16), vre[...], preferred_element_type=F32)
    y = y + jnp.dot(spim[...].astype(BF16), vim[...], preferred_element_type=F32)
    for c in range(gw):
        y_ref[:, c] = (y[:, c * t:(c + 1) * t].reshape(bsz, nc, t)
                       + d_ref[0, c:c + 1, :] * u_ref[:, c])


def _ssm(ut4, tables, d_skip):
    bsz, d_ssm, nc, t = ut4.shape
    d_tab = jnp.broadcast_to(d_skip.reshape(-1, SSM_GROUP_WIDTH, 1), (d_skip.size // SSM_GROUP_WIDTH,
                                                                    SSM_GROUP_WIDTH, t))
    gw = SSM_GROUP_WIDTH
    g = d_ssm // gw
    p = tables[-1].shape[-1]
    r = bsz * nc
    blk = lambda a, b: pl.BlockSpec((1, a, b), lambda i: (i, 0, 0))
    grp = pl.BlockSpec((bsz, gw, nc, t), lambda i: (0, i, 0, 0))
    return pl.pallas_call(
        functools.partial(_ssm_kernel, bsz=bsz, nc=nc),
        grid=(g,),
        in_specs=[grp, blk(gw * gw, t), blk(t, p), blk(t, p), blk(p, t), blk(p, t),
                  blk(gw, p), blk(gw, p), blk(p, gw), blk(p, gw), blk(1, p), blk(1, p), blk(gw, t)],
        out_specs=grp,
        out_shape=jax.ShapeDtypeStruct(ut4.shape, F32),
        scratch_shapes=[pltpu.VMEM((gw * t, gw * t), BF16), pltpu.VMEM((r, gw * t), BF16),
                        pltpu.VMEM((gw * t, p), BF16), pltpu.VMEM((gw * t, p), BF16),
                        pltpu.VMEM((p, gw * t), BF16), pltpu.VMEM((p, gw * t), BF16)]
        + [pltpu.VMEM((r, p), F32)] * 4,
        compiler_params=_cparams(("arbitrary",), VMEM_LIMIT),
        name="ssm",
    )(ut4, *tables, d_tab)


def _mix_kernel(x_ref, attn_ref, yt_ref, mod_ref, wglut_ref, bglu_ref, wout_ref,
                g2_ref, wrt_ref, br_ref, x1_ref, h2_ref, lgt_ref, *, d_attn):
    nck = yt_ref.shape[2]
    y3 = pltpu.einshape("mcl->cml", yt_ref[0])
    y = jnp.concatenate([y3[c] for c in range(nck)], axis=1)
    z = 0.5 * y * (1.0 + jnp.tanh(math.sqrt(2.0 / math.pi) * (y + 0.044715 * (y * y * y))))
    gl = jnp.dot(wglut_ref[...], z.astype(BF16), preferred_element_type=F32) + bglu_ref[...]
    so = (z * _sigmoid(gl)).T
    mix = (jnp.dot(attn_ref[0], wout_ref[:d_attn, :], preferred_element_type=F32)
           + jnp.dot(so.astype(BF16), wout_ref[d_attn:, :], preferred_element_type=F32))
    x1 = x_ref[0] + mod_ref[0, 2:3, :] * mix
    x1_ref[0] = x1
    h2 = _rms(x1, g2_ref[...]) * (1.0 + mod_ref[0, 4:5, :]) + mod_ref[0, 3:4, :]
    for j in range(SUBLANES):
        h2_ref[pl.ds(j, h2.shape[0], stride=SUBLANES), :] = h2[:, j * LANES:(j + 1) * LANES]
    lgt_ref[...] = lax.dot_general(wrt_ref[...], h2, NT_DIMS, preferred_element_type=F32,
                                   precision=HIGHEST) + br_ref[...]


def _mix(x, attn, yt4, mod3, wglu_t, bglu, wout, g2, wr, br, tm=1024):
    bsz, s, d = x.shape
    tm = min(tm, s)
    da = attn.shape[-1]
    ds_ = yt4.shape[1]
    e = wr.shape[-1]
    nt = s // tm
    tok = lambda n: pl.BlockSpec((1, tm, n), lambda b, i: (b, i, 0))
    chan = pl.BlockSpec((1, ds_, tm // SSM_CHUNK, SSM_CHUNK), lambda b, i: (b, 0, i, 0))
    full = lambda a, b_: pl.BlockSpec((a, b_), lambda b, i: (0, 0))
    return pl.pallas_call(
        functools.partial(_mix_kernel, d_attn=da),
        grid=(bsz, nt),
        in_specs=[tok(d), tok(da), chan,
                  pl.BlockSpec((1, 6, d), lambda b, i: (b, 0, 0)),
                  full(ds_, ds_), full(ds_, 1), full(da + ds_, d),
                  full(1, d), full(e, d), full(e, 1)],
        out_specs=[tok(d), pl.BlockSpec((tm * SUBLANES, LANES), lambda b, i: (b * nt + i, 0)),
                   pl.BlockSpec((e, tm), lambda b, i: (0, b * nt + i))],
        out_shape=[jax.ShapeDtypeStruct((bsz, s, d), F32),
                   jax.ShapeDtypeStruct((bsz * s * SUBLANES, LANES), F32),
                   jax.ShapeDtypeStruct((e, bsz * s), F32)],
        compiler_params=_cparams(("arbitrary", "arbitrary"), VMEM_LIMIT),
        name="mix",
    )(x, attn, yt4, mod3, wglu_t, bglu.reshape(-1, 1), wout,
      g2.reshape(1, -1), wr.T, br.reshape(-1, 1))


def _route_kernel(lg_ref, idx_ref, gate_ref, rank_ref, cnt_ref, run_ref, *, tm):
    i = pl.program_id(0)

    @pl.when(i == 0)
    def _():
        run_ref[...] = jnp.zeros_like(run_ref)

    l = lg_ref[...]
    e = l.shape[0]
    sub = lax.broadcasted_iota(jnp.int32, l.shape, 0)
    vals, sels, idxs = [], [], []
    for _k in range(TOP_K):
        mx = jnp.max(l, axis=0, keepdims=True)
        ix = jnp.min(jnp.where(l == mx, sub, e), axis=0, keepdims=True)
        sel = sub == ix
        vals.append(mx)
        idxs.append(ix)
        sels.append(sel)
        l = jnp.where(sel, -jnp.inf, l)
    ex = [jnp.exp(v - vals[0]) for v in vals]
    den = ex[0] + ex[1] + ex[2] + ex[3]
    chosen = jnp.zeros(l.shape, F32)
    for sel in sels:
        chosen = chosen + jnp.where(sel, 1.0, 0.0)
    r_i = lax.broadcasted_iota(jnp.int32, (tm, tm), 0)
    c_i = lax.broadcasted_iota(jnp.int32, (tm, tm), 1)
    tri = jnp.where(r_i < c_i, 1.0, 0.0).astype(BF16)
    before = jnp.dot(chosen.astype(BF16), tri, preferred_element_type=F32) + run_ref[...]
    ksub = lax.broadcasted_iota(jnp.int32, (TOP_K, tm), 0)
    idx_o = jnp.zeros((TOP_K, tm), jnp.int32)
    gate_o = jnp.zeros((TOP_K, tm), F32)
    rank_o = jnp.zeros((TOP_K, tm), F32)
    for k in range(TOP_K):
        rk = jnp.sum(jnp.where(sels[k], before, 0.0), axis=0, keepdims=True)
        idx_o = jnp.where(ksub == k, idxs[k], idx_o)
        gate_o = jnp.where(ksub == k, ex[k] / den, gate_o)
        rank_o = jnp.where(ksub == k, rk, rank_o)
    idx_ref[...] = idx_o
    gate_ref[...] = gate_o
    rank_ref[...] = rank_o.astype(jnp.int32)
    run_ref[...] = run_ref[...] + jnp.sum(chosen, axis=1, keepdims=True)
    cnt_ref[...] = run_ref[...].astype(jnp.int32)


def _route(logits_t, tm=512):
    e, n = logits_t.shape
    tok = lambda w: pl.BlockSpec((w, tm), lambda i: (0, i))
    return pl.pallas_call(
        functools.partial(_route_kernel, tm=tm),
        grid=(n // tm,),
        in_specs=[tok(e)],
        out_specs=[tok(TOP_K), tok(TOP_K), tok(TOP_K), pl.BlockSpec((e, 1), lambda i: (0, 0))],
        out_shape=[jax.ShapeDtypeStruct((TOP_K, n), jnp.int32),
                   jax.ShapeDtypeStruct((TOP_K, n), F32),
                   jax.ShapeDtypeStruct((TOP_K, n), jnp.int32),
                   jax.ShapeDtypeStruct((e, 1), jnp.int32)],
        scratch_shapes=[pltpu.VMEM((e, 1), F32)],
        compiler_params=_cparams(("arbitrary",)),
        name="route",
    )(logits_t)


def _dispatch_kernel(pend_ref, padded_ref, nb_ref, pos_ref, h_ref, xs_ref, zero_ref, sem, zsem,
                     *, tm, rb, n_exp, nb_max):
    i = pl.program_id(0)

    def zero_copy(blk_start):
        rows = pl.ds(pl.multiple_of(blk_start * SUBLANES, rb * SUBLANES), rb * SUBLANES)
        return pltpu.make_async_copy(zero_ref, xs_ref.at[rows, :], zsem)

    @pl.when(i == 0)
    def _():
        zero_ref[...] = jnp.zeros_like(zero_ref)
        for wait in (False, True):
            for e in range(n_exp):
                @pl.when(padded_ref[e] > 0)
                def _():
                    cp = zero_copy(pend_ref[e] - rb)
                    cp.wait() if wait else cp.start()

                @pl.when(nb_ref[0] + e < nb_max)
                def _():
                    cp = zero_copy((nb_ref[0] + e) * rb)
                    cp.wait() if wait else cp.start()

    @pl.when(i > 0)
    def _():
        def issue(t, _):
            for k in range(TOP_K):
                p = pos_ref[0, 0, k * tm + t]
                pltpu.make_async_copy(h_ref.at[_tile_rows(t), :], xs_ref.at[_tile_rows(p), :],
                                      sem).start(priority=k % 2)
            return 0

        lax.fori_loop(0, tm, issue, 0, unroll=4)
        for k in range(TOP_K):
            pltpu.make_async_copy(h_ref, xs_ref.at[pl.ds(0, tm * SUBLANES), :], sem).wait()


def _dispatch(pend, padded, nblk, pos3, h2, n_rows, tm, rb):
    n = h2.shape[0] // SUBLANES
    n_exp = pend.shape[0]
    tile = lambda i, *_: (jnp.maximum(i - 1, 0), 0)
    return pl.pallas_call(
        functools.partial(_dispatch_kernel, tm=tm, rb=rb, n_exp=n_exp, nb_max=n_rows // rb),
        grid_spec=pltpu.PrefetchScalarGridSpec(
            num_scalar_prefetch=3,
            grid=(1 + n // tm,),
            in_specs=[pl.BlockSpec((1, 1, tm * TOP_K), lambda i, *_: tile(i) + (0,),
                                   memory_space=pltpu.SMEM),
                      pl.BlockSpec((tm * SUBLANES, LANES), tile)],
            out_specs=pl.BlockSpec(memory_space=pl.ANY),
            scratch_shapes=[pltpu.VMEM((rb * SUBLANES, LANES), F32), pltpu.SemaphoreType.DMA,
                            pltpu.SemaphoreType.DMA]),
        out_shape=jax.ShapeDtypeStruct((n_rows * SUBLANES, LANES), F32),
        compiler_params=_cparams(("arbitrary",)),
        name="dispatch",
    )(pend, padded, nblk, pos3, h2)


def _experts_kernel(be_ref, xb_ref, nb_ref, x_ref, w1_ref, b1_ref, w2_ref, b2_ref, y_ref,
                    w1b, w2b, *, f):
    i = pl.program_id(0)
    prev = be_ref[jnp.maximum(i - 1, 0)]
    live = i < nb_ref[0]

    @pl.when(live & ((i == 0) | (be_ref[i] != prev)))
    def _():
        w1b[...] = w1_ref[0].astype(BF16)
        w2b[...] = w2_ref[0].astype(BF16)

    @pl.when(live)
    def _():
        rb = x_ref.shape[0] // SUBLANES
        x = jnp.concatenate([x_ref[pl.ds(j, rb, stride=SUBLANES), :].astype(BF16)
                             for j in range(SUBLANES)], axis=1)
        gu = jnp.dot(x, w1b[...], preferred_element_type=F32) + b1_ref[0]
        gate = jnp.minimum(gu[:, :f], SWIGLU_LIMIT)
        up = jnp.clip(gu[:, f:], -SWIGLU_LIMIT, SWIGLU_LIMIT)
        glu = gate * _sigmoid(SWIGLU_ALPHA * gate)
        hmid = ((up + 1.0) * glu).astype(BF16)
        y = jnp.dot(hmid, w2b[...], preferred_element_type=F32) + b2_ref[0]
        for j in range(SUBLANES):
            y_ref[pl.ds(j, rb, stride=SUBLANES), :] = y[:, j * LANES:(j + 1) * LANES]

    @pl.when(jnp.logical_not(live))
    def _():
        y_ref[...] = jnp.zeros_like(y_ref)


def _experts(block_e, xblk, nblk, xs, w1, b1, w2, b2, rb):
    e, d, f2 = w1.shape
    assert d == SUBLANES * LANES, "token-tile layout holds one row per (8, 128) f32 tile"
    f = f2 // 2
    nb_max = xs.shape[0] // (rb * SUBLANES)
    rows = pl.BlockSpec((rb * SUBLANES, LANES), lambda i, be, xb, nb: (xb[i], 0))
    return pl.pallas_call(
        functools.partial(_experts_kernel, f=f),
        grid_spec=pltpu.PrefetchScalarGridSpec(
            num_scalar_prefetch=3,
            grid=(nb_max,),
            in_specs=[rows,
                      pl.BlockSpec((1, d, f2), lambda i, be, xb, nb: (be[i], 0, 0)),
                      pl.BlockSpec((1, 1, f2), lambda i, be, xb, nb: (be[i], 0, 0)),
                      pl.BlockSpec((1, f, d), lambda i, be, xb, nb: (be[i], 0, 0)),
                      pl.BlockSpec((1, 1, d), lambda i, be, xb, nb: (be[i], 0, 0))],
            out_specs=pl.BlockSpec((rb * SUBLANES, LANES), lambda i, be, xb, nb: (i, 0)),
            scratch_shapes=[pltpu.VMEM((d, f2), BF16), pltpu.VMEM((f, d), BF16)]),
        out_shape=jax.ShapeDtypeStruct(xs.shape, F32),
        compiler_params=_cparams(("arbitrary",), VMEM_LIMIT),
        name="experts",
    )(block_e, xblk, nblk, xs, w1, b1.reshape(e, 1, f2), w2, b2.reshape(e, 1, d))


def _combine_kernel(pos_ref, nxt_ref, gate_ref, x1_ref, mod_ref, fg_ref, ys_ref, o_ref, buf, sems,
                    *, tm):
    i = pl.program_id(0)
    n_tiles = pl.num_programs(0)
    slot = lax.rem(i, 2)

    def gather(p_ref, dst_slot):
        def issue(t, _):
            for k in range(TOP_K):
                p = p_ref[0, 0, k * tm + t]
                pltpu.make_async_copy(ys_ref.at[_tile_rows(p), :],
                                      buf.at[dst_slot, k, _tile_rows(t), :],
                                      sems.at[dst_slot]).start(priority=k % 2)
            return 0

        lax.fori_loop(0, tm, issue, 0, unroll=4)

    @pl.when(i == 0)
    def _():
        gather(pos_ref, 0)

    @pl.when(i + 1 < n_tiles)
    def _():
        gather(nxt_ref, 1 - slot)

    for k in range(TOP_K):
        pltpu.make_async_copy(ys_ref.at[pl.ds(0, tm * SUBLANES), :], buf.at[slot, k],
                              sems.at[slot]).wait()
    gate = gate_ref[...]
    pieces = []
    for j in range(SUBLANES):
        acc = gate[:, 0:1] * buf[slot, 0, pl.ds(j, tm, stride=SUBLANES), :]
        for k in range(1, TOP_K):
            acc = acc + gate[:, k:k + 1] * buf[slot, k, pl.ds(j, tm, stride=SUBLANES), :]
        pieces.append(acc)
    moe = jnp.concatenate(pieces, axis=1)
    x2 = x1_ref[...] + mod_ref[0, 5:6, :] * moe
    o_ref[...] = _rms(x2, fg_ref[...])


def _combine(pos3, gates, x1, mod3, final_g, ys, tm):
    bsz, s, d = x1.shape
    n = bsz * s
    nt = s // tm
    n_tiles = n // tm
    pos_spec = lambda f: pl.BlockSpec((1, 1, tm * TOP_K), lambda i: (f(i), 0, 0),
                                      memory_space=pltpu.SMEM)
    out = pl.pallas_call(
        functools.partial(_combine_kernel, tm=tm),
        grid=(n_tiles,),
        in_specs=[pos_spec(lambda i: i),
                  pos_spec(lambda i: jnp.minimum(i + 1, n_tiles - 1)),
                  pl.BlockSpec((tm, TOP_K), lambda i: (i, 0)),
                  pl.BlockSpec((tm, d), lambda i: (i, 0)),
                  pl.BlockSpec((1, 6, d), lambda i: (i // nt, 0, 0)),
                  pl.BlockSpec((1, d), lambda i: (0, 0)),
                  pl.BlockSpec(memory_space=pl.ANY)],
        out_specs=pl.BlockSpec((tm, d), lambda i: (i, 0)),
        out_shape=jax.ShapeDtypeStruct((n, d), F32),
        scratch_shapes=[pltpu.VMEM((2, TOP_K, tm * SUBLANES, LANES), F32),
                        pltpu.SemaphoreType.DMA((2,))],
        compiler_params=_cparams(("arbitrary",), VMEM_LIMIT),
        name="combine",
    )(pos3, pos3, gates, x1.reshape(n, d), mod3, final_g.reshape(1, d), ys)
    return out.reshape(bsz, s, d)


def _layer(x, mod3, norm1_g, w_in, lq1, lk1, lq2, lk2, subln_g, ssm_a_re, ssm_a_im, ssm_log_dt,
           ssm_b_re, ssm_b_im, ssm_c_re, ssm_c_im, ssm_d, w_glu, b_glu, w_out, norm2_g,
           w_router, b_router, w1, b1, w2, b2, final_g):
    bsz, s, d = x.shape
    n = bsz * s
    d_attn = ATTN_HEADS * V_DIM
    n_qkv = 3 * d_attn

    qkv, ut4 = _inproj(x, mod3, norm1_g, w_in[:, :n_qkv].astype(BF16),
                       w_in[:, n_qkv:].T.astype(BF16))
    attn = _attention(qkv, lq1, lk1, lq2, lk2, subln_g)
    tables = _ssm_tables(ssm_a_re, ssm_a_im, ssm_log_dt, ssm_b_re, ssm_b_im, ssm_c_re, ssm_c_im,
                         SSM_CHUNK)
    yt4 = _ssm(ut4, tables, ssm_d)
    x1, h2, logits_t = _mix(x, attn, yt4, mod3, w_glu.T.astype(BF16), b_glu,
                            w_out.astype(BF16), norm2_g, w_router, b_router)

    idx, gates, rank, counts = _route(logits_t)
    rb = 512 if n * TOP_K >= 512 * N_EXPERTS else 128
    counts = counts.reshape(N_EXPERTS)
    padded = ((counts + rb - 1) // rb) * rb
    pend = jnp.cumsum(padded).astype(jnp.int32)
    pstart = pend - padded
    eids = jnp.arange(N_EXPERTS, dtype=jnp.int32)[:, None, None]
    pos = rank + jnp.sum(jnp.where(idx[None] == eids, pstart[:, None, None], 0), axis=0)
    pos = pos.astype(jnp.int32)
    nb_max = (n * TOP_K) // rb + N_EXPERTS
    n_rows = nb_max * rb
    nblk = pend[-1] // rb
    blk_ids = jnp.minimum(jnp.arange(nb_max, dtype=jnp.int32), nblk - 1)
    block_e = jnp.minimum(jnp.sum((pend[None, :] <= (blk_ids * rb)[:, None]).astype(jnp.int32), axis=1),
                          N_EXPERTS - 1).astype(jnp.int32)
    tmd = 512
    pos3 = (pos.reshape(TOP_K, n // tmd, tmd).transpose(1, 0, 2)
            .reshape(n // tmd, 1, TOP_K * tmd))
    gates_tok = gates.T

    nblk = nblk.reshape(1).astype(jnp.int32)
    xs = _dispatch(pend, padded.astype(jnp.int32), nblk, pos3, h2, n_rows, tmd, rb)
    ys = _experts(block_e, blk_ids, nblk, xs, w1, b1, w2, b2, rb)
    return _combine(pos3, gates_tok, x1, mod3, final_g, ys, tmd)


def kernel(x, c, w_ada, b_ada, norm1_g, w_in, lq1, lk1, lq2, lk2, subln_g, ssm_a_re, ssm_a_im,
           ssm_log_dt, ssm_b_re, ssm_b_im, ssm_c_re, ssm_c_im, ssm_d, w_glu, b_glu, w_out, norm2_g,
           w_router, b_router, w1, b1, w2, b2, final_g):
    assert w_ada.shape[0] == 1, "single-layer block"
    bsz, s, d = x.shape
    mod3 = _adaln(c, w_ada[0], b_ada[0]).reshape(bsz, 6, d)
    return _layer(x, mod3, norm1_g[0], w_in[0], lq1[0], lk1[0], lq2[0], lk2[0], subln_g[0],
                  ssm_a_re[0], ssm_a_im[0], ssm_log_dt[0], ssm_b_re[0], ssm_b_im[0], ssm_c_re[0],
                  ssm_c_im[0], ssm_d[0], w_glu[0], b_glu[0], w_out[0], norm2_g[0], w_router[0],
                  b_router[0], w1[0], b1[0], w2[0], b2[0], final_g)
```

```python
import functools
import math

import jax
import jax.numpy as jnp
from jax import lax
from jax.experimental import pallas as pl
from jax.experimental.pallas import tpu as pltpu

F32 = jnp.float32
BF16 = jnp.bfloat16
HIGHEST = lax.Precision.HIGHEST

RMS_EPS = 1e-6
MASK_VALUE = -1e30
ATTN_HEADS = 4
QK_DIM = 64
V_DIM = 128
SSM_GROUP_WIDTH = 16
SSM_CHUNK = 128
N_EXPERTS = 32
TOP_K = 4
SWIGLU_LIMIT = 7.0
SWIGLU_ALPHA = 1.702
LAMBDA_INIT = 0.8 - 0.6 * math.exp(-0.3 * 0)

VMEM_LIMIT = 56 * 1024 * 1024
NT_DIMS = (((1,), (1,)), ((), ()))
LANES = 128
SUBLANES = 8


def _tile_rows(t):
    return pl.ds(pl.multiple_of(t * SUBLANES, SUBLANES), SUBLANES)


def _cparams(sem, vmem=None):
    return pltpu.CompilerParams(dimension_semantics=sem, vmem_limit_bytes=vmem)


def _sigmoid(x):
    return 1.0 / (1.0 + jnp.exp(-x))


def _rms(x, g):
    ms = jnp.mean(x * x, axis=-1, keepdims=True)
    return x * lax.rsqrt(ms + RMS_EPS) * g


def _adaln_kernel(c_ref, w_ref, b_ref, o_ref):
    c = c_ref[...]
    ca = c * _sigmoid(c)
    o_ref[...] = jnp.dot(ca, w_ref[...], preferred_element_type=F32, precision=HIGHEST) + b_ref[...]


def _adaln(c, w, b):
    bsz, d = c.shape
    n = w.shape[1]
    tn = 1536
    return pl.pallas_call(
        _adaln_kernel,
        grid=(n // tn,),
        in_specs=[pl.BlockSpec((bsz, d), lambda j: (0, 0)),
                  pl.BlockSpec((d, tn), lambda j: (0, j)),
                  pl.BlockSpec((1, tn), lambda j: (0, j))],
        out_specs=pl.BlockSpec((bsz, tn), lambda j: (0, j)),
        out_shape=jax.ShapeDtypeStruct((bsz, n), F32),
        compiler_params=_cparams(("arbitrary",)),
        name="adaln",
    )(c, w, b.reshape(1, n))


def _inproj_kernel(x_ref, mod_ref, g_ref, wq_ref, wut_ref, qkv_ref, ut_ref):
    x = x_ref[0]
    y = _rms(x, g_ref[...])
    h = (y * (1.0 + mod_ref[0, 1:2, :]) + mod_ref[0, 0:1, :]).astype(BF16)
    qkv_ref[0] = jnp.dot(h, wq_ref[...], preferred_element_type=F32).astype(BF16)
    ut = lax.dot_general(wut_ref[...], h, NT_DIMS, preferred_element_type=F32)
    ut_ref[0] = pltpu.einshape("m(cl)->mcl", ut, l=SSM_CHUNK)


def _inproj(x, mod3, g, wq_bf16, wut_bf16, tm=1024):
    bsz, s, d = x.shape
    tm = min(tm, s)
    n_qkv = wq_bf16.shape[1]
    n_u = wut_bf16.shape[0]
    nck = tm // SSM_CHUNK
    return pl.pallas_call(
        _inproj_kernel,
        grid=(bsz, s // tm),
        in_specs=[pl.BlockSpec((1, tm, d), lambda b, i: (b, i, 0)),
                  pl.BlockSpec((1, 6, d), lambda b, i: (b, 0, 0)),
                  pl.BlockSpec((1, d), lambda b, i: (0, 0)),
                  pl.BlockSpec((d, n_qkv), lambda b, i: (0, 0)),
                  pl.BlockSpec((n_u, d), lambda b, i: (0, 0))],
        out_specs=[pl.BlockSpec((1, tm, n_qkv), lambda b, i: (b, i, 0)),
                   pl.BlockSpec((1, n_u, nck, SSM_CHUNK), lambda b, i: (b, 0, i, 0))],
        out_shape=[jax.ShapeDtypeStruct((bsz, s, n_qkv), BF16),
                   jax.ShapeDtypeStruct((bsz, n_u, s // SSM_CHUNK, SSM_CHUNK), F32)],
        compiler_params=_cparams(("arbitrary", "arbitrary"), VMEM_LIMIT),
        name="inproj",
    )(x, mod3, g.reshape(1, d), wq_bf16, wut_bf16)


def _attn_kernel(lq1_ref, lk1_ref, lq2_ref, lk2_ref, sg_ref, q_ref, k_ref, v_ref, o_ref,
                 vt_ref, m_ref, acc_ref, p_ref, al_ref, *, tq):
    qi = pl.program_id(2)
    n_acc = acc_ref.shape[0]

    @pl.when(qi == 0)
    def _():
        vt_ref[:V_DIM, :] = v_ref[0].astype(F32).T.astype(BF16)
        vt_ref[V_DIM:, :] = jnp.ones((n_acc - V_DIM, vt_ref.shape[1]), BF16)

    lam = (jnp.exp(jnp.sum(lq1_ref[...] * lk1_ref[...], axis=-1, keepdims=True))
           - jnp.exp(jnp.sum(lq2_ref[...] * lk2_ref[...], axis=-1, keepdims=True))
           + LAMBDA_INIT)
    q = q_ref[0] * (QK_DIM ** -0.5)
    lane = lax.broadcasted_iota(jnp.int32, q.shape, 1)
    zero = jnp.zeros_like(q)
    qs = jnp.concatenate([jnp.where(lane < QK_DIM, q, zero), jnp.where(lane >= QK_DIM, q, zero)],
                         axis=0)
    m_ref[...] = jnp.full(m_ref.shape, MASK_VALUE, F32)
    acc_ref[...] = jnp.zeros(acc_ref.shape, F32)

    def softmax_stage(ki, mask):
        start = pl.multiple_of(ki * tq, tq)
        st = lax.dot_general(k_ref[0, pl.ds(start, tq), :], qs, NT_DIMS,
                             preferred_element_type=F32)
        if mask is not None:
            st = jnp.where(mask, st, MASK_VALUE)
        m_prev = m_ref[...]
        m_new = jnp.maximum(m_prev, jnp.max(st, axis=0, keepdims=True))
        m_ref[...] = m_new
        return jnp.exp(st - m_new).astype(BF16), jnp.exp(m_prev - m_new)

    def value_stage(ki, pt, alpha):
        start = pl.multiple_of(ki * tq, tq)
        acc_ref[...] = alpha * acc_ref[...] + jnp.dot(vt_ref[:, pl.ds(start, tq)], pt,
                                                      preferred_element_type=F32)

    key = lax.broadcasted_iota(jnp.int32, (tq, 2 * tq), 0)
    col = lax.broadcasted_iota(jnp.int32, (tq, 2 * tq), 1)
    causal = key <= jnp.where(col >= tq, col - tq, col)

    @pl.when(qi == 0)
    def _():
        value_stage(0, *softmax_stage(0, causal))

    @pl.when(qi > 0)
    def _():
        p_ref[...], al_ref[...] = softmax_stage(0, None)

        def body(ki, carry):
            pt_prev = p_ref[...]
            al_prev = al_ref[...]
            pt, alpha = softmax_stage(ki, None)
            value_stage(ki - 1, pt_prev, al_prev)
            p_ref[...] = pt
            al_ref[...] = alpha
            return carry

        lax.fori_loop(1, qi, body, 0)
        pt_prev = p_ref[...]
        al_prev = al_ref[...]
        pt, alpha = softmax_stage(qi, causal)
        value_stage(qi - 1, pt_prev, al_prev)
        value_stage(qi, pt, alpha)

    a = acc_ref[...]

    ot = (a[:V_DIM, :tq] / a[V_DIM:V_DIM + 1, :tq]
          - lam * (a[:V_DIM, tq:] / a[V_DIM:V_DIM + 1, tq:]))
    o = _rms(ot.T, sg_ref[...]) * (1.0 - LAMBDA_INIT)
    o_ref[0] = o.astype(BF16)


def _attention(qkv, lq1, lk1, lq2, lk2, subln_g, tq=1024):
    bsz, s, _ = qkv.shape
    tq = min(tq, s)
    h = ATTN_HEADS
    n_acc = V_DIM + 8
    vec = lambda n: pl.BlockSpec((1, n), lambda b, hh, i: (0, 0))
    return pl.pallas_call(
        functools.partial(_attn_kernel, tq=tq),
        grid=(bsz, h, s // tq),
        in_specs=[vec(QK_DIM), vec(QK_DIM), vec(QK_DIM), vec(QK_DIM), vec(V_DIM),
                  pl.BlockSpec((1, tq, V_DIM), lambda b, hh, i: (b, i, hh)),
                  pl.BlockSpec((1, s, V_DIM), lambda b, hh, i: (b, 0, h + hh)),
                  pl.BlockSpec((1, s, V_DIM), lambda b, hh, i: (b, 0, 2 * h + hh))],
        out_specs=pl.BlockSpec((1, tq, V_DIM), lambda b, hh, i: (b, i, hh)),
        out_shape=jax.ShapeDtypeStruct((bsz, s, h * V_DIM), BF16),
        scratch_shapes=[pltpu.VMEM((n_acc, s), BF16),
                        pltpu.VMEM((1, 2 * tq), F32), pltpu.VMEM((n_acc, 2 * tq), F32),
                        pltpu.VMEM((tq, 2 * tq), BF16), pltpu.VMEM((1, 2 * tq), F32)],
        compiler_params=_cparams(("arbitrary", "arbitrary", "arbitrary"), VMEM_LIMIT),
        name="attention",
    )(lq1.reshape(1, -1), lk1.reshape(1, -1), lq2.reshape(1, -1), lk2.reshape(1, -1),
      subln_g.reshape(1, -1), qkv, qkv, qkv)


def _ssm_tables(a_re, a_im, log_dt, b_re, b_im, c_re, c_im, t):
    g, p = a_re.shape
    w = b_re.shape[-1]
    dt = jnp.exp(log_dt.astype(F32))[:, None]
    lam = lax.complex(jnp.minimum(a_re.astype(F32), -1e-4), a_im.astype(F32))
    lam_dt = lam * dt
    lam_bar = jnp.exp(lam_dt)
    b_bar = ((lam_bar - 1.0) / lam)[..., None] * lax.complex(b_re.astype(F32), b_im.astype(F32))
    c_cplx = lax.complex(c_re.astype(F32), c_im.astype(F32))
    tau = jnp.arange(t + 1, dtype=F32)
    pw = jnp.exp(lam_dt[:, None, :] * tau[None, :, None])
    ktab = jnp.einsum('gcp,gtp,gpd->gdct', c_cplx, pw[:, :t], b_bar).real.reshape(g, w * w, t)
    pw_rev = pw[:, t - 1 - jnp.arange(t)]
    pw_nxt = pw[:, 1:t + 1].transpose(0, 2, 1)
    bb = b_bar.transpose(0, 2, 1)
    cc = c_cplx.transpose(0, 2, 1)
    lt = pw[:, t][:, None, :]
    return (ktab, pw_rev.real, pw_rev.imag, pw_nxt.real, pw_nxt.imag, bb.real, bb.imag,
            cc.real, cc.imag, lt.real, lt.imag)


def _ssm_kernel(u_ref, k_ref, prr_ref, pri_ref, pnr_ref, pni_ref, bbr_ref, bbi_ref, ccr_ref, cci_ref,
                ltre_ref, ltim_ref, d_ref, y_ref,
                toep, ubuf, wre, wim, vre, vim, slre, slim, spre, spim, *, bsz, nc):
    gw = SSM_GROUP_WIDTH
    t = SSM_CHUNK
    r = bsz * nc

    jrow = lax.broadcasted_iota(jnp.int32, (t, t), 0)
    tcol = lax.broadcasted_iota(jnp.int32, (t, t), 1)
    causal = tcol >= jrow

    def build(d, carry):
        rows = pl.ds(pl.multiple_of(d * t, t), t)
        for c in range(gw):
            krow = k_ref[0, pl.ds(d * gw + c, 1), :]
            blk = pltpu.roll(jnp.broadcast_to(krow, (t, t)), 0, 1, stride=1, stride_axis=0)
            toep[rows, c * t:(c + 1) * t] = jnp.where(causal, blk, 0.0).astype(BF16)
        return carry

    lax.fori_loop(0, gw, build, 0)

    for d in range(gw):
        ubuf[:, d * t:(d + 1) * t] = u_ref[:, d].reshape(r, t).astype(BF16)
    u = ubuf[...]

    prr, pri = prr_ref[0], pri_ref[0]
    for d in range(gw):
        br, bi = bbr_ref[0, d:d + 1, :], bbi_ref[0, d:d + 1, :]
        wre[d * t:(d + 1) * t, :] = (prr * br - pri * bi).astype(BF16)
        wim[d * t:(d + 1) * t, :] = (prr * bi + pri * br).astype(BF16)
    pnr, pni = pnr_ref[0], pni_ref[0]
    for c in range(gw):
        cr, ci = ccr_ref[0, :, c:c + 1], cci_ref[0, :, c:c + 1]
        vre[:, c * t:(c + 1) * t] = (cr * pnr - ci * pni).astype(BF16)
        vim[:, c * t:(c + 1) * t] = (-(cr * pni + ci * pnr)).astype(BF16)

    slre[...] = jnp.dot(u, wre[...], preferred_element_type=F32)
    slim[...] = jnp.dot(u, wim[...], preferred_element_type=F32)
    a = ltre_ref[0]
    b = ltim_ref[0]

    def step(c, carry):
        sre, sim = carry
        rows = pl.ds(c, bsz, stride=nc)
        spre[rows, :] = sre
        spim[rows, :] = sim
        nre = a * sre - b * sim + slre[rows, :]
        nim = a * sim + b * sre + slim[rows, :]
        return nre, nim

    z = jnp.zeros((bsz, a.shape[-1]), F32)
    lax.fori_loop(0, nc, step, (z, z))
    y = jnp.dot(u, toep[...], preferred_element_type=F32)
    y = y + jnp.dot(spre[...].astype(BF16), vre[...], preferred_element_type=F32)
    y = y + jnp.dot(spim[...].astype(BF16), vim[...], preferred_element_type=F32)
    for c in range(gw):
        y_ref[:, c] = (y[:, c * t:(c + 1) * t].reshape(bsz, nc, t)
                       + d_ref[0, c:c + 1, :] * u_ref[:, c])


def _ssm(ut4, tables, d_skip):
    bsz, d_ssm, nc, t = ut4.shape
    d_tab = jnp.broadcast_to(d_skip.reshape(-1, SSM_GROUP_WIDTH, 1), (d_skip.size // SSM_GROUP_WIDTH,
                                                                    SSM_GROUP_WIDTH, t))
    gw = SSM_GROUP_WIDTH
    g = d_ssm // gw
    p = tables[-1].shape[-1]
    r = bsz * nc
    blk = lambda a, b: pl.BlockSpec((1, a, b), lambda i: (i, 0, 0))
    grp = pl.BlockSpec((bsz, gw, nc, t), lambda i: (0, i, 0, 0))
    return pl.pallas_call(
        functools.partial(_ssm_kernel, bsz=bsz, nc=nc),
        grid=(g,),
        in_specs=[grp, blk(gw * gw, t), blk(t, p), blk(t, p), blk(p, t), blk(p, t),
                  blk(gw, p), blk(gw, p), blk(p, gw), blk(p, gw), blk(1, p), blk(1, p), blk(gw, t)],
        out_specs=grp,
        out_shape=jax.ShapeDtypeStruct(ut4.shape, F32),
        scratch_shapes=[pltpu.VMEM((gw * t, gw * t), BF16), pltpu.VMEM((r, gw * t), BF16),
                        pltpu.VMEM((gw * t, p), BF16), pltpu.VMEM((gw * t, p), BF16),
                        pltpu.VMEM((p, gw * t), BF16), pltpu.VMEM((p, gw * t), BF16)]
        + [pltpu.VMEM((r, p), F32)] * 4,
        compiler_params=_cparams(("arbitrary",), VMEM_LIMIT),
        name="ssm",
    )(ut4, *tables, d_tab)


def _mix_kernel(x_ref, attn_ref, yt_ref, mod_ref, wglut_ref, bglu_ref, wout_ref,
                g2_ref, wrt_ref, br_ref, x1_ref, h2_ref, lgt_ref, *, d_attn):
    nck = yt_ref.shape[2]
    y3 = pltpu.einshape("mcl->cml", yt_ref[0])
    y = jnp.concatenate([y3[c] for c in range(nck)], axis=1)
    z = 0.5 * y * (1.0 + jnp.tanh(math.sqrt(2.0 / math.pi) * (y + 0.044715 * (y * y * y))))
    gl = jnp.dot(wglut_ref[...], z.astype(BF16), preferred_element_type=F32) + bglu_ref[...]
    so = (z * _sigmoid(gl)).T
    mix = (jnp.dot(attn_ref[0], wout_ref[:d_attn, :], preferred_element_type=F32)
           + jnp.dot(so.astype(BF16), wout_ref[d_attn:, :], preferred_element_type=F32))
    x1 = x_ref[0] + mod_ref[0, 2:3, :] * mix
    x1_ref[0] = x1
    h2 = _rms(x1, g2_ref[...]) * (1.0 + mod_ref[0, 4:5, :]) + mod_ref[0, 3:4, :]
    for j in range(SUBLANES):
        h2_ref[pl.ds(j, h2.shape[0], stride=SUBLANES), :] = h2[:, j * LANES:(j + 1) * LANES]
    def split(v):
        hi = v.astype(BF16)
        return hi, (v - hi.astype(F32)).astype(BF16)

    w_hi, w_lo = split(wrt_ref[...])
    h_hi, h_lo = split(h2)
    nt = functools.partial(lax.dot_general, dimension_numbers=NT_DIMS, preferred_element_type=F32)
    lgt_ref[...] = nt(w_hi, h_hi) + (nt(w_hi, h_lo) + nt(w_lo, h_hi)) + br_ref[...]


def _mix(x, attn, yt4, mod3, wglu_t, bglu, wout, g2, wr, br, tm=1024):
    bsz, s, d = x.shape
    tm = min(tm, s)
    da = attn.shape[-1]
    ds_ = yt4.shape[1]
    e = wr.shape[-1]
    nt = s // tm
    tok = lambda n: pl.BlockSpec((1, tm, n), lambda b, i: (b, i, 0))
    chan = pl.BlockSpec((1, ds_, tm // SSM_CHUNK, SSM_CHUNK), lambda b, i: (b, 0, i, 0))
    full = lambda a, b_: pl.BlockSpec((a, b_), lambda b, i: (0, 0))
    return pl.pallas_call(
        functools.partial(_mix_kernel, d_attn=da),
        grid=(bsz, nt),
        in_specs=[tok(d), tok(da), chan,
                  pl.BlockSpec((1, 6, d), lambda b, i: (b, 0, 0)),
                  full(ds_, ds_), full(ds_, 1), full(da + ds_, d),
                  full(1, d), full(e, d), full(e, 1)],
        out_specs=[tok(d), pl.BlockSpec((tm * SUBLANES, LANES), lambda b, i: (b * nt + i, 0)),
                   pl.BlockSpec((e, tm), lambda b, i: (0, b * nt + i))],
        out_shape=[jax.ShapeDtypeStruct((bsz, s, d), F32),
                   jax.ShapeDtypeStruct((bsz * s * SUBLANES, LANES), F32),
                   jax.ShapeDtypeStruct((e, bsz * s), F32)],
        compiler_params=_cparams(("arbitrary", "arbitrary"), VMEM_LIMIT),
        name="mix",
    )(x, attn, yt4, mod3, wglu_t, bglu.reshape(-1, 1), wout,
      g2.reshape(1, -1), wr.T, br.reshape(-1, 1))


def _route_kernel(lg_ref, idx_ref, gate_ref, rank_ref, cnt_ref, run_ref, *, tm):
    i = pl.program_id(0)

    @pl.when(i == 0)
    def _():
        run_ref[...] = jnp.zeros_like(run_ref)

    l = lg_ref[...]
    e = l.shape[0]
    sub = lax.broadcasted_iota(jnp.int32, l.shape, 0)
    vals, sels, idxs = [], [], []
    for _k in range(TOP_K):
        mx = jnp.max(l, axis=0, keepdims=True)
        ix = jnp.min(jnp.where(l == mx, sub, e), axis=0, keepdims=True)
        sel = sub == ix
        vals.append(mx)
        idxs.append(ix)
        sels.append(sel)
        l = jnp.where(sel, -jnp.inf, l)
    ex = [jnp.exp(v - vals[0]) for v in vals]
    den = ex[0] + ex[1] + ex[2] + ex[3]
    chosen = jnp.zeros(l.shape, F32)
    for sel in sels:
        chosen = chosen + jnp.where(sel, 1.0, 0.0)
    r_i = lax.broadcasted_iota(jnp.int32, (tm, tm), 0)
    c_i = lax.broadcasted_iota(jnp.int32, (tm, tm), 1)
    tri = jnp.where(r_i < c_i, 1.0, 0.0).astype(BF16)
    before = jnp.dot(chosen.astype(BF16), tri, preferred_element_type=F32) + run_ref[...]
    ksub = lax.broadcasted_iota(jnp.int32, (TOP_K, tm), 0)
    idx_o = jnp.zeros((TOP_K, tm), jnp.int32)
    gate_o = jnp.zeros((TOP_K, tm), F32)
    rank_o = jnp.zeros((TOP_K, tm), F32)
    for k in range(TOP_K):
        rk = jnp.sum(jnp.where(sels[k], before, 0.0), axis=0, keepdims=True)
        idx_o = jnp.where(ksub == k, idxs[k], idx_o)
        gate_o = jnp.where(ksub == k, ex[k] / den, gate_o)
        rank_o = jnp.where(ksub == k, rk, rank_o)
    idx_ref[...] = idx_o
    gate_ref[...] = gate_o
    rank_ref[...] = rank_o.astype(jnp.int32)
    run_ref[...] = run_ref[...] + jnp.sum(chosen, axis=1, keepdims=True)
    cnt_ref[...] = run_ref[...].astype(jnp.int32)


def _route(logits_t, tm=512):
    e, n = logits_t.shape
    tok = lambda w: pl.BlockSpec((w, tm), lambda i: (0, i))
    return pl.pallas_call(
        functools.partial(_route_kernel, tm=tm),
        grid=(n // tm,),
        in_specs=[tok(e)],
        out_specs=[tok(TOP_K), tok(TOP_K), tok(TOP_K), pl.BlockSpec((e, 1), lambda i: (0, 0))],
        out_shape=[jax.ShapeDtypeStruct((TOP_K, n), jnp.int32),
                   jax.ShapeDtypeStruct((TOP_K, n), F32),
                   jax.ShapeDtypeStruct((TOP_K, n), jnp.int32),
                   jax.ShapeDtypeStruct((e, 1), jnp.int32)],
        scratch_shapes=[pltpu.VMEM((e, 1), F32)],
        compiler_params=_cparams(("arbitrary",)),
        name="route",
    )(logits_t)


def _dispatch_kernel(pend_ref, padded_ref, nb_ref, pos_ref, h_ref, xs_ref, zero_ref, sem, zsem,
                     *, tm, rb, n_exp, nb_max):
    i = pl.program_id(0)

    def zero_copy(blk_start):
        rows = pl.ds(pl.multiple_of(blk_start * SUBLANES, rb * SUBLANES), rb * SUBLANES)
        return pltpu.make_async_copy(zero_ref, xs_ref.at[rows, :], zsem)

    @pl.when(i == 0)
    def _():
        zero_ref[...] = jnp.zeros_like(zero_ref)
        for wait in (False, True):
            for e in range(n_exp):
                @pl.when(padded_ref[e] > 0)
                def _():
                    cp = zero_copy(pend_ref[e] - rb)
                    cp.wait() if wait else cp.start()

                @pl.when(nb_ref[0] + e < nb_max)
                def _():
                    cp = zero_copy((nb_ref[0] + e) * rb)
                    cp.wait() if wait else cp.start()

    @pl.when(i > 0)
    def _():
        def issue(t, _):
            for k in range(TOP_K):
                p = pos_ref[0, 0, k * tm + t]
                pltpu.make_async_copy(h_ref.at[_tile_rows(t), :], xs_ref.at[_tile_rows(p), :],
                                      sem).start(priority=k % 2)
            return 0

        lax.fori_loop(0, tm, issue, 0, unroll=4)
        for k in range(TOP_K):
            pltpu.make_async_copy(h_ref, xs_ref.at[pl.ds(0, tm * SUBLANES), :], sem).wait()


def _dispatch(pend, padded, nblk, pos3, h2, n_rows, tm, rb):
    n = h2.shape[0] // SUBLANES
    n_exp = pend.shape[0]
    tile = lambda i, *_: (jnp.maximum(i - 1, 0), 0)
    return pl.pallas_call(
        functools.partial(_dispatch_kernel, tm=tm, rb=rb, n_exp=n_exp, nb_max=n_rows // rb),
        grid_spec=pltpu.PrefetchScalarGridSpec(
            num_scalar_prefetch=3,
            grid=(1 + n // tm,),
            in_specs=[pl.BlockSpec((1, 1, tm * TOP_K), lambda i, *_: tile(i) + (0,),
                                   memory_space=pltpu.SMEM),
                      pl.BlockSpec((tm * SUBLANES, LANES), tile)],
            out_specs=pl.BlockSpec(memory_space=pl.ANY),
            scratch_shapes=[pltpu.VMEM((rb * SUBLANES, LANES), F32), pltpu.SemaphoreType.DMA,
                            pltpu.SemaphoreType.DMA]),
        out_shape=jax.ShapeDtypeStruct((n_rows * SUBLANES, LANES), F32),
        compiler_params=_cparams(("arbitrary",)),
        name="dispatch",
    )(pend, padded, nblk, pos3, h2)


def _experts_kernel(be_ref, xb_ref, nb_ref, x_ref, w1_ref, b1_ref, w2_ref, b2_ref, y_ref,
                    w1b, w2b, *, f):
    i = pl.program_id(0)
    prev = be_ref[jnp.maximum(i - 1, 0)]
    live = i < nb_ref[0]

    @pl.when(live & ((i == 0) | (be_ref[i] != prev)))
    def _():
        w1b[...] = w1_ref[0].astype(BF16)
        w2b[...] = w2_ref[0].astype(BF16)

    @pl.when(live)
    def _():
        rb = x_ref.shape[0] // SUBLANES
        x = jnp.concatenate([x_ref[pl.ds(j, rb, stride=SUBLANES), :].astype(BF16)
                             for j in range(SUBLANES)], axis=1)
        gu = jnp.dot(x, w1b[...], preferred_element_type=F32) + b1_ref[0]
        gate = jnp.minimum(gu[:, :f], SWIGLU_LIMIT)
        up = jnp.clip(gu[:, f:], -SWIGLU_LIMIT, SWIGLU_LIMIT)
        glu = gate * _sigmoid(SWIGLU_ALPHA * gate)
        hmid = ((up + 1.0) * glu).astype(BF16)
        y = jnp.dot(hmid, w2b[...], preferred_element_type=F32) + b2_ref[0]
        for j in range(SUBLANES):
            y_ref[pl.ds(j, rb, stride=SUBLANES), :] = y[:, j * LANES:(j + 1) * LANES]

    @pl.when(jnp.logical_not(live))
    def _():
        y_ref[...] = jnp.zeros_like(y_ref)


def _experts(block_e, xblk, nblk, xs, w1, b1, w2, b2, rb):
    e, d, f2 = w1.shape
    assert d == SUBLANES * LANES, "token-tile layout holds one row per (8, 128) f32 tile"
    f = f2 // 2
    nb_max = xs.shape[0] // (rb * SUBLANES)
    rows = pl.BlockSpec((rb * SUBLANES, LANES), lambda i, be, xb, nb: (xb[i], 0))
    return pl.pallas_call(
        functools.partial(_experts_kernel, f=f),
        grid_spec=pltpu.PrefetchScalarGridSpec(
            num_scalar_prefetch=3,
            grid=(nb_max,),
            in_specs=[rows,
                      pl.BlockSpec((1, d, f2), lambda i, be, xb, nb: (be[i], 0, 0)),
                      pl.BlockSpec((1, 1, f2), lambda i, be, xb, nb: (be[i], 0, 0)),
                      pl.BlockSpec((1, f, d), lambda i, be, xb, nb: (be[i], 0, 0)),
                      pl.BlockSpec((1, 1, d), lambda i, be, xb, nb: (be[i], 0, 0))],
            out_specs=pl.BlockSpec((rb * SUBLANES, LANES), lambda i, be, xb, nb: (i, 0)),
            scratch_shapes=[pltpu.VMEM((d, f2), BF16), pltpu.VMEM((f, d), BF16)]),
        out_shape=jax.ShapeDtypeStruct(xs.shape, F32),
        compiler_params=_cparams(("arbitrary",), VMEM_LIMIT),
        name="experts",
    )(block_e, xblk, nblk, xs, w1, b1.reshape(e, 1, f2), w2, b2.reshape(e, 1, d))


def _combine_kernel(pos_ref, nxt_ref, gate_ref, x1_ref, mod_ref, fg_ref, ys_ref, o_ref, buf, sems,
                    *, tm):
    i = pl.program_id(0)
    n_tiles = pl.num_programs(0)
    slot = lax.rem(i, 2)

    def gather(p_ref, dst_slot):
        def issue(t, _):
            for k in range(TOP_K):
                p = p_ref[0, 0, k * tm + t]
                pltpu.make_async_copy(ys_ref.at[_tile_rows(p), :],
                                      buf.at[dst_slot, k, _tile_rows(t), :],
                                      sems.at[dst_slot]).start(priority=k % 2)
            return 0

        lax.fori_loop(0, tm, issue, 0, unroll=4)

    @pl.when(i == 0)
    def _():
        gather(pos_ref, 0)

    @pl.when(i + 1 < n_tiles)
    def _():
        gather(nxt_ref, 1 - slot)

    for k in range(TOP_K):
        pltpu.make_async_copy(ys_ref.at[pl.ds(0, tm * SUBLANES), :], buf.at[slot, k],
                              sems.at[slot]).wait()
    gate = gate_ref[...]
    pieces = []
    for j in range(SUBLANES):
        acc = gate[:, 0:1] * buf[slot, 0, pl.ds(j, tm, stride=SUBLANES), :]
        for k in range(1, TOP_K):
            acc = acc + gate[:, k:k + 1] * buf[slot, k, pl.ds(j, tm, stride=SUBLANES), :]
        pieces.append(acc)
    moe = jnp.concatenate(pieces, axis=1)
    x2 = x1_ref[...] + mod_ref[0, 5:6, :] * moe
    o_ref[...] = _rms(x2, fg_ref[...])


def _combine(pos3, gates, x1, mod3, final_g, ys, tm):
    bsz, s, d = x1.shape
    n = bsz * s
    nt = s // tm
    n_tiles = n // tm
    pos_spec = lambda f: pl.BlockSpec((1, 1, tm * TOP_K), lambda i: (f(i), 0, 0),
                                      memory_space=pltpu.SMEM)
    out = pl.pallas_call(
        functools.partial(_combine_kernel, tm=tm),
        grid=(n_tiles,),
        in_specs=[pos_spec(lambda i: i),
                  pos_spec(lambda i: jnp.minimum(i + 1, n_tiles - 1)),
                  pl.BlockSpec((tm, TOP_K), lambda i: (i, 0)),
                  pl.BlockSpec((tm, d), lambda i: (i, 0)),
                  pl.BlockSpec((1, 6, d), lambda i: (i // nt, 0, 0)),
                  pl.BlockSpec((1, d), lambda i: (0, 0)),
                  pl.BlockSpec(memory_space=pl.ANY)],
        out_specs=pl.BlockSpec((tm, d), lambda i: (i, 0)),
        out_shape=jax.ShapeDtypeStruct((n, d), F32),
        scratch_shapes=[pltpu.VMEM((2, TOP_K, tm * SUBLANES, LANES), F32),
                        pltpu.SemaphoreType.DMA((2,))],
        compiler_params=_cparams(("arbitrary",), VMEM_LIMIT),
        name="combine",
    )(pos3, pos3, gates, x1.reshape(n, d), mod3, final_g.reshape(1, d), ys)
    return out.reshape(bsz, s, d)


def _layer(x, mod3, norm1_g, w_in, lq1, lk1, lq2, lk2, subln_g, ssm_a_re, ssm_a_im, ssm_log_dt,
           ssm_b_re, ssm_b_im, ssm_c_re, ssm_c_im, ssm_d, w_glu, b_glu, w_out, norm2_g,
           w_router, b_router, w1, b1, w2, b2, final_g):
    bsz, s, d = x.shape
    n = bsz * s
    d_attn = ATTN_HEADS * V_DIM
    n_qkv = 3 * d_attn

    qkv, ut4 = _inproj(x, mod3, norm1_g, w_in[:, :n_qkv].astype(BF16),
                       w_in[:, n_qkv:].T.astype(BF16))
    attn = _attention(qkv, lq1, lk1, lq2, lk2, subln_g)
    tables = _ssm_tables(ssm_a_re, ssm_a_im, ssm_log_dt, ssm_b_re, ssm_b_im, ssm_c_re, ssm_c_im,
                         SSM_CHUNK)
    yt4 = _ssm(ut4, tables, ssm_d)
    x1, h2, logits_t = _mix(x, attn, yt4, mod3, w_glu.T.astype(BF16), b_glu,
                            w_out.astype(BF16), norm2_g, w_router, b_router)

    idx, gates, rank, counts = _route(logits_t)
    rb = 512 if n * TOP_K >= 512 * N_EXPERTS else 128
    counts = counts.reshape(N_EXPERTS)
    padded = ((counts + rb - 1) // rb) * rb
    pend = jnp.cumsum(padded).astype(jnp.int32)
    pstart = pend - padded
    eids = jnp.arange(N_EXPERTS, dtype=jnp.int32)[:, None, None]
    pos = rank + jnp.sum(jnp.where(idx[None] == eids, pstart[:, None, None], 0), axis=0)
    pos = pos.astype(jnp.int32)
    nb_max = (n * TOP_K) // rb + N_EXPERTS
    n_rows = nb_max * rb
    nblk = pend[-1] // rb
    blk_ids = jnp.minimum(jnp.arange(nb_max, dtype=jnp.int32), nblk - 1)
    block_e = jnp.minimum(jnp.sum((pend[None, :] <= (blk_ids * rb)[:, None]).astype(jnp.int32), axis=1),
                          N_EXPERTS - 1).astype(jnp.int32)
    tmd = 512
    pos3 = (pos.reshape(TOP_K, n // tmd, tmd).transpose(1, 0, 2)
            .reshape(n // tmd, 1, TOP_K * tmd))
    gates_tok = gates.T

    nblk = nblk.reshape(1).astype(jnp.int32)
    xs = _dispatch(pend, padded.astype(jnp.int32), nblk, pos3, h2, n_rows, tmd, rb)
    ys = _experts(block_e, blk_ids, nblk, xs, w1, b1, w2, b2, rb)
    return _combine(pos3, gates_tok, x1, mod3, final_g, ys, tmd)


def kernel(x, c, w_ada, b_ada, norm1_g, w_in, lq1, lk1, lq2, lk2, subln_g, ssm_a_re, ssm_a_im,
           ssm_log_dt, ssm_b_re, ssm_b_im, ssm_c_re, ssm_c_im, ssm_d, w_glu, b_glu, w_out, norm2_g,
           w_router, b_router, w1, b1, w2, b2, final_g):
    assert w_ada.shape[0] == 1, "single-layer block"
    bsz, s, d = x.shape
    mod3 = _adaln(c, w_ada[0], b_ada[0]).reshape(bsz, 6, d)
    return _layer(x, mod3, norm1_g[0], w_in[0], lq1[0], lk1[0], lq2[0], lk2[0], subln_g[0],
                  ssm_a_re[0], ssm_a_im[0], ssm_log_dt[0], ssm_b_re[0], ssm_b_im[0], ssm_c_re[0],
                  ssm_c_im[0], ssm_d[0], w_glu[0], b_glu[0], w_out[0], norm2_g[0], w_router[0],
                  b_router[0], w1[0], b1[0], w2[0], b2[0], final_g)
```

```python
import functools
import math

import jax
import jax.numpy as jnp
from jax import lax
from jax.experimental import pallas as pl
from jax.experimental.pallas import tpu as pltpu

F32 = jnp.float32
BF16 = jnp.bfloat16
HIGHEST = lax.Precision.HIGHEST

RMS_EPS = 1e-6
MASK_VALUE = -1e30
ATTN_HEADS = 4
QK_DIM = 64
V_DIM = 128
SSM_GROUP_WIDTH = 16
SSM_CHUNK = 128
N_EXPERTS = 32
TOP_K = 4
SWIGLU_LIMIT = 7.0
SWIGLU_ALPHA = 1.702
LAMBDA_INIT = 0.8 - 0.6 * math.exp(-0.3 * 0)

VMEM_LIMIT = 56 * 1024 * 1024
NT_DIMS = (((1,), (1,)), ((), ()))
LANES = 128
SUBLANES = 8


def _tile_rows(t):
    return pl.ds(pl.multiple_of(t * SUBLANES, SUBLANES), SUBLANES)


def _cparams(sem, vmem=None):
    return pltpu.CompilerParams(dimension_semantics=sem, vmem_limit_bytes=vmem)


def _sigmoid(x):
    return 1.0 / (1.0 + jnp.exp(-x))


def _rms(x, g):
    ms = jnp.mean(x * x, axis=-1, keepdims=True)
    return x * lax.rsqrt(ms + RMS_EPS) * g


def _adaln_kernel(c_ref, w_ref, b_ref, o_ref):
    c = c_ref[...]
    ca = c * _sigmoid(c)
    o_ref[...] = jnp.dot(ca, w_ref[...], preferred_element_type=F32, precision=HIGHEST) + b_ref[...]


def _adaln(c, w, b):
    bsz, d = c.shape
    n = w.shape[1]
    tn = 1536
    return pl.pallas_call(
        _adaln_kernel,
        grid=(n // tn,),
        in_specs=[pl.BlockSpec((bsz, d), lambda j: (0, 0)),
                  pl.BlockSpec((d, tn), lambda j: (0, j)),
                  pl.BlockSpec((1, tn), lambda j: (0, j))],
        out_specs=pl.BlockSpec((bsz, tn), lambda j: (0, j)),
        out_shape=jax.ShapeDtypeStruct((bsz, n), F32),
        compiler_params=_cparams(("arbitrary",)),
        name="adaln",
    )(c, w, b.reshape(1, n))


def _inproj_kernel(x_ref, mod_ref, g_ref, wq_ref, wut_ref, qkv_ref, ut_ref):
    x = x_ref[0]
    y = _rms(x, g_ref[...])
    h = (y * (1.0 + mod_ref[0, 1:2, :]) + mod_ref[0, 0:1, :]).astype(BF16)
    qkv_ref[0] = jnp.dot(h, wq_ref[...], preferred_element_type=F32).astype(BF16)
    ut = lax.dot_general(wut_ref[...], h, NT_DIMS, preferred_element_type=F32)
    ut_ref[0] = pltpu.einshape("m(cl)->mcl", ut, l=SSM_CHUNK)


def _inproj(x, mod3, g, wq_bf16, wut_bf16, tm=1024):
    bsz, s, d = x.shape
    tm = min(tm, s)
    n_qkv = wq_bf16.shape[1]
    n_u = wut_bf16.shape[0]
    nck = tm // SSM_CHUNK
    return pl.pallas_call(
        _inproj_kernel,
        grid=(bsz, s // tm),
        in_specs=[pl.BlockSpec((1, tm, d), lambda b, i: (b, i, 0)),
                  pl.BlockSpec((1, 6, d), lambda b, i: (b, 0, 0)),
                  pl.BlockSpec((1, d), lambda b, i: (0, 0)),
                  pl.BlockSpec((d, n_qkv), lambda b, i: (0, 0)),
                  pl.BlockSpec((n_u, d), lambda b, i: (0, 0))],
        out_specs=[pl.BlockSpec((1, tm, n_qkv), lambda b, i: (b, i, 0)),
                   pl.BlockSpec((1, n_u, nck, SSM_CHUNK), lambda b, i: (b, 0, i, 0))],
        out_shape=[jax.ShapeDtypeStruct((bsz, s, n_qkv), BF16),
                   jax.ShapeDtypeStruct((bsz, n_u, s // SSM_CHUNK, SSM_CHUNK), F32)],
        compiler_params=_cparams(("arbitrary", "arbitrary"), VMEM_LIMIT),
        name="inproj",
    )(x, mod3, g.reshape(1, d), wq_bf16, wut_bf16)


def _attn_kernel(lq1_ref, lk1_ref, lq2_ref, lk2_ref, sg_ref, q_ref, k_ref, v_ref, o_ref,
                 vt_ref, m_ref, acc_ref, p_ref, al_ref, *, tq):
    qi = pl.program_id(2)
    n_acc = acc_ref.shape[0]

    @pl.when(qi == 0)
    def _():
        vt_ref[:V_DIM, :] = v_ref[0].astype(F32).T.astype(BF16)
        vt_ref[V_DIM:, :] = jnp.ones((n_acc - V_DIM, vt_ref.shape[1]), BF16)

    lam = (jnp.exp(jnp.sum(lq1_ref[...] * lk1_ref[...], axis=-1, keepdims=True))
           - jnp.exp(jnp.sum(lq2_ref[...] * lk2_ref[...], axis=-1, keepdims=True))
           + LAMBDA_INIT)
    q = q_ref[0] * (QK_DIM ** -0.5)
    lane = lax.broadcasted_iota(jnp.int32, q.shape, 1)
    zero = jnp.zeros_like(q)
    qs = jnp.concatenate([jnp.where(lane < QK_DIM, q, zero), jnp.where(lane >= QK_DIM, q, zero)],
                         axis=0)
    m_ref[...] = jnp.full(m_ref.shape, MASK_VALUE, F32)
    acc_ref[...] = jnp.zeros(acc_ref.shape, F32)

    def softmax_stage(ki, mask):
        start = pl.multiple_of(ki * tq, tq)
        st = lax.dot_general(k_ref[0, pl.ds(start, tq), :], qs, NT_DIMS,
                             preferred_element_type=F32)
        if mask is not None:
            st = jnp.where(mask, st, MASK_VALUE)
        m_prev = m_ref[...]
        m_new = jnp.maximum(m_prev, jnp.max(st, axis=0, keepdims=True))
        m_ref[...] = m_new
        return jnp.exp(st - m_new).astype(BF16), jnp.exp(m_prev - m_new)

    def value_stage(ki, pt, alpha):
        start = pl.multiple_of(ki * tq, tq)
        acc_ref[...] = alpha * acc_ref[...] + jnp.dot(vt_ref[:, pl.ds(start, tq)], pt,
                                                      preferred_element_type=F32)

    key = lax.broadcasted_iota(jnp.int32, (tq, 2 * tq), 0)
    col = lax.broadcasted_iota(jnp.int32, (tq, 2 * tq), 1)
    causal = key <= jnp.where(col >= tq, col - tq, col)

    @pl.when(qi == 0)
    def _():
        value_stage(0, *softmax_stage(0, causal))

    @pl.when(qi > 0)
    def _():
        p_ref[...], al_ref[...] = softmax_stage(0, None)

        def body(ki, carry):
            pt_prev = p_ref[...]
            al_prev = al_ref[...]
            pt, alpha = softmax_stage(ki, None)
            value_stage(ki - 1, pt_prev, al_prev)
            p_ref[...] = pt
            al_ref[...] = alpha
            return carry

        lax.fori_loop(1, qi, body, 0)
        pt_prev = p_ref[...]
        al_prev = al_ref[...]
        pt, alpha = softmax_stage(qi, causal)
        value_stage(qi - 1, pt_prev, al_prev)
        value_stage(qi, pt, alpha)

    a = acc_ref[...]

    ot = (a[:V_DIM, :tq] / a[V_DIM:V_DIM + 1, :tq]
          - lam * (a[:V_DIM, tq:] / a[V_DIM:V_DIM + 1, tq:]))
    o = _rms(ot.T, sg_ref[...]) * (1.0 - LAMBDA_INIT)
    o_ref[0] = o.astype(BF16)


def _attention(qkv, lq1, lk1, lq2, lk2, subln_g, tq=1024):
    bsz, s, _ = qkv.shape
    tq = min(tq, s)
    h = ATTN_HEADS
    n_acc = V_DIM + 8
    vec = lambda n: pl.BlockSpec((1, n), lambda b, hh, i: (0, 0))
    return pl.pallas_call(
        functools.partial(_attn_kernel, tq=tq),
        grid=(bsz, h, s // tq),
        in_specs=[vec(QK_DIM), vec(QK_DIM), vec(QK_DIM), vec(QK_DIM), vec(V_DIM),
                  pl.BlockSpec((1, tq, V_DIM), lambda b, hh, i: (b, i, hh)),
                  pl.BlockSpec((1, s, V_DIM), lambda b, hh, i: (b, 0, h + hh)),
                  pl.BlockSpec((1, s, V_DIM), lambda b, hh, i: (b, 0, 2 * h + hh))],
        out_specs=pl.BlockSpec((1, tq, V_DIM), lambda b, hh, i: (b, i, hh)),
        out_shape=jax.ShapeDtypeStruct((bsz, s, h * V_DIM), BF16),
        scratch_shapes=[pltpu.VMEM((n_acc, s), BF16),
                        pltpu.VMEM((1, 2 * tq), F32), pltpu.VMEM((n_acc, 2 * tq), F32),
                        pltpu.VMEM((tq, 2 * tq), BF16), pltpu.VMEM((1, 2 * tq), F32)],
        compiler_params=_cparams(("arbitrary", "arbitrary", "arbitrary"), VMEM_LIMIT),
        name="attention",
    )(lq1.reshape(1, -1), lk1.reshape(1, -1), lq2.reshape(1, -1), lk2.reshape(1, -1),
      subln_g.reshape(1, -1), qkv, qkv, qkv)


def _ssm_tables(a_re, a_im, log_dt, b_re, b_im, c_re, c_im, t):
    g, p = a_re.shape
    w = b_re.shape[-1]
    dt = jnp.exp(log_dt.astype(F32))[:, None]
    lam = lax.complex(jnp.minimum(a_re.astype(F32), -1e-4), a_im.astype(F32))
    lam_dt = lam * dt
    lam_bar = jnp.exp(lam_dt)
    b_bar = ((lam_bar - 1.0) / lam)[..., None] * lax.complex(b_re.astype(F32), b_im.astype(F32))
    c_cplx = lax.complex(c_re.astype(F32), c_im.astype(F32))
    tau = jnp.arange(t + 1, dtype=F32)
    pw = jnp.exp(lam_dt[:, None, :] * tau[None, :, None])
    ktab = jnp.einsum('gcp,gtp,gpd->gdct', c_cplx, pw[:, :t], b_bar).real.reshape(g, w * w, t)
    pw_rev = pw[:, t - 1 - jnp.arange(t)]
    pw_nxt = pw[:, 1:t + 1].transpose(0, 2, 1)
    bb = b_bar.transpose(0, 2, 1)
    cc = c_cplx.transpose(0, 2, 1)
    lt = pw[:, t][:, None, :]
    return (ktab, pw_rev.real, pw_rev.imag, pw_nxt.real, pw_nxt.imag, bb.real, bb.imag,
            cc.real, cc.imag, lt.real, lt.imag)


def _ssm_kernel(u_ref, k_ref, prr_ref, pri_ref, pnr_ref, pni_ref, bbr_ref, bbi_ref, ccr_ref, cci_ref,
                ltre_ref, ltim_ref, d_ref, y_ref,
                toep, ubuf, wre, wim, vre, vim, slre, slim, spre, spim, *, bsz, nc):
    gw = SSM_GROUP_WIDTH
    t = SSM_CHUNK
    r = bsz * nc

    jrow = lax.broadcasted_iota(jnp.int32, (t, t), 0)
    tcol = lax.broadcasted_iota(jnp.int32, (t, t), 1)
    causal = tcol >= jrow

    def build(d, carry):
        rows = pl.ds(pl.multiple_of(d * t, t), t)
        for c in range(gw):
            krow = k_ref[0, pl.ds(d * gw + c, 1), :]
            blk = pltpu.roll(jnp.broadcast_to(krow, (t, t)), 0, 1, stride=1, stride_axis=0)
            toep[rows, c * t:(c + 1) * t] = jnp.where(causal, blk, 0.0).astype(BF16)
        return carry

    lax.fori_loop(0, gw, build, 0)

    for d in range(gw):
        ubuf[:, d * t:(d + 1) * t] = u_ref[:, d].reshape(r, t).astype(BF16)
    u = ubuf[...]

    prr, pri = prr_ref[0], pri_ref[0]
    for d in range(gw):
        br, bi = bbr_ref[0, d:d + 1, :], bbi_ref[0, d:d + 1, :]
        wre[d * t:(d + 1) * t, :] = (prr * br - pri * bi).astype(BF16)
        wim[d * t:(d + 1) * t, :] = (prr * bi + pri * br).astype(BF16)
    pnr, pni = pnr_ref[0], pni_ref[0]
    for c in range(gw):
        cr, ci = ccr_ref[0, :, c:c + 1], cci_ref[0, :, c:c + 1]
        vre[:, c * t:(c + 1) * t] = (cr * pnr - ci * pni).astype(BF16)
        vim[:, c * t:(c + 1) * t] = (-(cr * pni + ci * pnr)).astype(BF16)

    slre[...] = jnp.dot(u, wre[...], preferred_element_type=F32)
    slim[...] = jnp.dot(u, wim[...], preferred_element_type=F32)
    a = ltre_ref[0]
    b = ltim_ref[0]

    def step(c, carry):
        sre, sim = carry
        rows = pl.ds(c, bsz, stride=nc)
        spre[rows, :] = sre
        spim[rows, :] = sim
        nre = a * sre - b * sim + slre[rows, :]
        nim = a * sim + b * sre + slim[rows, :]
        return nre, nim

    z = jnp.zeros((bsz, a.shape[-1]), F32)
    lax.fori_loop(0, nc, step, (z, z))
    y = jnp.dot(u, toep[...], preferred_element_type=F32)
    y = y + jnp.dot(spre[...].astype(BF16), vre[...], preferred_element_type=F32)
    y = y + jnp.dot(spim[...].astype(BF16), vim[...], preferred_element_type=F32)
    for c in range(gw):
        y_ref[:, c] = (y[:, c * t:(c + 1) * t].reshape(bsz, nc, t)
                       + d_ref[0, c:c + 1, :] * u_ref[:, c])


def _ssm(ut4, tables, d_skip):
    bsz, d_ssm, nc, t = ut4.shape
    d_tab = jnp.broadcast_to(d_skip.reshape(-1, SSM_GROUP_WIDTH, 1), (d_skip.size // SSM_GROUP_WIDTH,
                                                                    SSM_GROUP_WIDTH, t))
    gw = SSM_GROUP_WIDTH
    g = d_ssm // gw
    p = tables[-1].shape[-1]
    r = bsz * nc
    blk = lambda a, b: pl.BlockSpec((1, a, b), lambda i: (i, 0, 0))
    grp = pl.BlockSpec((bsz, gw, nc, t), lambda i: (0, i, 0, 0))
    return pl.pallas_call(
        functools.partial(_ssm_kernel, bsz=bsz, nc=nc),
        grid=(g,),
        in_specs=[grp, blk(gw * gw, t), blk(t, p), blk(t, p), blk(p, t), blk(p, t),
                  blk(gw, p), blk(gw, p), blk(p, gw), blk(p, gw), blk(1, p), blk(1, p), blk(gw, t)],
        out_specs=grp,
        out_shape=jax.ShapeDtypeStruct(ut4.shape, F32),
        scratch_shapes=[pltpu.VMEM((gw * t, gw * t), BF16), pltpu.VMEM((r, gw * t), BF16),
                        pltpu.VMEM((gw * t, p), BF16), pltpu.VMEM((gw * t, p), BF16),
                        pltpu.VMEM((p, gw * t), BF16), pltpu.VMEM((p, gw * t), BF16)]
        + [pltpu.VMEM((r, p), F32)] * 4,
        compiler_params=_cparams(("arbitrary",), VMEM_LIMIT),
        name="ssm",
    )(ut4, *tables, d_tab)


def _mix_kernel(x_ref, attn_ref, yt_ref, mod_ref, wglut_ref, bglu_ref, wout_ref,
                g2_ref, wrt_ref, br_ref, x1_ref, h2_ref, lgt_ref, *, d_attn):
    nck = yt_ref.shape[2]
    y3 = pltpu.einshape("mcl->cml", yt_ref[0])
    y = jnp.concatenate([y3[c] for c in range(nck)], axis=1)
    z = 0.5 * y * (1.0 + jnp.tanh(math.sqrt(2.0 / math.pi) * (y + 0.044715 * (y * y * y))))
    gl = jnp.dot(wglut_ref[...], z.astype(BF16), preferred_element_type=F32) + bglu_ref[...]
    so = (z * _sigmoid(gl)).T
    mix = (jnp.dot(attn_ref[0], wout_ref[:d_attn, :], preferred_element_type=F32)
           + jnp.dot(so.astype(BF16), wout_ref[d_attn:, :], preferred_element_type=F32))
    x1 = x_ref[0] + mod_ref[0, 2:3, :] * mix
    x1_ref[0] = x1
    h2 = _rms(x1, g2_ref[...]) * (1.0 + mod_ref[0, 4:5, :]) + mod_ref[0, 3:4, :]
    for j in range(SUBLANES):
        h2_ref[pl.ds(j, h2.shape[0], stride=SUBLANES), :] = h2[:, j * LANES:(j + 1) * LANES]
    def split(v):
        hi = v.astype(BF16)
        return hi, (v - hi.astype(F32)).astype(BF16)

    w_hi, w_lo = split(wrt_ref[...])
    h_hi, h_lo = split(h2)
    nt = functools.partial(lax.dot_general, dimension_numbers=NT_DIMS, preferred_element_type=F32)
    lgt_ref[...] = nt(w_hi, h_hi) + (nt(w_hi, h_lo) + nt(w_lo, h_hi)) + br_ref[...]


def _mix(x, attn, yt4, mod3, wglu_t, bglu, wout, g2, wr, br, tm=1024):
    bsz, s, d = x.shape
    tm = min(tm, s)
    da = attn.shape[-1]
    ds_ = yt4.shape[1]
    e = wr.shape[-1]
    nt = s // tm
    tok = lambda n: pl.BlockSpec((1, tm, n), lambda b, i: (b, i, 0))
    chan = pl.BlockSpec((1, ds_, tm // SSM_CHUNK, SSM_CHUNK), lambda b, i: (b, 0, i, 0))
    full = lambda a, b_: pl.BlockSpec((a, b_), lambda b, i: (0, 0))
    return pl.pallas_call(
        functools.partial(_mix_kernel, d_attn=da),
        grid=(bsz, nt),
        in_specs=[tok(d), tok(da), chan,
                  pl.BlockSpec((1, 6, d), lambda b, i: (b, 0, 0)),
                  full(ds_, ds_), full(ds_, 1), full(da + ds_, d),
                  full(1, d), full(e, d), full(e, 1)],
        out_specs=[tok(d), pl.BlockSpec((tm * SUBLANES, LANES), lambda b, i: (b * nt + i, 0)),
                   pl.BlockSpec((e, tm), lambda b, i: (0, b * nt + i))],
        out_shape=[jax.ShapeDtypeStruct((bsz, s, d), F32),
                   jax.ShapeDtypeStruct((bsz * s * SUBLANES, LANES), F32),
                   jax.ShapeDtypeStruct((e, bsz * s), F32)],
        compiler_params=_cparams(("arbitrary", "arbitrary"), VMEM_LIMIT),
        name="mix",
    )(x, attn, yt4, mod3, wglu_t, bglu.reshape(-1, 1), wout,
      g2.reshape(1, -1), wr.T, br.reshape(-1, 1))


def _route_kernel(lg_ref, idx_ref, gate_ref, rank_ref, cnt_ref, run_ref, *, tm):
    i = pl.program_id(0)

    @pl.when(i == 0)
    def _():
        run_ref[...] = jnp.zeros_like(run_ref)

    l = lg_ref[...]
    e = l.shape[0]
    sub = lax.broadcasted_iota(jnp.int32, l.shape, 0)
    vals, sels, idxs = [], [], []
    for _k in range(TOP_K):
        mx = jnp.max(l, axis=0, keepdims=True)
        ix = jnp.min(jnp.where(l == mx, sub, e), axis=0, keepdims=True)
        sel = sub == ix
        vals.append(mx)
        idxs.append(ix)
        sels.append(sel)
        l = jnp.where(sel, -jnp.inf, l)
    ex = [jnp.exp(v - vals[0]) for v in vals]
    den = ex[0] + ex[1] + ex[2] + ex[3]
    chosen = jnp.zeros(l.shape, F32)
    for sel in sels:
        chosen = chosen + jnp.where(sel, 1.0, 0.0)
    r_i = lax.broadcasted_iota(jnp.int32, (tm, tm), 0)
    c_i = lax.broadcasted_iota(jnp.int32, (tm, tm), 1)
    tri = jnp.where(r_i < c_i, 1.0, 0.0).astype(BF16)
    before = jnp.dot(chosen.astype(BF16), tri, preferred_element_type=F32) + run_ref[...]
    ksub = lax.broadcasted_iota(jnp.int32, (TOP_K, tm), 0)
    idx_o = jnp.zeros((TOP_K, tm), jnp.int32)
    gate_o = jnp.zeros((TOP_K, tm), F32)
    rank_o = jnp.zeros((TOP_K, tm), F32)
    for k in range(TOP_K):
        rk = jnp.sum(jnp.where(sels[k], before, 0.0), axis=0, keepdims=True)
        idx_o = jnp.where(ksub == k, idxs[k], idx_o)
        gate_o = jnp.where(ksub == k, ex[k] / den, gate_o)
        rank_o = jnp.where(ksub == k, rk, rank_o)
    idx_ref[...] = idx_o
    gate_ref[...] = gate_o
    rank_ref[...] = rank_o.astype(jnp.int32)
    run_ref[...] = run_ref[...] + jnp.sum(chosen, axis=1, keepdims=True)
    cnt_ref[...] = run_ref[...].astype(jnp.int32)


def _route(logits_t, tm=512):
    e, n = logits_t.shape
    tok = lambda w: pl.BlockSpec((w, tm), lambda i: (0, i))
    return pl.pallas_call(
        functools.partial(_route_kernel, tm=tm),
        grid=(n // tm,),
        in_specs=[tok(e)],
        out_specs=[tok(TOP_K), tok(TOP_K), tok(TOP_K), pl.BlockSpec((e, 1), lambda i: (0, 0))],
        out_shape=[jax.ShapeDtypeStruct((TOP_K, n), jnp.int32),
                   jax.ShapeDtypeStruct((TOP_K, n), F32),
                   jax.ShapeDtypeStruct((TOP_K, n), jnp.int32),
                   jax.ShapeDtypeStruct((e, 1), jnp.int32)],
        scratch_shapes=[pltpu.VMEM((e, 1), F32)],
        compiler_params=_cparams(("arbitrary",)),
        name="route",
    )(logits_t)


def _dispatch_kernel(pend_ref, padded_ref, nb_ref, pos_ref, h_ref, xs_ref, zero_ref, sem, zsem,
                     *, tm, rb, n_exp, nb_max):
    i = pl.program_id(0)

    def zero_copy(blk_start):
        rows = pl.ds(pl.multiple_of(blk_start * SUBLANES, rb * SUBLANES), rb * SUBLANES)
        return pltpu.make_async_copy(zero_ref, xs_ref.at[rows, :], zsem)

    @pl.when(i == 0)
    def _():
        zero_ref[...] = jnp.zeros_like(zero_ref)
        for wait in (False, True):
            for e in range(n_exp):
                @pl.when(padded_ref[e] > 0)
                def _():
                    cp = zero_copy(pend_ref[e] - rb)
                    cp.wait() if wait else cp.start()

                @pl.when(nb_ref[0] + e < nb_max)
                def _():
                    cp = zero_copy((nb_ref[0] + e) * rb)
                    cp.wait() if wait else cp.start()

    @pl.when(i > 0)
    def _():
        def issue(t, _):
            for k in range(TOP_K):
                p = pos_ref[0, 0, k * tm + t]
                pltpu.make_async_copy(h_ref.at[_tile_rows(t), :], xs_ref.at[_tile_rows(p), :],
                                      sem).start(priority=k % 2)
            return 0

        lax.fori_loop(0, tm, issue, 0, unroll=4)
        for k in range(TOP_K):
            pltpu.make_async_copy(h_ref, xs_ref.at[pl.ds(0, tm * SUBLANES), :], sem).wait()


def _dispatch(pend, padded, nblk, pos3, h2, n_rows, tm, rb):
    n = h2.shape[0] // SUBLANES
    n_exp = pend.shape[0]
    tile = lambda i, *_: (jnp.maximum(i - 1, 0), 0)
    return pl.pallas_call(
        functools.partial(_dispatch_kernel, tm=tm, rb=rb, n_exp=n_exp, nb_max=n_rows // rb),
        grid_spec=pltpu.PrefetchScalarGridSpec(
            num_scalar_prefetch=3,
            grid=(1 + n // tm,),
            in_specs=[pl.BlockSpec((1, 1, tm * TOP_K), lambda i, *_: tile(i) + (0,),
                                   memory_space=pltpu.SMEM),
                      pl.BlockSpec((tm * SUBLANES, LANES), tile)],
            out_specs=pl.BlockSpec(memory_space=pl.ANY),
            scratch_shapes=[pltpu.VMEM((rb * SUBLANES, LANES), F32), pltpu.SemaphoreType.DMA,
                            pltpu.SemaphoreType.DMA]),
        out_shape=jax.ShapeDtypeStruct((n_rows * SUBLANES, LANES), F32),
        compiler_params=_cparams(("arbitrary",)),
        name="dispatch",
    )(pend, padded, nblk, pos3, h2)


def _experts_kernel(be_ref, xb_ref, nb_ref, seg_ref, ne_ref, x_ref, w1_ref, b1_ref, w2_ref, b2_ref,
                    y_ref, w1f, w2f, w1b, w2b, sems, *, f):
    i = pl.program_id(0)
    prev = be_ref[jnp.maximum(i - 1, 0)]
    live = i < nb_ref[0]
    slot = lax.rem(seg_ref[i], 2)

    def fetch(e, s):
        return (pltpu.make_async_copy(w1_ref.at[e], w1f.at[s], sems.at[s, 0]),
                pltpu.make_async_copy(w2_ref.at[e], w2f.at[s], sems.at[s, 1]))

    @pl.when(i == 0)
    def _():
        for cp in fetch(be_ref[0], 0):
            cp.start()

    @pl.when(live & ((i == 0) | (be_ref[i] != prev)))
    def _():
        for cp in fetch(be_ref[i], slot):
            cp.wait()
        w1b[...] = w1f[slot].astype(BF16)
        w2b[...] = w2f[slot].astype(BF16)

        @pl.when(ne_ref[i] >= 0)
        def _():
            for cp in fetch(ne_ref[i], 1 - slot):
                cp.start()

    @pl.when(live)
    def _():
        rb = x_ref.shape[0] // SUBLANES
        x = jnp.concatenate([x_ref[pl.ds(j, rb, stride=SUBLANES), :].astype(BF16)
                             for j in range(SUBLANES)], axis=1)
        gu = jnp.dot(x, w1b[...], preferred_element_type=F32) + b1_ref[0]
        gate = jnp.minimum(gu[:, :f], SWIGLU_LIMIT)
        up = jnp.clip(gu[:, f:], -SWIGLU_LIMIT, SWIGLU_LIMIT)
        glu = gate * _sigmoid(SWIGLU_ALPHA * gate)
        hmid = ((up + 1.0) * glu).astype(BF16)
        y = jnp.dot(hmid, w2b[...], preferred_element_type=F32) + b2_ref[0]
        for j in range(SUBLANES):
            y_ref[pl.ds(j, rb, stride=SUBLANES), :] = y[:, j * LANES:(j + 1) * LANES]

    @pl.when(jnp.logical_not(live))
    def _():
        y_ref[...] = jnp.zeros_like(y_ref)


def _experts(block_e, xblk, nblk, seg, nxt_e, xs, w1, b1, w2, b2, rb):
    e, d, f2 = w1.shape
    assert d == SUBLANES * LANES, "token-tile layout holds one row per (8, 128) f32 tile"
    f = f2 // 2
    nb_max = xs.shape[0] // (rb * SUBLANES)
    by_expert = lambda i, be, *_: (be[i], 0, 0)
    return pl.pallas_call(
        functools.partial(_experts_kernel, f=f),
        grid_spec=pltpu.PrefetchScalarGridSpec(
            num_scalar_prefetch=5,
            grid=(nb_max,),
            in_specs=[pl.BlockSpec((rb * SUBLANES, LANES), lambda i, be, xb, *_: (xb[i], 0)),
                      pl.BlockSpec(memory_space=pl.ANY),
                      pl.BlockSpec((1, 1, f2), by_expert),
                      pl.BlockSpec(memory_space=pl.ANY),
                      pl.BlockSpec((1, 1, d), by_expert)],
            out_specs=pl.BlockSpec((rb * SUBLANES, LANES), lambda i, *_: (i, 0)),
            scratch_shapes=[pltpu.VMEM((2, d, f2), F32), pltpu.VMEM((2, f, d), F32),
                            pltpu.VMEM((d, f2), BF16), pltpu.VMEM((f, d), BF16),
                            pltpu.SemaphoreType.DMA((2, 2))]),
        out_shape=jax.ShapeDtypeStruct(xs.shape, F32),
        compiler_params=_cparams(("arbitrary",), VMEM_LIMIT),
        name="experts",
    )(block_e, xblk, nblk, seg, nxt_e, xs, w1, b1.reshape(e, 1, f2), w2, b2.reshape(e, 1, d))


def _combine_kernel(pos_ref, nxt_ref, gate_ref, x1_ref, mod_ref, fg_ref, ys_ref, o_ref, buf, sems,
                    *, tm):
    i = pl.program_id(0)
    n_tiles = pl.num_programs(0)
    slot = lax.rem(i, 2)

    def gather(p_ref, dst_slot):
        def issue(t, _):
            for k in range(TOP_K):
                p = p_ref[0, 0, k * tm + t]
                pltpu.make_async_copy(ys_ref.at[_tile_rows(p), :],
                                      buf.at[dst_slot, k, _tile_rows(t), :],
                                      sems.at[dst_slot]).start(priority=k % 2)
            return 0

        lax.fori_loop(0, tm, issue, 0, unroll=4)

    @pl.when(i == 0)
    def _():
        gather(pos_ref, 0)

    @pl.when(i + 1 < n_tiles)
    def _():
        gather(nxt_ref, 1 - slot)

    for k in range(TOP_K):
        pltpu.make_async_copy(ys_ref.at[pl.ds(0, tm * SUBLANES), :], buf.at[slot, k],
                              sems.at[slot]).wait()
    gate = gate_ref[...]
    pieces = []
    for j in range(SUBLANES):
        acc = gate[:, 0:1] * buf[slot, 0, pl.ds(j, tm, stride=SUBLANES), :]
        for k in range(1, TOP_K):
            acc = acc + gate[:, k:k + 1] * buf[slot, k, pl.ds(j, tm, stride=SUBLANES), :]
        pieces.append(acc)
    moe = jnp.concatenate(pieces, axis=1)
    x2 = x1_ref[...] + mod_ref[0, 5:6, :] * moe
    o_ref[...] = _rms(x2, fg_ref[...])


def _combine(pos3, gates, x1, mod3, final_g, ys, tm):
    bsz, s, d = x1.shape
    n = bsz * s
    nt = s // tm
    n_tiles = n // tm
    pos_spec = lambda f: pl.BlockSpec((1, 1, tm * TOP_K), lambda i: (f(i), 0, 0),
                                      memory_space=pltpu.SMEM)
    out = pl.pallas_call(
        functools.partial(_combine_kernel, tm=tm),
        grid=(n_tiles,),
        in_specs=[pos_spec(lambda i: i),
                  pos_spec(lambda i: jnp.minimum(i + 1, n_tiles - 1)),
                  pl.BlockSpec((tm, TOP_K), lambda i: (i, 0)),
                  pl.BlockSpec((tm, d), lambda i: (i, 0)),
                  pl.BlockSpec((1, 6, d), lambda i: (i // nt, 0, 0)),
                  pl.BlockSpec((1, d), lambda i: (0, 0)),
                  pl.BlockSpec(memory_space=pl.ANY)],
        out_specs=pl.BlockSpec((tm, d), lambda i: (i, 0)),
        out_shape=jax.ShapeDtypeStruct((n, d), F32),
        scratch_shapes=[pltpu.VMEM((2, TOP_K, tm * SUBLANES, LANES), F32),
                        pltpu.SemaphoreType.DMA((2,))],
        compiler_params=_cparams(("arbitrary",), VMEM_LIMIT),
        name="combine",
    )(pos3, pos3, gates, x1.reshape(n, d), mod3, final_g.reshape(1, d), ys)
    return out.reshape(bsz, s, d)


def _layer(x, mod3, norm1_g, w_in, lq1, lk1, lq2, lk2, subln_g, ssm_a_re, ssm_a_im, ssm_log_dt,
           ssm_b_re, ssm_b_im, ssm_c_re, ssm_c_im, ssm_d, w_glu, b_glu, w_out, norm2_g,
           w_router, b_router, w1, b1, w2, b2, final_g):
    bsz, s, d = x.shape
    n = bsz * s
    d_attn = ATTN_HEADS * V_DIM
    n_qkv = 3 * d_attn

    qkv, ut4 = _inproj(x, mod3, norm1_g, w_in[:, :n_qkv].astype(BF16),
                       w_in[:, n_qkv:].T.astype(BF16))
    attn = _attention(qkv, lq1, lk1, lq2, lk2, subln_g)
    tables = _ssm_tables(ssm_a_re, ssm_a_im, ssm_log_dt, ssm_b_re, ssm_b_im, ssm_c_re, ssm_c_im,
                         SSM_CHUNK)
    yt4 = _ssm(ut4, tables, ssm_d)
    x1, h2, logits_t = _mix(x, attn, yt4, mod3, w_glu.T.astype(BF16), b_glu,
                            w_out.astype(BF16), norm2_g, w_router, b_router)

    idx, gates, rank, counts = _route(logits_t)
    rb = 512 if n * TOP_K >= 512 * N_EXPERTS else 128
    counts = counts.reshape(N_EXPERTS)
    padded = ((counts + rb - 1) // rb) * rb
    pend = jnp.cumsum(padded).astype(jnp.int32)
    pstart = pend - padded
    eids = jnp.arange(N_EXPERTS, dtype=jnp.int32)[:, None, None]
    pos = rank + jnp.sum(jnp.where(idx[None] == eids, pstart[:, None, None], 0), axis=0)
    pos = pos.astype(jnp.int32)
    nb_max = (n * TOP_K) // rb + N_EXPERTS
    n_rows = nb_max * rb
    nblk = pend[-1] // rb
    blk_ids = jnp.minimum(jnp.arange(nb_max, dtype=jnp.int32), nblk - 1)
    block_e = jnp.minimum(jnp.sum((pend[None, :] <= (blk_ids * rb)[:, None]).astype(jnp.int32), axis=1),
                          N_EXPERTS - 1).astype(jnp.int32)
    first = jnp.concatenate([jnp.ones((1,), jnp.int32),
                             (block_e[1:] != block_e[:-1]).astype(jnp.int32)])
    seg = (jnp.cumsum(first) - 1).astype(jnp.int32)
    seg_end = jnp.sum(jnp.where(block_e[:, None] == jnp.arange(N_EXPERTS, dtype=jnp.int32)[None, :],
                                pend[None, :], 0), axis=1) // rb
    nxt_e = jnp.where(seg_end < nblk, block_e[jnp.minimum(seg_end, nb_max - 1)], -1).astype(jnp.int32)
    tmd = 512
    pos3 = (pos.reshape(TOP_K, n // tmd, tmd).transpose(1, 0, 2)
            .reshape(n // tmd, 1, TOP_K * tmd))
    gates_tok = gates.T

    nblk = nblk.reshape(1).astype(jnp.int32)
    xs = _dispatch(pend, padded.astype(jnp.int32), nblk, pos3, h2, n_rows, tmd, rb)
    ys = _experts(block_e, blk_ids, nblk, seg, nxt_e, xs, w1, b1, w2, b2, rb)
    return _combine(pos3, gates_tok, x1, mod3, final_g, ys, tmd)


def kernel(x, c, w_ada, b_ada, norm1_g, w_in, lq1, lk1, lq2, lk2, subln_g, ssm_a_re, ssm_a_im,
           ssm_log_dt, ssm_b_re, ssm_b_im, ssm_c_re, ssm_c_im, ssm_d, w_glu, b_glu, w_out, norm2_g,
           w_router, b_router, w1, b1, w2, b2, final_g):
    assert w_ada.shape[0] == 1, "single-layer block"
    bsz, s, d = x.shape
    mod3 = _adaln(c, w_ada[0], b_ada[0]).reshape(bsz, 6, d)
    return _layer(x, mod3, norm1_g[0], w_in[0], lq1[0], lk1[0], lq2[0], lk2[0], subln_g[0],
                  ssm_a_re[0], ssm_a_im[0], ssm_log_dt[0], ssm_b_re[0], ssm_b_im[0], ssm_c_re[0],
                  ssm_c_im[0], ssm_d[0], w_glu[0], b_glu[0], w_out[0], norm2_g[0], w_router[0],
                  b_router[0], w1[0], b1[0], w2[0], b2[0], final_g)
```

```python
import functools
import math

import jax
import jax.numpy as jnp
from jax import lax
from jax.experimental import pallas as pl
from jax.experimental.pallas import tpu as pltpu

F32 = jnp.float32
BF16 = jnp.bfloat16
HIGHEST = lax.Precision.HIGHEST

RMS_EPS = 1e-6
MASK_VALUE = -1e30
ATTN_HEADS = 4
QK_DIM = 64
V_DIM = 128
SSM_GROUP_WIDTH = 16
SSM_CHUNK = 128
N_EXPERTS = 32
TOP_K = 4
SWIGLU_LIMIT = 7.0
SWIGLU_ALPHA = 1.702
LAMBDA_INIT = 0.8 - 0.6 * math.exp(-0.3 * 0)

VMEM_LIMIT = 56 * 1024 * 1024
NT_DIMS = (((1,), (1,)), ((), ()))
LANES = 128
SUBLANES = 8


def _tile_rows(t):
    return pl.ds(pl.multiple_of(t * SUBLANES, SUBLANES), SUBLANES)


def _cparams(sem, vmem=None):
    return pltpu.CompilerParams(dimension_semantics=sem, vmem_limit_bytes=vmem)


def _sigmoid(x):
    return 1.0 / (1.0 + jnp.exp(-x))


def _rms(x, g):
    ms = jnp.mean(x * x, axis=-1, keepdims=True)
    return x * lax.rsqrt(ms + RMS_EPS) * g


def _adaln_kernel(c_ref, w_ref, b_ref, o_ref):
    c = c_ref[...]
    ca = c * _sigmoid(c)
    o_ref[...] = jnp.dot(ca, w_ref[...], preferred_element_type=F32, precision=HIGHEST) + b_ref[...]


def _adaln(c, w, b):
    bsz, d = c.shape
    n = w.shape[1]
    tn = 1536
    return pl.pallas_call(
        _adaln_kernel,
        grid=(n // tn,),
        in_specs=[pl.BlockSpec((bsz, d), lambda j: (0, 0)),
                  pl.BlockSpec((d, tn), lambda j: (0, j)),
                  pl.BlockSpec((1, tn), lambda j: (0, j))],
        out_specs=pl.BlockSpec((bsz, tn), lambda j: (0, j)),
        out_shape=jax.ShapeDtypeStruct((bsz, n), F32),
        compiler_params=_cparams(("arbitrary",)),
        name="adaln",
    )(c, w, b.reshape(1, n))


def _inproj_kernel(x_ref, mod_ref, g_ref, wq_ref, wut_ref, qkv_ref, ut_ref):
    x = x_ref[0]
    y = _rms(x, g_ref[...])
    h = (y * (1.0 + mod_ref[0, 1:2, :]) + mod_ref[0, 0:1, :]).astype(BF16)
    qkv_ref[0] = jnp.dot(h, wq_ref[...], preferred_element_type=F32).astype(BF16)
    ut = lax.dot_general(wut_ref[...], h, NT_DIMS, preferred_element_type=F32)
    ut_ref[0] = pltpu.einshape("m(cl)->mcl", ut, l=SSM_CHUNK)


def _inproj(x, mod3, g, wq_bf16, wut_bf16, tm=1024):
    bsz, s, d = x.shape
    tm = min(tm, s)
    n_qkv = wq_bf16.shape[1]
    n_u = wut_bf16.shape[0]
    nck = tm // SSM_CHUNK
    return pl.pallas_call(
        _inproj_kernel,
        grid=(bsz, s // tm),
        in_specs=[pl.BlockSpec((1, tm, d), lambda b, i: (b, i, 0)),
                  pl.BlockSpec((1, 6, d), lambda b, i: (b, 0, 0)),
                  pl.BlockSpec((1, d), lambda b, i: (0, 0)),
                  pl.BlockSpec((d, n_qkv), lambda b, i: (0, 0)),
                  pl.BlockSpec((n_u, d), lambda b, i: (0, 0))],
        out_specs=[pl.BlockSpec((1, tm, n_qkv), lambda b, i: (b, i, 0)),
                   pl.BlockSpec((1, n_u, nck, SSM_CHUNK), lambda b, i: (b, 0, i, 0))],
        out_shape=[jax.ShapeDtypeStruct((bsz, s, n_qkv), BF16),
                   jax.ShapeDtypeStruct((bsz, n_u, s // SSM_CHUNK, SSM_CHUNK), F32)],
        compiler_params=_cparams(("arbitrary", "arbitrary"), VMEM_LIMIT),
        name="inproj",
    )(x, mod3, g.reshape(1, d), wq_bf16, wut_bf16)


def _attn_kernel(lq1_ref, lk1_ref, lq2_ref, lk2_ref, sg_ref, q_ref, k_ref, v_ref, o_ref,
                 vt_ref, m_ref, acc_ref, p_ref, al_ref, *, tq):
    qi = pl.program_id(2)
    n_acc = acc_ref.shape[0]

    @pl.when(qi == 0)
    def _():
        vt_ref[:V_DIM, :] = v_ref[0].astype(F32).T.astype(BF16)
        vt_ref[V_DIM:, :] = jnp.ones((n_acc - V_DIM, vt_ref.shape[1]), BF16)

    lam = (jnp.exp(jnp.sum(lq1_ref[...] * lk1_ref[...], axis=-1, keepdims=True))
           - jnp.exp(jnp.sum(lq2_ref[...] * lk2_ref[...], axis=-1, keepdims=True))
           + LAMBDA_INIT)
    q = q_ref[0] * (QK_DIM ** -0.5)
    lane = lax.broadcasted_iota(jnp.int32, q.shape, 1)
    zero = jnp.zeros_like(q)
    qs = jnp.concatenate([jnp.where(lane < QK_DIM, q, zero), jnp.where(lane >= QK_DIM, q, zero)],
                         axis=0)
    m_ref[...] = jnp.full(m_ref.shape, MASK_VALUE, F32)
    acc_ref[...] = jnp.zeros(acc_ref.shape, F32)

    def softmax_stage(ki, mask):
        start = pl.multiple_of(ki * tq, tq)
        st = lax.dot_general(k_ref[0, pl.ds(start, tq), :], qs, NT_DIMS,
                             preferred_element_type=F32)
        if mask is not None:
            st = jnp.where(mask, st, MASK_VALUE)
        m_prev = m_ref[...]
        m_new = jnp.maximum(m_prev, jnp.max(st, axis=0, keepdims=True))
        m_ref[...] = m_new
        return jnp.exp(st - m_new).astype(BF16), jnp.exp(m_prev - m_new)

    def value_stage(ki, pt, alpha):
        start = pl.multiple_of(ki * tq, tq)
        acc_ref[...] = alpha * acc_ref[...] + jnp.dot(vt_ref[:, pl.ds(start, tq)], pt,
                                                      preferred_element_type=F32)

    key = lax.broadcasted_iota(jnp.int32, (tq, 2 * tq), 0)
    col = lax.broadcasted_iota(jnp.int32, (tq, 2 * tq), 1)
    causal = key <= jnp.where(col >= tq, col - tq, col)

    @pl.when(qi == 0)
    def _():
        value_stage(0, *softmax_stage(0, causal))

    @pl.when(qi > 0)
    def _():
        p_ref[...], al_ref[...] = softmax_stage(0, None)

        def body(ki, carry):
            pt_prev = p_ref[...]
            al_prev = al_ref[...]
            pt, alpha = softmax_stage(ki, None)
            value_stage(ki - 1, pt_prev, al_prev)
            p_ref[...] = pt
            al_ref[...] = alpha
            return carry

        lax.fori_loop(1, qi, body, 0)
        pt_prev = p_ref[...]
        al_prev = al_ref[...]
        pt, alpha = softmax_stage(qi, causal)
        value_stage(qi - 1, pt_prev, al_prev)
        value_stage(qi, pt, alpha)

    a = acc_ref[...]

    ot = (a[:V_DIM, :tq] / a[V_DIM:V_DIM + 1, :tq]
          - lam * (a[:V_DIM, tq:] / a[V_DIM:V_DIM + 1, tq:]))
    o = _rms(ot.T, sg_ref[...]) * (1.0 - LAMBDA_INIT)
    o_ref[0] = o.astype(BF16)


def _attention(qkv, lq1, lk1, lq2, lk2, subln_g, tq=1024):
    bsz, s, _ = qkv.shape
    tq = min(tq, s)
    h = ATTN_HEADS
    n_acc = V_DIM + 8
    vec = lambda n: pl.BlockSpec((1, n), lambda b, hh, i: (0, 0))
    return pl.pallas_call(
        functools.partial(_attn_kernel, tq=tq),
        grid=(bsz, h, s // tq),
        in_specs=[vec(QK_DIM), vec(QK_DIM), vec(QK_DIM), vec(QK_DIM), vec(V_DIM),
                  pl.BlockSpec((1, tq, V_DIM), lambda b, hh, i: (b, i, hh)),
                  pl.BlockSpec((1, s, V_DIM), lambda b, hh, i: (b, 0, h + hh)),
                  pl.BlockSpec((1, s, V_DIM), lambda b, hh, i: (b, 0, 2 * h + hh))],
        out_specs=pl.BlockSpec((1, tq, V_DIM), lambda b, hh, i: (b, i, hh)),
        out_shape=jax.ShapeDtypeStruct((bsz, s, h * V_DIM), BF16),
        scratch_shapes=[pltpu.VMEM((n_acc, s), BF16),
                        pltpu.VMEM((1, 2 * tq), F32), pltpu.VMEM((n_acc, 2 * tq), F32),
                        pltpu.VMEM((tq, 2 * tq), BF16), pltpu.VMEM((1, 2 * tq), F32)],
        compiler_params=_cparams(("arbitrary", "arbitrary", "arbitrary"), VMEM_LIMIT),
        name="attention",
    )(lq1.reshape(1, -1), lk1.reshape(1, -1), lq2.reshape(1, -1), lk2.reshape(1, -1),
      subln_g.reshape(1, -1), qkv, qkv, qkv)


def _ssm_tables(a_re, a_im, log_dt, b_re, b_im, c_re, c_im, t):
    g, p = a_re.shape
    w = b_re.shape[-1]
    dt = jnp.exp(log_dt.astype(F32))[:, None]
    lam = lax.complex(jnp.minimum(a_re.astype(F32), -1e-4), a_im.astype(F32))
    lam_dt = lam * dt
    lam_bar = jnp.exp(lam_dt)
    b_bar = ((lam_bar - 1.0) / lam)[..., None] * lax.complex(b_re.astype(F32), b_im.astype(F32))
    c_cplx = lax.complex(c_re.astype(F32), c_im.astype(F32))
    tau = jnp.arange(t + 1, dtype=F32)
    pw = jnp.exp(lam_dt[:, None, :] * tau[None, :, None])
    ktab = jnp.einsum('gcp,gtp,gpd->gdct', c_cplx, pw[:, :t], b_bar).real.reshape(g, w * w, t)
    pw_rev = pw[:, t - 1 - jnp.arange(t)]
    pw_nxt = pw[:, 1:t + 1].transpose(0, 2, 1)
    bb = b_bar.transpose(0, 2, 1)
    cc = c_cplx.transpose(0, 2, 1)
    lt = pw[:, t][:, None, :]
    return (ktab, pw_rev.real, pw_rev.imag, pw_nxt.real, pw_nxt.imag, bb.real, bb.imag,
            cc.real, cc.imag, lt.real, lt.imag)


def _ssm_kernel(u_ref, k_ref, prr_ref, pri_ref, pnr_ref, pni_ref, bbr_ref, bbi_ref, ccr_ref, cci_ref,
                ltre_ref, ltim_ref, d_ref, y_ref,
                toep, ubuf, wre, wim, vre, vim, slre, slim, spre, spim, *, bsz, nc):
    gw = SSM_GROUP_WIDTH
    t = SSM_CHUNK
    r = bsz * nc

    jrow = lax.broadcasted_iota(jnp.int32, (t, t), 0)
    tcol = lax.broadcasted_iota(jnp.int32, (t, t), 1)
    causal = tcol >= jrow

    def build(d, carry):
        rows = pl.ds(pl.multiple_of(d * t, t), t)
        for c in range(gw):
            krow = k_ref[0, pl.ds(d * gw + c, 1), :]
            blk = pltpu.roll(jnp.broadcast_to(krow, (t, t)), 0, 1, stride=1, stride_axis=0)
            toep[rows, c * t:(c + 1) * t] = jnp.where(causal, blk, 0.0).astype(BF16)
        return carry

    lax.fori_loop(0, gw, build, 0)

    for d in range(gw):
        ubuf[:, d * t:(d + 1) * t] = u_ref[:, d].reshape(r, t).astype(BF16)
    u = ubuf[...]

    prr, pri = prr_ref[0], pri_ref[0]
    for d in range(gw):
        br, bi = bbr_ref[0, d:d + 1, :], bbi_ref[0, d:d + 1, :]
        wre[d * t:(d + 1) * t, :] = (prr * br - pri * bi).astype(BF16)
        wim[d * t:(d + 1) * t, :] = (prr * bi + pri * br).astype(BF16)
    pnr, pni = pnr_ref[0], pni_ref[0]
    for c in range(gw):
        cr, ci = ccr_ref[0, :, c:c + 1], cci_ref[0, :, c:c + 1]
        vre[:, c * t:(c + 1) * t] = (cr * pnr - ci * pni).astype(BF16)
        vim[:, c * t:(c + 1) * t] = (-(cr * pni + ci * pnr)).astype(BF16)

    slre[...] = jnp.dot(u, wre[...], preferred_element_type=F32)
    slim[...] = jnp.dot(u, wim[...], preferred_element_type=F32)
    a = ltre_ref[0]
    b = ltim_ref[0]

    def step(c, carry):
        sre, sim = carry
        rows = pl.ds(c, bsz, stride=nc)
        spre[rows, :] = sre
        spim[rows, :] = sim
        nre = a * sre - b * sim + slre[rows, :]
        nim = a * sim + b * sre + slim[rows, :]
        return nre, nim

    z = jnp.zeros((bsz, a.shape[-1]), F32)
    lax.fori_loop(0, nc, step, (z, z))
    y = jnp.dot(u, toep[...], preferred_element_type=F32)
    y = y + jnp.dot(spre[...].astype(BF16), vre[...], preferred_element_type=F32)
    y = y + jnp.dot(spim[...].astype(BF16), vim[...], preferred_element_type=F32)
    for c in range(gw):
        y_ref[:, c] = (y[:, c * t:(c + 1) * t].reshape(bsz, nc, t)
                       + d_ref[0, c:c + 1, :] * u_ref[:, c])


def _ssm(ut4, tables, d_skip):
    bsz, d_ssm, nc, t = ut4.shape
    d_tab = jnp.broadcast_to(d_skip.reshape(-1, SSM_GROUP_WIDTH, 1), (d_skip.size // SSM_GROUP_WIDTH,
                                                                    SSM_GROUP_WIDTH, t))
    gw = SSM_GROUP_WIDTH
    g = d_ssm // gw
    p = tables[-1].shape[-1]
    r = bsz * nc
    blk = lambda a, b: pl.BlockSpec((1, a, b), lambda i: (i, 0, 0))
    grp = pl.BlockSpec((bsz, gw, nc, t), lambda i: (0, i, 0, 0))
    return pl.pallas_call(
        functools.partial(_ssm_kernel, bsz=bsz, nc=nc),
        grid=(g,),
        in_specs=[grp, blk(gw * gw, t), blk(t, p), blk(t, p), blk(p, t), blk(p, t),
                  blk(gw, p), blk(gw, p), blk(p, gw), blk(p, gw), blk(1, p), blk(1, p), blk(gw, t)],
        out_specs=grp,
        out_shape=jax.ShapeDtypeStruct(ut4.shape, F32),
        scratch_shapes=[pltpu.VMEM((gw * t, gw * t), BF16), pltpu.VMEM((r, gw * t), BF16),
                        pltpu.VMEM((gw * t, p), BF16), pltpu.VMEM((gw * t, p), BF16),
                        pltpu.VMEM((p, gw * t), BF16), pltpu.VMEM((p, gw * t), BF16)]
        + [pltpu.VMEM((r, p), F32)] * 4,
        compiler_params=_cparams(("arbitrary",), VMEM_LIMIT),
        name="ssm",
    )(ut4, *tables, d_tab)


def _mix_kernel(x_ref, attn_ref, yt_ref, mod_ref, wglut_ref, bglu_ref, wout_ref,
                g2_ref, wrt_ref, br_ref, x1_ref, h2_ref, lgt_ref, *, d_attn):
    nck = yt_ref.shape[2]
    y3 = pltpu.einshape("mcl->cml", yt_ref[0])
    y = jnp.concatenate([y3[c] for c in range(nck)], axis=1)
    z = 0.5 * y * (1.0 + jnp.tanh(math.sqrt(2.0 / math.pi) * (y + 0.044715 * (y * y * y))))
    gl = jnp.dot(wglut_ref[...], z.astype(BF16), preferred_element_type=F32) + bglu_ref[...]
    so = (z * _sigmoid(gl)).T
    mix = (jnp.dot(attn_ref[0], wout_ref[:d_attn, :], preferred_element_type=F32)
           + jnp.dot(so.astype(BF16), wout_ref[d_attn:, :], preferred_element_type=F32))
    x1 = x_ref[0] + mod_ref[0, 2:3, :] * mix
    x1_ref[0] = x1
    h2 = _rms(x1, g2_ref[...]) * (1.0 + mod_ref[0, 4:5, :]) + mod_ref[0, 3:4, :]
    for j in range(SUBLANES):
        h2_ref[pl.ds(j, h2.shape[0], stride=SUBLANES), :] = h2[:, j * LANES:(j + 1) * LANES]
    def split(v):
        hi = v.astype(BF16)
        return hi, (v - hi.astype(F32)).astype(BF16)

    w_hi, w_lo = split(wrt_ref[...])
    h_hi, h_lo = split(h2)
    nt = functools.partial(lax.dot_general, dimension_numbers=NT_DIMS, preferred_element_type=F32)
    lgt_ref[...] = nt(w_hi, h_hi) + (nt(w_hi, h_lo) + nt(w_lo, h_hi)) + br_ref[...]


def _mix(x, attn, yt4, mod3, wglu_t, bglu, wout, g2, wr, br, tm=1024):
    bsz, s, d = x.shape
    tm = min(tm, s)
    da = attn.shape[-1]
    ds_ = yt4.shape[1]
    e = wr.shape[-1]
    nt = s // tm
    tok = lambda n: pl.BlockSpec((1, tm, n), lambda b, i: (b, i, 0))
    chan = pl.BlockSpec((1, ds_, tm // SSM_CHUNK, SSM_CHUNK), lambda b, i: (b, 0, i, 0))
    full = lambda a, b_: pl.BlockSpec((a, b_), lambda b, i: (0, 0))
    return pl.pallas_call(
        functools.partial(_mix_kernel, d_attn=da),
        grid=(bsz, nt),
        in_specs=[tok(d), tok(da), chan,
                  pl.BlockSpec((1, 6, d), lambda b, i: (b, 0, 0)),
                  full(ds_, ds_), full(ds_, 1), full(da + ds_, d),
                  full(1, d), full(e, d), full(e, 1)],
        out_specs=[tok(d), pl.BlockSpec((tm * SUBLANES, LANES), lambda b, i: (b * nt + i, 0)),
                   pl.BlockSpec((e, tm), lambda b, i: (0, b * nt + i))],
        out_shape=[jax.ShapeDtypeStruct((bsz, s, d), F32),
                   jax.ShapeDtypeStruct((bsz * s * SUBLANES, LANES), F32),
                   jax.ShapeDtypeStruct((e, bsz * s), F32)],
        compiler_params=_cparams(("arbitrary", "arbitrary"), VMEM_LIMIT),
        name="mix",
    )(x, attn, yt4, mod3, wglu_t, bglu.reshape(-1, 1), wout,
      g2.reshape(1, -1), wr.T, br.reshape(-1, 1))


def _route_kernel(lg_ref, idx_ref, gate_ref, rank_ref, cnt_ref, run_ref, *, tm):
    i = pl.program_id(0)

    @pl.when(i == 0)
    def _():
        run_ref[...] = jnp.zeros_like(run_ref)

    l = lg_ref[...]
    e = l.shape[0]
    sub = lax.broadcasted_iota(jnp.int32, l.shape, 0)
    vals, sels, idxs = [], [], []
    for _k in range(TOP_K):
        mx = jnp.max(l, axis=0, keepdims=True)
        ix = jnp.min(jnp.where(l == mx, sub, e), axis=0, keepdims=True)
        sel = sub == ix
        vals.append(mx)
        idxs.append(ix)
        sels.append(sel)
        l = jnp.where(sel, -jnp.inf, l)
    ex = [jnp.exp(v - vals[0]) for v in vals]
    den = ex[0] + ex[1] + ex[2] + ex[3]
    chosen = jnp.zeros(l.shape, F32)
    for sel in sels:
        chosen = chosen + jnp.where(sel, 1.0, 0.0)
    r_i = lax.broadcasted_iota(jnp.int32, (tm, tm), 0)
    c_i = lax.broadcasted_iota(jnp.int32, (tm, tm), 1)
    tri = jnp.where(r_i < c_i, 1.0, 0.0).astype(BF16)
    before = jnp.dot(chosen.astype(BF16), tri, preferred_element_type=F32) + run_ref[...]
    ksub = lax.broadcasted_iota(jnp.int32, (TOP_K, tm), 0)
    idx_o = jnp.zeros((TOP_K, tm), jnp.int32)
    gate_o = jnp.zeros((TOP_K, tm), F32)
    rank_o = jnp.zeros((TOP_K, tm), F32)
    for k in range(TOP_K):
        rk = jnp.sum(jnp.where(sels[k], before, 0.0), axis=0, keepdims=True)
        idx_o = jnp.where(ksub == k, idxs[k], idx_o)
        gate_o = jnp.where(ksub == k, ex[k] / den, gate_o)
        rank_o = jnp.where(ksub == k, rk, rank_o)
    idx_ref[...] = idx_o
    gate_ref[...] = gate_o
    rank_ref[...] = rank_o.astype(jnp.int32)
    run_ref[...] = run_ref[...] + jnp.sum(chosen, axis=1, keepdims=True)
    cnt_ref[...] = run_ref[...].astype(jnp.int32)


def _route(logits_t, tm=512):
    e, n = logits_t.shape
    tok = lambda w: pl.BlockSpec((w, tm), lambda i: (0, i))
    return pl.pallas_call(
        functools.partial(_route_kernel, tm=tm),
        grid=(n // tm,),
        in_specs=[tok(e)],
        out_specs=[tok(TOP_K), tok(TOP_K), tok(TOP_K), pl.BlockSpec((e, 1), lambda i: (0, 0))],
        out_shape=[jax.ShapeDtypeStruct((TOP_K, n), jnp.int32),
                   jax.ShapeDtypeStruct((TOP_K, n), F32),
                   jax.ShapeDtypeStruct((TOP_K, n), jnp.int32),
                   jax.ShapeDtypeStruct((e, 1), jnp.int32)],
        scratch_shapes=[pltpu.VMEM((e, 1), F32)],
        compiler_params=_cparams(("arbitrary",)),
        name="route",
    )(logits_t)


def _dispatch_kernel(pend_ref, padded_ref, nb_ref, pos_ref, h_ref, xs_ref, zero_ref, sem, zsem,
                     *, tm, rb, n_exp, nb_max):
    i = pl.program_id(0)

    def zero_copy(blk_start):
        rows = pl.ds(pl.multiple_of(blk_start * SUBLANES, rb * SUBLANES), rb * SUBLANES)
        return pltpu.make_async_copy(zero_ref, xs_ref.at[rows, :], zsem)

    @pl.when(i == 0)
    def _():
        zero_ref[...] = jnp.zeros_like(zero_ref)
        for wait in (False, True):
            for e in range(n_exp):
                @pl.when(padded_ref[e] > 0)
                def _():
                    cp = zero_copy(pend_ref[e] - rb)
                    cp.wait() if wait else cp.start()

                @pl.when(nb_ref[0] + e < nb_max)
                def _():
                    cp = zero_copy((nb_ref[0] + e) * rb)
                    cp.wait() if wait else cp.start()

    @pl.when(i > 0)
    def _():
        def issue(t, _):
            for k in range(TOP_K):
                p = pos_ref[0, 0, k * tm + t]
                pltpu.make_async_copy(h_ref.at[_tile_rows(t), :], xs_ref.at[_tile_rows(p), :],
                                      sem).start(priority=k % 2)
            return 0

        lax.fori_loop(0, tm, issue, 0, unroll=4)
        for k in range(TOP_K):
            pltpu.make_async_copy(h_ref, xs_ref.at[pl.ds(0, tm * SUBLANES), :], sem).wait()


def _dispatch(pend, padded, nblk, pos3, h2, n_rows, tm, rb):
    n = h2.shape[0] // SUBLANES
    n_exp = pend.shape[0]
    tile = lambda i, *_: (jnp.maximum(i - 1, 0), 0)
    return pl.pallas_call(
        functools.partial(_dispatch_kernel, tm=tm, rb=rb, n_exp=n_exp, nb_max=n_rows // rb),
        grid_spec=pltpu.PrefetchScalarGridSpec(
            num_scalar_prefetch=3,
            grid=(1 + n // tm,),
            in_specs=[pl.BlockSpec((1, 1, tm * TOP_K), lambda i, *_: tile(i) + (0,),
                                   memory_space=pltpu.SMEM),
                      pl.BlockSpec((tm * SUBLANES, LANES), tile)],
            out_specs=pl.BlockSpec(memory_space=pl.ANY),
            scratch_shapes=[pltpu.VMEM((rb * SUBLANES, LANES), F32), pltpu.SemaphoreType.DMA,
                            pltpu.SemaphoreType.DMA]),
        out_shape=jax.ShapeDtypeStruct((n_rows * SUBLANES, LANES), F32),
        compiler_params=_cparams(("arbitrary",)),
        name="dispatch",
    )(pend, padded, nblk, pos3, h2)


def _experts_kernel(be_ref, xb_ref, nb_ref, seg_ref, ne_ref, x_ref, w1_ref, b1_ref, w2_ref, b2_ref,
                    y_ref, w1f, w2f, w1b, w2b, sems, *, f):
    i = pl.program_id(0)
    prev = be_ref[jnp.maximum(i - 1, 0)]
    live = i < nb_ref[0]
    slot = lax.rem(seg_ref[i], 2)

    def fetch(e, s):
        return (pltpu.make_async_copy(w1_ref.at[e], w1f.at[s], sems.at[s, 0]),
                pltpu.make_async_copy(w2_ref.at[e], w2f.at[s], sems.at[s, 1]))

    @pl.when(i == 0)
    def _():
        for cp in fetch(be_ref[0], 0):
            cp.start()

    @pl.when(live & ((i == 0) | (be_ref[i] != prev)))
    def _():
        for cp in fetch(be_ref[i], slot):
            cp.wait()
        w1b[...] = w1f[slot].astype(BF16)
        w2b[...] = w2f[slot].astype(BF16)

        @pl.when(ne_ref[i] >= 0)
        def _():
            for cp in fetch(ne_ref[i], 1 - slot):
                cp.start(priority=1)

    @pl.when(live)
    def _():
        rb = x_ref.shape[0] // SUBLANES
        x = jnp.concatenate([x_ref[pl.ds(j, rb, stride=SUBLANES), :].astype(BF16)
                             for j in range(SUBLANES)], axis=1)
        gu = jnp.dot(x, w1b[...], preferred_element_type=F32) + b1_ref[0]
        gate = jnp.minimum(gu[:, :f], SWIGLU_LIMIT)
        up = jnp.clip(gu[:, f:], -SWIGLU_LIMIT, SWIGLU_LIMIT)
        glu = gate * _sigmoid(SWIGLU_ALPHA * gate)
        hmid = ((up + 1.0) * glu).astype(BF16)
        y = jnp.dot(hmid, w2b[...], preferred_element_type=F32) + b2_ref[0]
        for j in range(SUBLANES):
            y_ref[pl.ds(j, rb, stride=SUBLANES), :] = y[:, j * LANES:(j + 1) * LANES]

    @pl.when(jnp.logical_not(live))
    def _():
        y_ref[...] = jnp.zeros_like(y_ref)


def _experts(block_e, xblk, nblk, seg, nxt_e, xs, w1, b1, w2, b2, rb):
    e, d, f2 = w1.shape
    assert d == SUBLANES * LANES, "token-tile layout holds one row per (8, 128) f32 tile"
    f = f2 // 2
    nb_max = xs.shape[0] // (rb * SUBLANES)
    by_expert = lambda i, be, *_: (be[i], 0, 0)
    return pl.pallas_call(
        functools.partial(_experts_kernel, f=f),
        grid_spec=pltpu.PrefetchScalarGridSpec(
            num_scalar_prefetch=5,
            grid=(nb_max,),
            in_specs=[pl.BlockSpec((rb * SUBLANES, LANES), lambda i, be, xb, *_: (xb[i], 0)),
                      pl.BlockSpec(memory_space=pl.ANY),
                      pl.BlockSpec((1, 1, f2), by_expert),
                      pl.BlockSpec(memory_space=pl.ANY),
                      pl.BlockSpec((1, 1, d), by_expert)],
            out_specs=pl.BlockSpec((rb * SUBLANES, LANES), lambda i, *_: (i, 0)),
            scratch_shapes=[pltpu.VMEM((2, d, f2), F32), pltpu.VMEM((2, f, d), F32),
                            pltpu.VMEM((d, f2), BF16), pltpu.VMEM((f, d), BF16),
                            pltpu.SemaphoreType.DMA((2, 2))]),
        out_shape=jax.ShapeDtypeStruct(xs.shape, F32),
        compiler_params=_cparams(("arbitrary",), VMEM_LIMIT),
        name="experts",
    )(block_e, xblk, nblk, seg, nxt_e, xs, w1, b1.reshape(e, 1, f2), w2, b2.reshape(e, 1, d))


def _combine_kernel(pos_ref, nxt_ref, gate_ref, x1_ref, mod_ref, fg_ref, ys_ref, o_ref, buf, sems,
                    *, tm):
    i = pl.program_id(0)
    n_tiles = pl.num_programs(0)
    slot = lax.rem(i, 2)

    def gather(p_ref, dst_slot):
        def issue(t, _):
            for k in range(TOP_K):
                p = p_ref[0, 0, k * tm + t]
                pltpu.make_async_copy(ys_ref.at[_tile_rows(p), :],
                                      buf.at[dst_slot, k, _tile_rows(t), :],
                                      sems.at[dst_slot]).start(priority=k % 2)
            return 0

        lax.fori_loop(0, tm, issue, 0, unroll=4)

    @pl.when(i == 0)
    def _():
        gather(pos_ref, 0)

    @pl.when(i + 1 < n_tiles)
    def _():
        gather(nxt_ref, 1 - slot)

    for k in range(TOP_K):
        pltpu.make_async_copy(ys_ref.at[pl.ds(0, tm * SUBLANES), :], buf.at[slot, k],
                              sems.at[slot]).wait()
    gate = gate_ref[...]
    pieces = []
    for j in range(SUBLANES):
        acc = gate[:, 0:1] * buf[slot, 0, pl.ds(j, tm, stride=SUBLANES), :]
        for k in range(1, TOP_K):
            acc = acc + gate[:, k:k + 1] * buf[slot, k, pl.ds(j, tm, stride=SUBLANES), :]
        pieces.append(acc)
    moe = jnp.concatenate(pieces, axis=1)
    x2 = x1_ref[...] + mod_ref[0, 5:6, :] * moe
    o_ref[...] = _rms(x2, fg_ref[...])


def _combine(pos3, gates, x1, mod3, final_g, ys, tm):
    bsz, s, d = x1.shape
    n = bsz * s
    nt = s // tm
    n_tiles = n // tm
    pos_spec = lambda f: pl.BlockSpec((1, 1, tm * TOP_K), lambda i: (f(i), 0, 0),
                                      memory_space=pltpu.SMEM)
    out = pl.pallas_call(
        functools.partial(_combine_kernel, tm=tm),
        grid=(n_tiles,),
        in_specs=[pos_spec(lambda i: i),
                  pos_spec(lambda i: jnp.minimum(i + 1, n_tiles - 1)),
                  pl.BlockSpec((tm, TOP_K), lambda i: (i, 0)),
                  pl.BlockSpec((tm, d), lambda i: (i, 0)),
                  pl.BlockSpec((1, 6, d), lambda i: (i // nt, 0, 0)),
                  pl.BlockSpec((1, d), lambda i: (0, 0)),
                  pl.BlockSpec(memory_space=pl.ANY)],
        out_specs=pl.BlockSpec((tm, d), lambda i: (i, 0)),
        out_shape=jax.ShapeDtypeStruct((n, d), F32),
        scratch_shapes=[pltpu.VMEM((2, TOP_K, tm * SUBLANES, LANES), F32),
                        pltpu.SemaphoreType.DMA((2,))],
        compiler_params=_cparams(("arbitrary",), VMEM_LIMIT),
        name="combine",
    )(pos3, pos3, gates, x1.reshape(n, d), mod3, final_g.reshape(1, d), ys)
    return out.reshape(bsz, s, d)


def _layer(x, mod3, norm1_g, w_in, lq1, lk1, lq2, lk2, subln_g, ssm_a_re, ssm_a_im, ssm_log_dt,
           ssm_b_re, ssm_b_im, ssm_c_re, ssm_c_im, ssm_d, w_glu, b_glu, w_out, norm2_g,
           w_router, b_router, w1, b1, w2, b2, final_g):
    bsz, s, d = x.shape
    n = bsz * s
    d_attn = ATTN_HEADS * V_DIM
    n_qkv = 3 * d_attn

    qkv, ut4 = _inproj(x, mod3, norm1_g, w_in[:, :n_qkv].astype(BF16),
                       w_in[:, n_qkv:].T.astype(BF16))
    attn = _attention(qkv, lq1, lk1, lq2, lk2, subln_g)
    tables = _ssm_tables(ssm_a_re, ssm_a_im, ssm_log_dt, ssm_b_re, ssm_b_im, ssm_c_re, ssm_c_im,
                         SSM_CHUNK)
    yt4 = _ssm(ut4, tables, ssm_d)
    x1, h2, logits_t = _mix(x, attn, yt4, mod3, w_glu.T.astype(BF16), b_glu,
                            w_out.astype(BF16), norm2_g, w_router, b_router)

    idx, gates, rank, counts = _route(logits_t)
    rb = 512 if n * TOP_K >= 512 * N_EXPERTS else 128
    counts = counts.reshape(N_EXPERTS)
    padded = ((counts + rb - 1) // rb) * rb
    pend = jnp.cumsum(padded).astype(jnp.int32)
    pstart = pend - padded
    eids = jnp.arange(N_EXPERTS, dtype=jnp.int32)[:, None, None]
    pos = rank + jnp.sum(jnp.where(idx[None] == eids, pstart[:, None, None], 0), axis=0)
    pos = pos.astype(jnp.int32)
    nb_max = (n * TOP_K) // rb + N_EXPERTS
    n_rows = nb_max * rb
    nblk = pend[-1] // rb
    blk_ids = jnp.minimum(jnp.arange(nb_max, dtype=jnp.int32), nblk - 1)
    block_e = jnp.minimum(jnp.sum((pend[None, :] <= (blk_ids * rb)[:, None]).astype(jnp.int32), axis=1),
                          N_EXPERTS - 1).astype(jnp.int32)
    first = jnp.concatenate([jnp.ones((1,), jnp.int32),
                             (block_e[1:] != block_e[:-1]).astype(jnp.int32)])
    seg = (jnp.cumsum(first) - 1).astype(jnp.int32)
    seg_end = jnp.sum(jnp.where(block_e[:, None] == jnp.arange(N_EXPERTS, dtype=jnp.int32)[None, :],
                                pend[None, :], 0), axis=1) // rb
    nxt_e = jnp.where(seg_end < nblk, block_e[jnp.minimum(seg_end, nb_max - 1)], -1).astype(jnp.int32)
    tmd = 512
    pos3 = (pos.reshape(TOP_K, n // tmd, tmd).transpose(1, 0, 2)
            .reshape(n // tmd, 1, TOP_K * tmd))
    gates_tok = gates.T

    nblk = nblk.reshape(1).astype(jnp.int32)
    xs = _dispatch(pend, padded.astype(jnp.int32), nblk, pos3, h2, n_rows, tmd, rb)
    ys = _experts(block_e, blk_ids, nblk, seg, nxt_e, xs, w1, b1, w2, b2, rb)
    return _combine(pos3, gates_tok, x1, mod3, final_g, ys, tmd)


def kernel(x, c, w_ada, b_ada, norm1_g, w_in, lq1, lk1, lq2, lk2, subln_g, ssm_a_re, ssm_a_im,
           ssm_log_dt, ssm_b_re, ssm_b_im, ssm_c_re, ssm_c_im, ssm_d, w_glu, b_glu, w_out, norm2_g,
           w_router, b_router, w1, b1, w2, b2, final_g):
    assert w_ada.shape[0] == 1, "single-layer block"
    bsz, s, d = x.shape
    mod3 = _adaln(c, w_ada[0], b_ada[0]).reshape(bsz, 6, d)
    return _layer(x, mod3, norm1_g[0], w_in[0], lq1[0], lk1[0], lq2[0], lk2[0], subln_g[0],
                  ssm_a_re[0], ssm_a_im[0], ssm_log_dt[0], ssm_b_re[0], ssm_b_im[0], ssm_c_re[0],
                  ssm_c_im[0], ssm_d[0], w_glu[0], b_glu[0], w_out[0], norm2_g[0], w_router[0],
                  b_router[0], w1[0], b1[0], w2[0], b2[0], final_g)
```
